```python
import math
import jax, jax.numpy as jnp
from jax import lax
import numpy as np

D_MODEL = 1024
BATCH = 8
SEQ = 2048
DEPTH = 2
DEC_BATCH = 32
DEC_SEQ = 4
PAST_LEN = 8192
PAGE_SIZE = 128

SSD_HEADS = 16
SSD_HEAD_DIM = 64
SSD_WIDTH = SSD_HEADS * SSD_HEAD_DIM
SSD_GROUPS = 2
SSD_STATE = 128
SSD_CHUNK = 128
CONV_WIDTH = 4
CONV_CH = SSD_WIDTH + 2 * SSD_GROUPS * SSD_STATE
ATT_HEADS = 8
ATT_HEAD_DIM = 64
ATT_WIDTH = ATT_HEADS * ATT_HEAD_DIM
MOBA_BLOCK = 256
MOBA_TOPK = 3
MOBA_QBLOCK = 32
REL_BUCKETS = 32
REL_MAX_DIST = 128
MIX_WIDTH = SSD_WIDTH + ATT_WIDTH
SPLIT_IDX = [SSD_WIDTH, SSD_WIDTH + CONV_CH, SSD_WIDTH + CONV_CH + SSD_HEADS,
             SSD_WIDTH + CONV_CH + SSD_HEADS + ATT_WIDTH,
             SSD_WIDTH + CONV_CH + SSD_HEADS + 2 * ATT_WIDTH]
IN_PROJ = SSD_WIDTH + CONV_CH + SSD_HEADS + 3 * ATT_WIDTH
MEM_LEN = 256
CA_HEADS = 4
CA_HEAD_DIM = 128
CA_WIDTH = CA_HEADS * CA_HEAD_DIM
FF = 4 * D_MODEL
EPS = 1e-5

kernel_name = 'hymba_ssd_moba_memory_decoder_step'


def rmsnorm(x, w):
    xf = x.astype(jnp.float32)
    y = xf * lax.rsqrt(jnp.mean(xf * xf, axis=-1, keepdims=True) + EPS)
    return (y * w.astype(jnp.float32)).astype(x.dtype)


def causal_conv(xbc, conv_state, w, b):
    L = xbc.shape[1]
    xpad = jnp.concatenate([conv_state.astype(xbc.dtype), xbc], axis=1)
    y = b
    for k in range(CONV_WIDTH):
        y = y + xpad[:, k:k + L] * w[k]
    return jax.nn.silu(y), xpad[:, L:]


def ssd_scan(x, dt, a, bm, cm, s0):
    Bsz, L, H, P = x.shape
    N = bm.shape[-1]
    Q = SSD_CHUNK if L % SSD_CHUNK == 0 else L
    nc = L // Q
    xc = x.reshape(Bsz, nc, Q, H, P)
    dtc = dt.reshape(Bsz, nc, Q, H)
    bc = bm.reshape(Bsz, nc, Q, H, N)
    cc = cm.reshape(Bsz, nc, Q, H, N)
    acum = jnp.cumsum(dtc * a, axis=2)
    causal = jnp.tril(jnp.ones((Q, Q), bool))[None, None, :, :, None]
    seg = acum[:, :, :, None, :] - acum[:, :, None, :, :]
    lmat = jnp.where(causal, jnp.exp(jnp.where(causal, seg, 0.0)), 0.0)
    scores = jnp.einsum('bcihn,bcjhn->bcijh', cc, bc) * lmat
    y_diag = jnp.einsum('bcijh,bcjh,bcjhp->bcihp', scores, dtc, xc)
    decay_end = jnp.exp(acum[:, :, -1:, :] - acum)
    chunk_states = jnp.einsum('bcjhn,bcjh,bcjhp->bchpn', bc, decay_end * dtc, xc)
    chunk_decay = jnp.exp(acum[:, :, -1, :])

    def step(s, inp):
        dec, st = inp
        return dec[:, :, None, None] * s + st, s

    s_final, s_in = lax.scan(step, s0, (jnp.moveaxis(chunk_decay, 1, 0), jnp.moveaxis(chunk_states, 1, 0)))
    s_in = jnp.moveaxis(s_in, 0, 1)
    y_off = jnp.einsum('bcihn,bchpn,bcih->bcihp', cc, s_in, jnp.exp(acum))
    return (y_diag + y_off).reshape(Bsz, L, H, P), s_final


def ssd_branch(z, xbc, dt_raw, conv_state, ssm_state, conv_w, conv_b, dt_bias, a_log, d_skip, norm_w):
    Bsz, L = z.shape[:2]
    gn = SSD_GROUPS * SSD_STATE
    xbc_c, new_conv = causal_conv(xbc, conv_state, conv_w, conv_b)
    xs = xbc_c[..., :SSD_WIDTH].reshape(Bsz, L, SSD_HEADS, SSD_HEAD_DIM).astype(jnp.float32)
    rep = SSD_HEADS // SSD_GROUPS
    bm = jnp.repeat(xbc_c[..., SSD_WIDTH:SSD_WIDTH + gn].reshape(Bsz, L, SSD_GROUPS, SSD_STATE), rep, axis=2)
    cm = jnp.repeat(xbc_c[..., SSD_WIDTH + gn:].reshape(Bsz, L, SSD_GROUPS, SSD_STATE), rep, axis=2)
    dt = jax.nn.softplus(dt_raw.astype(jnp.float32) + dt_bias.astype(jnp.float32))
    a = -jnp.exp(a_log.astype(jnp.float32))
    y, new_ssm = ssd_scan(xs, dt, a, bm.astype(jnp.float32), cm.astype(jnp.float32), ssm_state.astype(jnp.float32))
    y = y + xs * d_skip.astype(jnp.float32)[:, None]
    y = y.reshape(Bsz, L, SSD_WIDTH) * jax.nn.silu(z.astype(jnp.float32))
    y = rmsnorm(y, norm_w)
    return y.astype(z.dtype), new_conv, new_ssm.astype(ssm_state.dtype)


def t5_bucket(rel):
    n = jnp.maximum(rel, 0)
    max_exact = REL_BUCKETS // 2
    nf = jnp.maximum(n, 1).astype(jnp.float32)
    large = max_exact + (jnp.log(nf / max_exact) / math.log(REL_MAX_DIST / max_exact)
                         * (REL_BUCKETS - max_exact)).astype(jnp.int32)
    large = jnp.minimum(large, REL_BUCKETS - 1)
    return jnp.where(n < max_exact, n, large)


def moba_attention(q, k_all, v_all, q_start, rel_bias):
    Bsz, L, H, D = q.shape
    T = k_all.shape[1]
    nb = -(-T // MOBA_BLOCK)
    pad = nb * MOBA_BLOCK - T

    def blocks(t):
        t = jnp.pad(t, ((0, 0), (0, pad), (0, 0), (0, 0)))
        return t.reshape(Bsz, nb, MOBA_BLOCK, H, D).transpose(0, 3, 1, 2, 4)

    kb, vb = blocks(k_all), blocks(v_all)
    kmean = jnp.mean(kb.astype(jnp.float32), axis=3)
    n_top = min(MOBA_TOPK, nb)
    QB = MOBA_QBLOCK if L % MOBA_QBLOCK == 0 else L
    nq = L // QB
    scale = ATT_HEAD_DIM ** -0.5
    b_idx = jnp.arange(Bsz)[:, None, None, None]
    h_idx = jnp.arange(H)[None, :, None, None]
    offs = jnp.arange(MOBA_BLOCK)
    is_own = (jnp.arange(n_top + 1) == n_top)[:, None]

    def attend_block(args):
        qb, pos = args
        own = pos // MOBA_BLOCK
        gate = jnp.einsum('bhqd,bhnd->bhqn', qb.astype(jnp.float32), kmean)
        fully_past = jnp.arange(nb)[None, :] < own[:, None]
        gate = jnp.where(fully_past, gate, -jnp.inf)
        _, top = lax.top_k(gate, n_top)
        own_b = jnp.broadcast_to(own[None, None, :, None], top.shape[:3] + (1,))
        sel = jnp.concatenate([top, own_b], axis=-1)
        k_sel = kb[b_idx, h_idx, sel]
        v_sel = vb[b_idx, h_idx, sel]
        key_pos = sel[..., None] * MOBA_BLOCK + offs
        p5 = pos[None, None, :, None, None]
        allowed = jnp.where(is_own, key_pos <= p5, (sel < own[:, None])[..., None])
        bias = rel_bias[t5_bucket(p5 - key_pos), h_idx[..., None]]
        logits = jnp.einsum('bhqd,bhqskd->bhqsk', qb, k_sel).astype(jnp.float32) * scale + bias.astype(jnp.float32)
        logits = jnp.where(allowed, logits, -jnp.inf)
        probs = jax.nn.softmax(logits.reshape(logits.shape[:3] + (-1,)), axis=-1).reshape(logits.shape)
        return jnp.einsum('bhqsk,bhqskd->bhqd', probs.astype(v_sel.dtype), v_sel)

    qh = jnp.moveaxis(q.transpose(0, 2, 1, 3).reshape(Bsz, H, nq, QB, D), 2, 0)
    pos = (q_start + jnp.arange(L)).reshape(nq, QB)
    out = lax.map(attend_block, (qh, pos))
    return jnp.moveaxis(out, 0, 2).reshape(Bsz, H, L, D).transpose(0, 2, 1, 3).reshape(Bsz, L, H * D)


def memory_kv(mem, norm_w, w_kv):
    Bsz = mem.shape[0]
    kv = rmsnorm(mem, norm_w) @ w_kv
    k, v = jnp.split(kv, 2, axis=-1)
    return (k.reshape(Bsz, MEM_LEN, CA_HEADS, CA_HEAD_DIM), v.reshape(Bsz, MEM_LEN, CA_HEADS, CA_HEAD_DIM))


def cross_attend(hn, mk, mv, w_q, w_o):
    Bsz, L, _ = hn.shape
    q = (hn @ w_q).reshape(Bsz, L, CA_HEADS, CA_HEAD_DIM)
    logits = jnp.einsum('blhd,bmhd->bhlm', q, mk.astype(q.dtype)).astype(jnp.float32) * CA_HEAD_DIM ** -0.5
    p = jax.nn.softmax(logits, axis=-1)
    o = jnp.einsum('bhlm,bmhd->blhd', p.astype(q.dtype), mv.astype(q.dtype)).reshape(Bsz, L, CA_WIDTH)
    return o @ w_o


def trunk_layer(h, l, q_start, conv_state, ssm_state, k_past, v_past, mem_k, mem_v, p):
    Bsz, L, _ = h.shape
    hn = rmsnorm(h, p['norm_mix'][l])
    proj = hn @ p['w_in'][l]
    z, xbc, dt_raw, q, k, v = jnp.split(proj, SPLIT_IDX, axis=-1)
    y_ssd, new_conv, new_ssm = ssd_branch(z, xbc, dt_raw, conv_state, ssm_state, p['conv_w'][l], p['conv_b'][l],
                                          p['dt_bias'][l], p['a_log'][l], p['d_skip'][l], p['ssd_norm'][l])
    q = q.reshape(Bsz, L, ATT_HEADS, ATT_HEAD_DIM)
    k = k.reshape(Bsz, L, ATT_HEADS, ATT_HEAD_DIM)
    v = v.reshape(Bsz, L, ATT_HEADS, ATT_HEAD_DIM)
    if k_past is None:
        k_all, v_all = k, v
    else:
        k_all = jnp.concatenate([k_past.astype(k.dtype), k], axis=1)
        v_all = jnp.concatenate([v_past.astype(v.dtype), v], axis=1)
    y_att = moba_attention(q, k_all, v_all, q_start, p['rel_bias'])
    h = h + jnp.concatenate([y_ssd, y_att], axis=-1) @ p['w_out'][l]
    h = h + cross_attend(rmsnorm(h, p['norm_ca'][l]), mem_k, mem_v, p['w_ca_q'][l], p['w_ca_o'][l])
    hn = rmsnorm(h, p['norm_mlp'][l])
    h = h + jnp.square(jax.nn.relu(hn @ p['w_up'][l])) @ p['w_down'][l]
    return h, new_conv, new_ssm, k, v


def gather_pages(pool, page_table):
    g = pool[page_table]
    return g.reshape(g.shape[0], -1, g.shape[3], g.shape[4])


def setup_inputs(seed: int = 0) -> dict:
    key = jax.random.key(seed)
    ks = jax.random.split(key, 32)
    f32 = jnp.float32
    n_pages = PAST_LEN // PAGE_SIZE
    n_phys = (5 * DEC_BATCH * n_pages) // 4

    def nrm(k, shape, scale=1.0):
        return jax.random.normal(k, shape, f32) * scale

    def gain(k, shape):
        return 1.0 + 0.02 * jax.random.normal(k, shape, f32)

    page_table = jax.random.permutation(ks[0], n_phys)[:DEC_BATCH * n_pages].reshape(DEC_BATCH, n_pages).astype(jnp.int32)
    dt0 = jnp.exp(jax.random.uniform(ks[1], (DEPTH, SSD_HEADS), f32, math.log(1e-3), math.log(1e-1)))
    dt_bias = dt0 + jnp.log(-jnp.expm1(-dt0))
    a_log = jnp.log(jax.random.uniform(ks[2], (DEPTH, SSD_HEADS), f32, 1.0, 16.0))
    return {
        'x_prompt': nrm(ks[3], (BATCH, SEQ, D_MODEL)),
        'x_sample': nrm(ks[4], (DEC_BATCH, DEC_SEQ, D_MODEL)),
        'cache_k': nrm(ks[5], (DEPTH, n_phys, PAGE_SIZE, ATT_HEADS, ATT_HEAD_DIM)),
        'cache_v': nrm(ks[6], (DEPTH, n_phys, PAGE_SIZE, ATT_HEADS, ATT_HEAD_DIM)),
        'cache_mem_k': nrm(ks[7], (DEPTH, DEC_BATCH, MEM_LEN, CA_HEADS, CA_HEAD_DIM)),
        'cache_mem_v': nrm(ks[8], (DEPTH, DEC_BATCH, MEM_LEN, CA_HEADS, CA_HEAD_DIM)),
        'state_conv': nrm(ks[9], (DEPTH, DEC_BATCH, CONV_WIDTH - 1, CONV_CH)),
        'state_ssm': nrm(ks[10], (DEPTH, DEC_BATCH, SSD_HEADS, SSD_HEAD_DIM, SSD_STATE), 0.1),
        'page_table': page_table,
        'mem_prompt': nrm(ks[11], (BATCH, MEM_LEN, D_MODEL)),
        'norm_mix': gain(ks[12], (DEPTH, D_MODEL)),
        'w_in': nrm(ks[13], (DEPTH, D_MODEL, IN_PROJ), D_MODEL ** -0.5),
        'conv_w': nrm(ks[14], (DEPTH, CONV_WIDTH, CONV_CH), CONV_WIDTH ** -0.5),
        'conv_b': nrm(ks[15], (DEPTH, CONV_CH), 0.02),
        'dt_bias': dt_bias,
        'a_log': a_log,
        'd_skip': gain(ks[16], (DEPTH, SSD_HEADS)),
        'ssd_norm': gain(ks[17], (DEPTH, SSD_WIDTH)),
        'rel_bias': nrm(ks[18], (REL_BUCKETS, ATT_HEADS), 0.5),
        'w_out': nrm(ks[19], (DEPTH, MIX_WIDTH, D_MODEL), MIX_WIDTH ** -0.5),
        'norm_ca': gain(ks[20], (DEPTH, D_MODEL)),
        'norm_mem': gain(ks[21], (DEPTH, D_MODEL)),
        'w_ca_q': nrm(ks[22], (DEPTH, D_MODEL, CA_WIDTH), D_MODEL ** -0.5),
        'w_ca_kv': nrm(ks[23], (DEPTH, D_MODEL, 2 * CA_WIDTH), D_MODEL ** -0.5),
        'w_ca_o': nrm(ks[24], (DEPTH, CA_WIDTH, D_MODEL), CA_WIDTH ** -0.5),
        'norm_mlp': gain(ks[25], (DEPTH, D_MODEL)),
        'w_up': nrm(ks[26], (DEPTH, D_MODEL, FF), D_MODEL ** -0.5),
        'w_down': nrm(ks[27], (DEPTH, FF, D_MODEL), FF ** -0.5),
        'norm_final': gain(ks[28], (D_MODEL,)),
    }


def reference(x_prompt, x_sample, cache_k, cache_v, cache_mem_k, cache_mem_v, state_conv, state_ssm,
              page_table, mem_prompt, norm_mix, w_in, conv_w, conv_b, dt_bias, a_log, d_skip, ssd_norm,
              rel_bias, w_out, norm_ca, norm_mem, w_ca_q, w_ca_kv, w_ca_o, norm_mlp, w_up, w_down, norm_final):
    p = {'norm_mix': norm_mix, 'w_in': w_in, 'conv_w': conv_w, 'conv_b': conv_b, 'dt_bias': dt_bias,
         'a_log': a_log, 'd_skip': d_skip, 'ssd_norm': ssd_norm, 'rel_bias': rel_bias, 'w_out': w_out,
         'norm_ca': norm_ca, 'w_ca_q': w_ca_q, 'w_ca_o': w_ca_o, 'norm_mlp': norm_mlp,
         'w_up': w_up, 'w_down': w_down}
    past_len = page_table.shape[1] * cache_k.shape[2]

    pb = x_prompt.shape[0]
    zero_conv = jnp.zeros((pb, CONV_WIDTH - 1, CONV_CH), x_prompt.dtype)
    zero_ssm = jnp.zeros((pb, SSD_HEADS, SSD_HEAD_DIM, SSD_STATE), jnp.float32)
    h = x_prompt
    pk, pv, pmk, pmv, pconv, pssm = [], [], [], [], [], []
    for l in range(DEPTH):
        mk, mv = memory_kv(mem_prompt, norm_mem[l], w_ca_kv[l])
        h, c, s, k, v = trunk_layer(h, l, 0, zero_conv, zero_ssm, None, None, mk, mv, p)
        pk.append(k); pv.append(v); pmk.append(mk); pmv.append(mv); pconv.append(c); pssm.append(s)
    y_prompt = rmsnorm(h, norm_final)

    g = x_sample
    sk, sv, sconv, sssm = [], [], [], []
    for l in range(DEPTH):
        k_past = gather_pages(cache_k[l], page_table)
        v_past = gather_pages(cache_v[l], page_table)
        g, c, s, k, v = trunk_layer(g, l, past_len, state_conv[l], state_ssm[l], k_past, v_past,
                                    cache_mem_k[l], cache_mem_v[l], p)
        sk.append(k); sv.append(v); sconv.append(c); sssm.append(s)
    y_sample = rmsnorm(g, norm_final)

    return (y_prompt, y_sample, jnp.stack(pk), jnp.stack(pv), jnp.stack(pmk), jnp.stack(pmv),
            jnp.stack(pconv), jnp.stack(pssm), jnp.stack(sk), jnp.stack(sv), jnp.stack(sconv), jnp.stack(sssm))
```

```python
import functools
import math

import numpy as np
import jax
import jax.numpy as jnp
from jax import lax
from jax.experimental import pallas as pl
from jax.experimental.pallas import tpu as pltpu

D_MODEL = 1024
SSD_HEADS = 16
SSD_HEAD_DIM = 64
SSD_WIDTH = SSD_HEADS * SSD_HEAD_DIM
SSD_GROUPS = 2
SSD_STATE = 128
SSD_CHUNK = 128
CONV_WIDTH = 4
CONV_CH = SSD_WIDTH + 2 * SSD_GROUPS * SSD_STATE
ATT_HEADS = 8
ATT_HEAD_DIM = 64
ATT_WIDTH = ATT_HEADS * ATT_HEAD_DIM
MOBA_BLOCK = 256
MOBA_TOPK = 3
REL_BUCKETS = 32
REL_MAX_DIST = 128
MEM_LEN = 256
CA_HEADS = 4
CA_HEAD_DIM = 128
CA_WIDTH = CA_HEADS * CA_HEAD_DIM
FF = 4 * D_MODEL
EPS = 1e-5

LANES = 128
VMEM_LIMIT = 56 * 1024 * 1024
NEG = -1e30

BF16 = jnp.bfloat16
F32 = jnp.float32
HI = lax.Precision.HIGHEST


def _params(*sem):
    return pltpu.CompilerParams(dimension_semantics=sem, vmem_limit_bytes=VMEM_LIMIT)


def _rms(x, gain):
    return x * lax.rsqrt(jnp.mean(x * x, axis=-1, keepdims=True) + EPS) * gain


def _dot(a, b):
    return jnp.dot(a.astype(BF16), b.astype(BF16), preferred_element_type=F32)


def _dot_t(a, b):
    return lax.dot_general(a.astype(BF16), b.astype(BF16), (((1,), (1,)), ((), ())),
                           preferred_element_type=F32)


def _full(shape):
    return pl.BlockSpec(shape, lambda *_: (0,) * len(shape))


def _norm_proj_body(n_out, x_ref, g_ref, *refs):
    w_refs, o_refs = refs[:n_out], refs[n_out:]
    xn = _rms(x_ref[...], g_ref[...]).astype(BF16)
    for w_ref, o_ref in zip(w_refs, o_refs):
        o_ref[...] = jnp.dot(xn, w_ref[...], preferred_element_type=F32).astype(o_ref.dtype)


def _norm_proj(x, gain, weights, tm):
    m, d = x.shape
    n_out = len(weights)
    return pl.pallas_call(
        functools.partial(_norm_proj_body, n_out),
        grid=(m // tm,),
        in_specs=[pl.BlockSpec((tm, d), lambda i: (i, 0)), _full((1, d))]
        + [_full(w.shape) for w in weights],
        out_specs=[pl.BlockSpec((tm, w.shape[1]), lambda i: (i, 0)) for w in weights],
        out_shape=[jax.ShapeDtypeStruct((m, w.shape[1]), F32) for w in weights],
        compiler_params=_params("parallel"),
        name="norm_proj",
    )(x, gain.reshape(1, d), *weights)


CONV_PAD = 8


def _ssd_body(real_len, has_init, xbc_ref, z_ref, dt_ref, cs_ref, s0_ref, cw_ref, cb_ref, dtb_ref,
              alog_ref, dsk_ref, nw_ref, exp_ref, y_ref, cso_ref, so_ref, xpad_ref, st_ref):
    c = pl.program_id(1)
    nc = pl.num_programs(1)
    q = SSD_CHUNK
    n_pairs = SSD_HEADS // 2

    @pl.when(c == 0)
    def _():
        if has_init:
            xpad_ref[pl.ds(0, CONV_PAD), :] = jnp.zeros((CONV_PAD, CONV_CH), F32)
            xpad_ref[pl.ds(CONV_PAD - (CONV_WIDTH - 1), CONV_WIDTH - 1), :] = cs_ref[...]
            for p in range(n_pairs):
                st_ref[:, p * LANES:(p + 1) * LANES] = s0_ref[pl.ds(p * LANES, LANES), :].T
        else:
            xpad_ref[pl.ds(0, CONV_PAD), :] = jnp.zeros((CONV_PAD, CONV_CH), F32)
            st_ref[...] = jnp.zeros_like(st_ref)

    xpad_ref[pl.ds(CONV_PAD, q), :] = xbc_ref[...]

    acc = cb_ref[...] + xpad_ref[pl.ds(CONV_PAD - 3, q), :] * cw_ref[0:1, :]
    acc = acc + xpad_ref[pl.ds(CONV_PAD - 2, q), :] * cw_ref[1:2, :]
    acc = acc + xpad_ref[pl.ds(CONV_PAD - 1, q), :] * cw_ref[2:3, :]
    acc = acc + xpad_ref[pl.ds(CONV_PAD, q), :] * cw_ref[3:4, :]
    xc = acc * (1.0 / (1.0 + jnp.exp(-acc)))
    new_tail = xpad_ref[pl.ds(CONV_PAD + real_len - (CONV_WIDTH - 1), CONV_WIDTH - 1), :]

    @pl.when(c == nc - 1)
    def _():
        cso_ref[...] = new_tail

    xpad_ref[pl.ds(CONV_PAD - (CONV_WIDTH - 1), CONV_WIDTH - 1), :] = new_tail

    xs = xc[:, :SSD_WIDTH]
    gn = SSD_GROUPS * SSD_STATE

    dtx = dt_ref[...] + dtb_ref[...]
    dt = jnp.maximum(dtx, 0.0) + jnp.log1p(jnp.exp(-jnp.abs(dtx)))
    if real_len < q:
        row = lax.broadcasted_iota(jnp.int32, (q, LANES), 0)
        dt = jnp.where(row < real_len, dt, 0.0)
    a = -jnp.exp(alog_ref[...])
    da = dt * a
    ii = lax.broadcasted_iota(jnp.int32, (q, q), 0)
    jj = lax.broadcasted_iota(jnp.int32, (q, q), 1)
    causal = ii >= jj
    tril = jnp.where(causal, 1.0, 0.0).astype(F32)
    acum = jnp.dot(tril, da, preferred_element_type=F32, precision=HI)
    acum_t = acum.T
    dt_t = dt.T
    alast = acum[q - 1:q, :]
    ea = jnp.exp(acum)
    wdec = jnp.exp(alast - acum) * dt
    alast8 = jnp.broadcast_to(alast, (8, LANES))
    dec_e = jnp.exp(jnp.dot(alast8, exp_ref[...], preferred_element_type=F32, precision=HI)[0:1, :])

    lane = lax.broadcasted_iota(jnp.int32, (q, LANES), 1)
    first = lane < SSD_HEAD_DIM
    y_parts = []
    xw_parts = []
    for g in range(SSD_GROUPS):
        bg = xc[:, SSD_WIDTH + g * SSD_STATE:SSD_WIDTH + (g + 1) * SSD_STATE]
        cg = xc[:, SSD_WIDTH + gn + g * SSD_STATE:SSD_WIDTH + gn + (g + 1) * SSD_STATE]
        bg_t = bg.T
        scores = _dot(cg, bg_t)
        per_group = SSD_HEADS // SSD_GROUPS
        for pp in range(per_group // 2):
            p = g * (per_group // 2) + pp
            x_pair = xs[:, p * LANES:(p + 1) * LANES]
            st_pair = st_ref[:, p * LANES:(p + 1) * LANES]
            rhs = jnp.concatenate([x_pair, st_pair], axis=0).astype(BF16)
            outs = []
            for k in range(2):
                h = 2 * p + k
                a_col = acum[:, h:h + 1]
                a_row = acum_t[h:h + 1, :]
                seg = jnp.where(causal, a_col - a_row, 0.0)
                m_h = jnp.where(causal, jnp.exp(seg), 0.0) * scores * dt_t[h:h + 1, :]
                c_h = cg * ea[:, h:h + 1]
                lhs = jnp.concatenate([m_h, c_h], axis=1).astype(BF16)
                outs.append(jnp.dot(lhs, rhs, preferred_element_type=F32))
            y_parts.append(jnp.where(first, outs[0], outs[1]))
            w_pair = jnp.where(first, wdec[:, 2 * p:2 * p + 1], wdec[:, 2 * p + 1:2 * p + 2])
            xw_parts.append(x_pair * w_pair)
        half = SSD_WIDTH // SSD_GROUPS
        xw_g = jnp.concatenate(xw_parts[-(per_group // 2):], axis=1)
        upd = _dot(bg_t, xw_g)
        st_ref[:, g * half:(g + 1) * half] = (
            st_ref[:, g * half:(g + 1) * half] * dec_e[:, g * half:(g + 1) * half] + upd)

    y = jnp.concatenate(y_parts, axis=1) + xs * dsk_ref[...]
    zz = z_ref[...]
    y = y * (zz * (1.0 / (1.0 + jnp.exp(-zz))))
    y_ref[...] = _rms(y, nw_ref[...])

    @pl.when(c == nc - 1)
    def _():
        for p in range(n_pairs):
            so_ref[pl.ds(p * LANES, LANES), :] = st_ref[:, p * LANES:(p + 1) * LANES].T


def _ssd(xbc, z, dt, conv_state, ssm_state, conv_w, conv_b, dt_bias, a_log, d_skip, norm_w, bsz, seq):
    has_init = conv_state is not None
    real_len = min(seq, SSD_CHUNK)
    nc = max(seq // SSD_CHUNK, 1)
    if not has_init:
        conv_state = jnp.zeros((bsz, CONV_WIDTH - 1, CONV_CH), F32)
        ssm_state = jnp.zeros((bsz, SSD_HEADS, SSD_HEAD_DIM, SSD_STATE), F32)
    s0 = ssm_state.reshape(bsz, SSD_WIDTH, SSD_STATE)
    pad_h = LANES - SSD_HEADS
    dtb = jnp.pad(dt_bias, (0, pad_h)).reshape(1, LANES)
    alog = jnp.pad(a_log, (0, pad_h)).reshape(1, LANES)
    dsk = jnp.repeat(d_skip, SSD_HEAD_DIM).reshape(1, SSD_WIDTH)
    expand = (np.arange(LANES)[:, None] == (np.arange(SSD_WIDTH) // SSD_HEAD_DIM)[None, :]).astype(np.float32)
    xbc3 = xbc.reshape(bsz, seq, CONV_CH)
    z3 = z.reshape(bsz, seq, SSD_WIDTH)
    dt3 = dt.reshape(bsz, seq, LANES)
    seq_p = nc * SSD_CHUNK
    if seq_p != seq:
        pad = ((0, 0), (0, seq_p - seq), (0, 0))
        xbc3, z3, dt3 = jnp.pad(xbc3, pad), jnp.pad(z3, pad), jnp.pad(dt3, pad)

    def tok(width):
        return pl.BlockSpec((None, SSD_CHUNK, width), lambda b, c: (b, c, 0))

    y, cso, so = pl.pallas_call(
        functools.partial(_ssd_body, real_len, has_init),
        grid=(bsz, nc),
        in_specs=[tok(CONV_CH), tok(SSD_WIDTH), tok(LANES),
                  pl.BlockSpec((None, CONV_WIDTH - 1, CONV_CH), lambda b, c: (b, 0, 0)),
                  pl.BlockSpec((None, SSD_WIDTH, SSD_STATE), lambda b, c: (b, 0, 0)),
                  _full((CONV_WIDTH, CONV_CH)), _full((1, CONV_CH)), _full((1, LANES)), _full((1, LANES)),
                  _full((1, SSD_WIDTH)), _full((1, SSD_WIDTH)), _full((LANES, SSD_WIDTH))],
        out_specs=[tok(SSD_WIDTH),
                   pl.BlockSpec((None, CONV_WIDTH - 1, CONV_CH), lambda b, c: (b, 0, 0)),
                   pl.BlockSpec((None, SSD_WIDTH, SSD_STATE), lambda b, c: (b, 0, 0))],
        out_shape=[jax.ShapeDtypeStruct((bsz, seq_p, SSD_WIDTH), F32),
                   jax.ShapeDtypeStruct((bsz, CONV_WIDTH - 1, CONV_CH), F32),
                   jax.ShapeDtypeStruct((bsz, SSD_WIDTH, SSD_STATE), F32)],
        scratch_shapes=[pltpu.VMEM((CONV_PAD + SSD_CHUNK, CONV_CH), F32),
                        pltpu.VMEM((SSD_STATE, SSD_WIDTH), F32)],
        compiler_params=_params("parallel", "arbitrary"),
        name="ssd_scan",
    )(xbc3, z3, dt3, conv_state, s0, conv_w, conv_b.reshape(1, CONV_CH), dtb, alog, dsk,
      norm_w.reshape(1, SSD_WIDTH), jnp.asarray(expand))
    return (y[:, :seq].reshape(bsz * seq, SSD_WIDTH), cso,
            so.reshape(bsz, SSD_HEADS, SSD_HEAD_DIM, SSD_STATE))


def _t5_bucket_np(rel):
    n = np.maximum(rel, 0)
    max_exact = REL_BUCKETS // 2
    nf = np.maximum(n, 1).astype(np.float32)
    large = max_exact + (np.log(nf / np.float32(max_exact)) / np.float32(math.log(REL_MAX_DIST / max_exact))
                         * np.float32(REL_BUCKETS - max_exact)).astype(np.int32)
    large = np.minimum(large, REL_BUCKETS - 1)
    return np.where(n < max_exact, n, large).astype(np.int32)


def _bias_tiles_body(bk_ref, rb_ref, o_ref):
    h = pl.program_id(0)
    ii = lax.broadcasted_iota(jnp.int32, (MOBA_BLOCK, MOBA_BLOCK), 0)
    jj = lax.broadcasted_iota(jnp.int32, (MOBA_BLOCK, MOBA_BLOCK), 1)
    for t in range(2):
        bk = bk_ref[t]
        acc = jnp.zeros((MOBA_BLOCK, MOBA_BLOCK), F32)
        for u in range(REL_BUCKETS):
            acc = jnp.where(bk == u, rb_ref[u, h], acc)
        if t == 0:
            acc = jnp.where(ii >= jj, acc, NEG)
        o_ref[t] = acc


def _bias_tiles(rel_bias):
    i = np.arange(MOBA_BLOCK)[:, None]
    j = np.arange(MOBA_BLOCK)[None, :]
    buckets = np.stack([_t5_bucket_np(i - j), _t5_bucket_np(MOBA_BLOCK + i - j)])
    return pl.pallas_call(
        _bias_tiles_body,
        grid=(ATT_HEADS,),
        in_specs=[_full((2, MOBA_BLOCK, MOBA_BLOCK)),
                  pl.BlockSpec(memory_space=pltpu.SMEM)],
        out_specs=pl.BlockSpec((None, 2, MOBA_BLOCK, MOBA_BLOCK), lambda h: (h, 0, 0, 0)),
        out_shape=jax.ShapeDtypeStruct((ATT_HEADS, 2, MOBA_BLOCK, MOBA_BLOCK), F32),
        compiler_params=_params("parallel"),
        name="bias_tiles",
    )(jnp.asarray(buckets), rel_bias)


def _block_rank(g, blk):
    rank = jnp.zeros(g.shape, jnp.int32)
    for m in range(g.shape[1]):
        gm = g[:, m:m + 1]
        rank = rank + ((gm > g) | ((gm == g) & (m < blk))).astype(jnp.int32)
    return rank


def _moba_prompt_body(nblk, q_ref, k_ref, v_ref, bias_ref, cfar_ref, o_ref, km_ref):
    pair = pl.program_id(0)
    blk_rows = MOBA_BLOCK
    km_ref[...] = jnp.zeros_like(km_ref)
    for n in range(nblk):
        km_ref[n:n + 1, :] = jnp.mean(k_ref[pl.ds(n * blk_rows, blk_rows), :], axis=0, keepdims=True)
    kmean = km_ref[...]
    lane = lax.broadcasted_iota(jnp.int32, (blk_rows, LANES), 1)
    blk = lax.broadcasted_iota(jnp.int32, (blk_rows, nblk), 1)
    scale = ATT_HEAD_DIM ** -0.5

    def q_tile(i, carry):
        qs = pl.multiple_of(i * blk_rows, blk_rows)
        qt = q_ref[pl.ds(qs, blk_rows), :]
        outs = []
        for kk in range(2):
            hm = (lane >= kk * ATT_HEAD_DIM) & (lane < (kk + 1) * ATT_HEAD_DIM)
            qh = jnp.where(hm, qt, 0.0)
            gate = lax.dot_general(qh, kmean[:nblk], (((1,), (1,)), ((), ())),
                                   preferred_element_type=F32, precision=HI)
            past = blk < i
            gate = jnp.where(past, gate, NEG)
            sel = past & (_block_rank(gate, blk) < MOBA_TOPK)
            pen = jnp.where(sel, 0.0, NEG)
            qs_h = (qh * scale).astype(BF16)
            cfar = cfar_ref[2 * pair + kk]

            def update(st, s, vb):
                m_run, l_run, acc = st
                m_new = jnp.maximum(m_run, jnp.max(s, axis=1, keepdims=True))
                alpha = jnp.exp(m_run - m_new)
                pr = jnp.exp(s - m_new)
                l_new = alpha * l_run + jnp.sum(pr, axis=1, keepdims=True)
                acc = alpha * acc + jnp.dot(pr.astype(BF16), vb.astype(BF16), preferred_element_type=F32)
                return m_new, l_new, acc

            def past_block(n, st):
                ks = pl.multiple_of(n * blk_rows, blk_rows)
                kb = k_ref[pl.ds(ks, blk_rows), :]
                vb = v_ref[pl.ds(ks, blk_rows), :]
                s = lax.dot_general(qs_h, kb.astype(BF16), (((1,), (1,)), ((), ())),
                                    preferred_element_type=F32)
                bias = jnp.where(n == i - 1, bias_ref[kk, 1], cfar)
                col = jnp.sum(jnp.where(blk == n, pen, 0.0), axis=1, keepdims=True)
                return update(st, s + bias + col, vb)

            st0 = (jnp.full((blk_rows, 1), NEG, F32), jnp.zeros((blk_rows, 1), F32),
                   jnp.zeros((blk_rows, LANES), F32))
            st = lax.fori_loop(0, i, past_block, st0)
            kb = k_ref[pl.ds(qs, blk_rows), :]
            vb = v_ref[pl.ds(qs, blk_rows), :]
            s = lax.dot_general(qs_h, kb.astype(BF16), (((1,), (1,)), ((), ())),
                                preferred_element_type=F32)
            _, l_fin, acc = update(st, s + bias_ref[kk, 0], vb)
            outs.append(acc / l_fin)
        o_ref[pl.ds(qs, blk_rows), :] = jnp.where(lane < ATT_HEAD_DIM, outs[0], outs[1])
        return carry

    lax.fori_loop(0, nblk, q_tile, 0)


def _moba_prompt(q, k, v, bias_tiles, cfar, bsz, seq):
    nblk = seq // MOBA_BLOCK
    n_pairs = ATT_HEADS // 2
    tok = pl.BlockSpec((seq, LANES), lambda p, b: (b, p))
    return pl.pallas_call(
        functools.partial(_moba_prompt_body, nblk),
        grid=(n_pairs, bsz),
        in_specs=[tok, tok, tok,
                  pl.BlockSpec((2, 2, MOBA_BLOCK, MOBA_BLOCK), lambda p, b: (p, 0, 0, 0)),
                  pl.BlockSpec(memory_space=pltpu.SMEM)],
        out_specs=tok,
        out_shape=jax.ShapeDtypeStruct((bsz * seq, ATT_WIDTH), F32),
        scratch_shapes=[pltpu.VMEM((max(nblk, 8), LANES), F32)],
        compiler_params=_params("parallel", "parallel"),
        name="moba_prompt",
    )(q, k, v, bias_tiles, cfar)


PAGES_PER_STEP = 16
ROWS8 = 8


def _page_sums_body(x_ref, o_ref):
    o_ref[...] = jnp.sum(x_ref[...], axis=1)


def _page_sums(cache):
    depth, n_phys, page, heads, dim = cache.shape
    return pl.pallas_call(
        _page_sums_body,
        grid=(depth, n_phys // PAGES_PER_STEP),
        in_specs=[pl.BlockSpec((None, PAGES_PER_STEP, page, heads, dim), lambda l, i: (l, i, 0, 0, 0))],
        out_specs=pl.BlockSpec((None, PAGES_PER_STEP, heads, dim), lambda l, i: (l, i, 0, 0)),
        out_shape=jax.ShapeDtypeStruct((depth, n_phys, heads, dim), F32),
        compiler_params=_params("parallel", "parallel"),
        name="page_sums",
    )(cache)


def _sample_select_body(n_past, pages_per_blk, page_size, pt_ref, q_ref, ps_ref, o_ref, km_ref):
    b = pl.program_id(0)
    inv = 1.0 / (pages_per_blk * page_size)
    for n in range(n_past):
        acc = ps_ref[pt_ref[b, n * pages_per_blk]]
        for r in range(1, pages_per_blk):
            acc = acc + ps_ref[pt_ref[b, n * pages_per_blk + r]]
        km_ref[pl.ds(n * ATT_HEADS, ATT_HEADS), :] = acc * inv
    blk = lax.broadcasted_iota(jnp.int32, (ROWS8, n_past), 1)
    lane = lax.broadcasted_iota(jnp.int32, (ROWS8, LANES), 1)
    q = q_ref[...]
    for h in range(ATT_HEADS):
        qh = q[:, h * ATT_HEAD_DIM:(h + 1) * ATT_HEAD_DIM]
        kmh = km_ref[pl.ds(h, n_past, stride=ATT_HEADS), :]
        gate = lax.dot_general(qh, kmh, (((1,), (1,)), ((), ())),
                               preferred_element_type=F32, precision=HI)
        rank = _block_rank(gate, blk)
        out = jnp.zeros((ROWS8, LANES), jnp.int32)
        for s in range(MOBA_TOPK):
            idx = jnp.sum(jnp.where(rank == s, blk, 0), axis=1, keepdims=True)
            out = jnp.where(lane == s, idx, out)
        o_ref[h] = out


def _sample_select(q8, page_sums, page_table, page_size):
    bsz = q8.shape[0]
    n_phys = page_sums.shape[0]
    pages_per_blk = MOBA_BLOCK // page_size
    n_past = page_table.shape[1] // pages_per_blk
    grid_spec = pltpu.PrefetchScalarGridSpec(
        num_scalar_prefetch=1,
        grid=(bsz,),
        in_specs=[pl.BlockSpec((None, ROWS8, ATT_WIDTH), lambda b, pt: (b, 0, 0)),
                  pl.BlockSpec((n_phys, ATT_HEADS, ATT_HEAD_DIM), lambda b, pt: (0, 0, 0))],
        out_specs=pl.BlockSpec((None, ATT_HEADS, ROWS8, LANES), lambda b, pt: (b, 0, 0, 0)),
        scratch_shapes=[pltpu.VMEM((n_past * ATT_HEADS, ATT_HEAD_DIM), F32)],
    )
    return pl.pallas_call(
        functools.partial(_sample_select_body, n_past, pages_per_blk, page_size),
        grid_spec=grid_spec,
        out_shape=jax.ShapeDtypeStruct((bsz, ATT_HEADS, ROWS8, LANES), jnp.int32),
        compiler_params=_params("arbitrary"),
        name="sample_select",
    )(page_table, q8, page_sums)


def _sample_attn_body(layer, n_new, n_past, pages_per_blk, page_size, pt_ref, sel_ref, q_ref, kn_ref, vn_ref,
                      tprev_ref, town_ref, cfar_ref, ck_ref, cv_ref, o_ref, kbuf, vbuf, ksem, vsem):
    b = pl.program_id(0)
    n_slots = n_new * MOBA_TOPK
    past_rows = n_slots * MOBA_BLOCK
    last_blk = n_past - 1

    def page_copies(h, slot, r):
        blk = sel_ref[(b * ATT_HEADS + h) * n_slots + slot]
        page = pt_ref[b, blk * pages_per_blk + r]
        dst = pl.ds(slot * MOBA_BLOCK + r * page_size, page_size)
        return (pltpu.make_async_copy(ck_ref.at[layer, page, :, h, :], kbuf.at[h, dst, :], ksem),
                pltpu.make_async_copy(cv_ref.at[layer, page, :, h, :], vbuf.at[h, dst, :], vsem))

    def for_all_copies(fn):
        def body(i, carry):
            h = i // n_slots
            slot = i % n_slots
            for r in range(pages_per_blk):
                kc, vc = page_copies(h, slot, r)
                fn(kc)
                fn(vc)
            return carry
        lax.fori_loop(0, ATT_HEADS * n_slots, body, 0)

    for_all_copies(lambda c: c.start())

    q = q_ref[...]
    kn = kn_ref[...]
    vn = vn_ref[...]
    tail = LANES - ROWS8
    for h in range(ATT_HEADS):
        sl = slice(h * ATT_HEAD_DIM, (h + 1) * ATT_HEAD_DIM)
        kbuf[h, pl.ds(past_rows, ROWS8), :] = kn[:, sl]
        vbuf[h, pl.ds(past_rows, ROWS8), :] = vn[:, sl]
        kbuf[h, pl.ds(past_rows + ROWS8, tail), :] = jnp.zeros((tail, ATT_HEAD_DIM), F32)
        vbuf[h, pl.ds(past_rows + ROWS8, tail), :] = jnp.zeros((tail, ATT_HEAD_DIM), F32)

    for_all_copies(lambda c: c.wait())

    rows = lax.broadcasted_iota(jnp.int32, (ROWS8, past_rows), 0)
    keyi = lax.broadcasted_iota(jnp.int32, (ROWS8, past_rows), 1)
    mine = (keyi // (MOBA_TOPK * MOBA_BLOCK)) == rows
    col_own = lax.broadcasted_iota(jnp.int32, (ROWS8, LANES), 1)
    outs = []
    for h in range(ATT_HEADS):
        sl = slice(h * ATT_HEAD_DIM, (h + 1) * ATT_HEAD_DIM)
        qh = (q[:, sl] * (ATT_HEAD_DIM ** -0.5)).astype(BF16)
        s = lax.dot_general(qh, kbuf[h].astype(BF16), (((1,), (1,)), ((), ())),
                            preferred_element_type=F32)
        cfar = cfar_ref[h]
        pieces = []
        for slot in range(n_slots):
            blk = sel_ref[(b * ATT_HEADS + h) * n_slots + slot]
            pieces.append(jnp.where(blk == last_blk, tprev_ref[h], cfar))
        bias_past = jnp.concatenate(pieces, axis=1)
        s_past = jnp.where(mine, s[:, :past_rows] + bias_past, NEG)
        s_own = jnp.where(col_own < n_new, s[:, past_rows:] + town_ref[h], NEG)
        m = jnp.maximum(jnp.max(s_past, axis=1, keepdims=True), jnp.max(s_own, axis=1, keepdims=True))
        p = jnp.concatenate([jnp.exp(s_past - m), jnp.exp(s_own - m)], axis=1)
        l = jnp.sum(p, axis=1, keepdims=True)
        outs.append(jnp.dot(p.astype(BF16), vbuf[h].astype(BF16), preferred_element_type=F32) / l)
    o_ref[...] = jnp.concatenate(outs, axis=1)


def _sample_attn(layer, q8, kn8, vn8, sel_flat, page_table, cache_k, cache_v, bias_tiles, cfar, n_new):
    bsz = q8.shape[0]
    page_size = cache_k.shape[2]
    pages_per_blk = MOBA_BLOCK // page_size
    n_past = page_table.shape[1] // pages_per_blk
    n_slots = n_new * MOBA_TOPK
    buf_rows = n_slots * MOBA_BLOCK + LANES
    tprev = bias_tiles[:, 1, :ROWS8, :]
    town = bias_tiles[:, 0, :ROWS8, :LANES]
    tok = pl.BlockSpec((None, ROWS8, ATT_WIDTH), lambda b, pt, sel: (b, 0, 0))
    grid_spec = pltpu.PrefetchScalarGridSpec(
        num_scalar_prefetch=2,
        grid=(bsz,),
        in_specs=[tok, tok, tok,
                  pl.BlockSpec((ATT_HEADS, ROWS8, MOBA_BLOCK), lambda b, pt, sel: (0, 0, 0)),
                  pl.BlockSpec((ATT_HEADS, ROWS8, LANES), lambda b, pt, sel: (0, 0, 0)),
                  pl.BlockSpec(memory_space=pltpu.SMEM),
                  pl.BlockSpec(memory_space=pl.ANY),
                  pl.BlockSpec(memory_space=pl.ANY)],
        out_specs=tok,
        scratch_shapes=[pltpu.VMEM((ATT_HEADS, buf_rows, ATT_HEAD_DIM), F32),
                        pltpu.VMEM((ATT_HEADS, buf_rows, ATT_HEAD_DIM), F32),
                        pltpu.SemaphoreType.DMA(()), pltpu.SemaphoreType.DMA(())],
    )
    return pl.pallas_call(
        functools.partial(_sample_attn_body, layer, n_new, n_past, pages_per_blk, page_size),
        grid_spec=grid_spec,
        out_shape=jax.ShapeDtypeStruct((bsz, ROWS8, ATT_WIDTH), F32),
        compiler_params=_params("arbitrary"),
        name="sample_attn",
    )(page_table, sel_flat, q8, kn8, vn8, tprev, town, cfar, cache_k, cache_v)


def _mix_ca_body(rows_per_batch, h_ref, ys_ref, ya_ref, mk_ref, mv_ref, wo1_ref, wo2_ref, g_ref,
                 wq_ref, wco_ref, o_ref):
    tm = h_ref.shape[0]
    h1 = (h_ref[...] + jnp.dot(ys_ref[...].astype(BF16), wo1_ref[...], preferred_element_type=F32)
          + jnp.dot(ya_ref[...].astype(BF16), wo2_ref[...], preferred_element_type=F32))
    hn = _rms(h1, g_ref[...]).astype(BF16)
    qq = jnp.dot(hn, wq_ref[...], preferred_element_type=F32) * (CA_HEAD_DIM ** -0.5)
    nkeys = mk_ref.shape[0] * mk_ref.shape[1]
    mk = mk_ref[...].reshape(nkeys, CA_WIDTH)
    mv = mv_ref[...].reshape(nkeys, CA_WIDTH)
    if rows_per_batch < tm:
        r = lax.broadcasted_iota(jnp.int32, (tm, nkeys), 0) // rows_per_batch
        c = lax.broadcasted_iota(jnp.int32, (tm, nkeys), 1) // MEM_LEN
        same = r == c
    outs = []
    for hd in range(CA_HEADS):
        sl = slice(hd * CA_HEAD_DIM, (hd + 1) * CA_HEAD_DIM)
        s = _dot_t(qq[:, sl], mk[:, sl])
        if rows_per_batch < tm:
            s = jnp.where(same, s, NEG)
        s = s - jnp.max(s, axis=1, keepdims=True)
        p = jnp.exp(s)
        p = p / jnp.sum(p, axis=1, keepdims=True)
        outs.append(_dot(p, mv[:, sl]))
    o = jnp.concatenate(outs, axis=1).astype(BF16)
    o_ref[...] = h1 + jnp.dot(o, wco_ref[...], preferred_element_type=F32)


def _mix_ca(h, ys, ya, mk, mv, wo1, wo2, g_ca, wq, wco, rows_per_batch, tm):
    m = h.shape[0]
    nbat = max(tm // rows_per_batch, 1)
    per = rows_per_batch // tm if rows_per_batch >= tm else 1

    def mem_map(i):
        return (i // per, 0, 0)

    def row(width):
        return pl.BlockSpec((tm, width), lambda i: (i, 0))

    return pl.pallas_call(
        functools.partial(_mix_ca_body, rows_per_batch),
        grid=(m // tm,),
        in_specs=[row(D_MODEL), row(SSD_WIDTH), row(ATT_WIDTH),
                  pl.BlockSpec((nbat, MEM_LEN, CA_WIDTH), mem_map),
                  pl.BlockSpec((nbat, MEM_LEN, CA_WIDTH), mem_map),
                  _full(wo1.shape), _full(wo2.shape), _full((1, D_MODEL)), _full(wq.shape), _full(wco.shape)],
        out_specs=row(D_MODEL),
        out_shape=jax.ShapeDtypeStruct((m, D_MODEL), F32),
        compiler_params=_params("parallel"),
        name="mix_ca",
    )(h, ys, ya, mk, mv, wo1, wo2, g_ca.reshape(1, D_MODEL), wq, wco)


FF_CHUNK = 1024


def _mlp_body(final, h_ref, g_ref, wu_ref, wd_ref, gf_ref, o_ref):
    h = h_ref[...]
    xn = _rms(h, g_ref[...]).astype(BF16)
    acc = h
    for c in range(FF // FF_CHUNK):
        u = jnp.dot(xn, wu_ref[:, c * FF_CHUNK:(c + 1) * FF_CHUNK], preferred_element_type=F32)
        u = jnp.maximum(u, 0.0)
        acc = acc + jnp.dot((u * u).astype(BF16), wd_ref[c * FF_CHUNK:(c + 1) * FF_CHUNK, :],
                            preferred_element_type=F32)
    o_ref[...] = _rms(acc, gf_ref[...]) if final else acc


def _mlp(h, g, wu, wd, g_final, final, tm):
    m = h.shape[0]
    row = pl.BlockSpec((tm, D_MODEL), lambda i: (i, 0))
    return pl.pallas_call(
        functools.partial(_mlp_body, final),
        grid=(m // tm,),
        in_specs=[row, _full((1, D_MODEL)), _full(wu.shape), _full(wd.shape), _full((1, D_MODEL))],
        out_specs=row,
        out_shape=jax.ShapeDtypeStruct((m, D_MODEL), F32),
        compiler_params=_params("parallel"),
        name="mlp",
    )(h, g.reshape(1, D_MODEL), wu, wd, g_final.reshape(1, D_MODEL))


def _layer_weights(l, w_in, w_out, w_ca_q, w_ca_kv, w_ca_o, w_up, w_down):
    w = w_in[l]
    o = 0
    parts = []
    for width in (SSD_WIDTH, CONV_CH, SSD_HEADS, ATT_WIDTH, ATT_WIDTH, ATT_WIDTH):
        parts.append(w[:, o:o + width])
        o += width
    parts[2] = jnp.pad(parts[2], ((0, 0), (0, LANES - SSD_HEADS)))
    wo = w_out[l]
    return dict(
        w_in=[p.astype(BF16) for p in parts],
        wo1=wo[:SSD_WIDTH].astype(BF16), wo2=wo[SSD_WIDTH:].astype(BF16),
        wq=w_ca_q[l].astype(BF16), wco=w_ca_o[l].astype(BF16),
        wkv=[w_ca_kv[l][:, :CA_WIDTH].astype(BF16), w_ca_kv[l][:, CA_WIDTH:].astype(BF16)],
        wu=w_up[l].astype(BF16), wd=w_down[l].astype(BF16))


def kernel(x_prompt, x_sample, cache_k, cache_v, cache_mem_k, cache_mem_v, state_conv, state_ssm,
           page_table, mem_prompt, norm_mix, w_in, conv_w, conv_b, dt_bias, a_log, d_skip, ssd_norm,
           rel_bias, w_out, norm_ca, norm_mem, w_ca_q, w_ca_kv, w_ca_o, norm_mlp, w_up, w_down, norm_final):
    depth = w_in.shape[0]
    pb, seq, _ = x_prompt.shape
    sb, dseq, _ = x_sample.shape
    page_size = cache_k.shape[2]
    tm = 512

    bias_tiles = _bias_tiles(rel_bias)
    cfar = rel_bias[REL_BUCKETS - 1]
    page_sums = _page_sums(cache_k)
    mem2d = mem_prompt.reshape(pb * MEM_LEN, D_MODEL)

    def pad8(a):
        return jnp.pad(a.reshape(sb, dseq, ATT_WIDTH), ((0, 0), (0, ROWS8 - dseq), (0, 0)))

    h = x_prompt.reshape(pb * seq, D_MODEL)
    g = x_sample.reshape(sb * dseq, D_MODEL)
    outs = {n: [] for n in ("pk", "pv", "pmk", "pmv", "pconv", "pssm", "sk", "sv", "sconv", "sssm")}
    for l in range(depth):
        lw = _layer_weights(l, w_in, w_out, w_ca_q, w_ca_kv, w_ca_o, w_up, w_down)
        last = l == depth - 1

        z, xbc, dt, q, k, v = _norm_proj(h, norm_mix[l], lw["w_in"], tm)
        y_ssd, c_new, s_new = _ssd(xbc, z, dt, None, None, conv_w[l], conv_b[l], dt_bias[l], a_log[l],
                                   d_skip[l], ssd_norm[l], pb, seq)
        y_att = _moba_prompt(q, k, v, bias_tiles, cfar, pb, seq)
        mk, mv = _norm_proj(mem2d, norm_mem[l], lw["wkv"], min(tm, pb * MEM_LEN))
        h = _mix_ca(h, y_ssd, y_att, mk.reshape(pb, MEM_LEN, CA_WIDTH), mv.reshape(pb, MEM_LEN, CA_WIDTH),
                    lw["wo1"], lw["wo2"], norm_ca[l], lw["wq"], lw["wco"], seq, tm)
        h = _mlp(h, norm_mlp[l], lw["wu"], lw["wd"], norm_final, last, tm)
        outs["pk"].append(k.reshape(pb, seq, ATT_HEADS, ATT_HEAD_DIM))
        outs["pv"].append(v.reshape(pb, seq, ATT_HEADS, ATT_HEAD_DIM))
        outs["pmk"].append(mk.reshape(pb, MEM_LEN, CA_HEADS, CA_HEAD_DIM))
        outs["pmv"].append(mv.reshape(pb, MEM_LEN, CA_HEADS, CA_HEAD_DIM))
        outs["pconv"].append(c_new)
        outs["pssm"].append(s_new)

        ms = sb * dseq
        z, xbc, dt, q, k, v = _norm_proj(g, norm_mix[l], lw["w_in"], ms)
        y_ssd, c_new, s_new = _ssd(xbc, z, dt, state_conv[l], state_ssm[l], conv_w[l], conv_b[l], dt_bias[l],
                                   a_log[l], d_skip[l], ssd_norm[l], sb, dseq)
        q8 = pad8(q)
        sel = _sample_select(q8, page_sums[l], page_table, page_size)
        sel_flat = sel[:, :, :dseq, :MOBA_TOPK].reshape(-1)
        y_att = _sample_attn(l, q8, pad8(k), pad8(v), sel_flat, page_table, cache_k, cache_v, bias_tiles,
                             cfar, dseq)[:, :dseq].reshape(ms, ATT_WIDTH)
        g = _mix_ca(g, y_ssd, y_att, cache_mem_k[l].reshape(sb, MEM_LEN, CA_WIDTH),
                    cache_mem_v[l].reshape(sb, MEM_LEN, CA_WIDTH), lw["wo1"], lw["wo2"], norm_ca[l],
                    lw["wq"], lw["wco"], dseq, min(32, ms))
        g = _mlp(g, norm_mlp[l], lw["wu"], lw["wd"], norm_final, last, ms)
        outs["sk"].append(k.reshape(sb, dseq, ATT_HEADS, ATT_HEAD_DIM))
        outs["sv"].append(v.reshape(sb, dseq, ATT_HEADS, ATT_HEAD_DIM))
        outs["sconv"].append(c_new)
        outs["sssm"].append(s_new)

    st = {n: jnp.stack(vs) for n, vs in outs.items()}
    return (h.reshape(pb, seq, D_MODEL), g.reshape(sb, dseq, D_MODEL), st["pk"], st["pv"], st["pmk"], st["pmv"],
            st["pconv"], st["pssm"], st["sk"], st["sv"], st["sconv"], st["sssm"])
```

```python
import functools
import math

import numpy as np
import jax
import jax.numpy as jnp
from jax import lax
from jax.experimental import pallas as pl
from jax.experimental.pallas import tpu as pltpu

D_MODEL = 1024
SSD_HEADS = 16
SSD_HEAD_DIM = 64
SSD_WIDTH = SSD_HEADS * SSD_HEAD_DIM
SSD_GROUPS = 2
SSD_STATE = 128
SSD_CHUNK = 128
CONV_WIDTH = 4
CONV_CH = SSD_WIDTH + 2 * SSD_GROUPS * SSD_STATE
ATT_HEADS = 8
ATT_HEAD_DIM = 64
ATT_WIDTH = ATT_HEADS * ATT_HEAD_DIM
MOBA_BLOCK = 256
MOBA_TOPK = 3
REL_BUCKETS = 32
REL_MAX_DIST = 128
MEM_LEN = 256
CA_HEADS = 4
CA_HEAD_DIM = 128
CA_WIDTH = CA_HEADS * CA_HEAD_DIM
FF = 4 * D_MODEL
EPS = 1e-5

LANES = 128
VMEM_LIMIT = 56 * 1024 * 1024
NEG = -1e30

BF16 = jnp.bfloat16
F32 = jnp.float32
HI = lax.Precision.HIGHEST


def _params(*sem):
    return pltpu.CompilerParams(dimension_semantics=sem, vmem_limit_bytes=VMEM_LIMIT)


def _rms(x, gain):
    return x * lax.rsqrt(jnp.mean(x * x, axis=-1, keepdims=True) + EPS) * gain


def _dot(a, b):
    return jnp.dot(a.astype(BF16), b.astype(BF16), preferred_element_type=F32)


def _dot_t(a, b):
    return lax.dot_general(a.astype(BF16), b.astype(BF16), (((1,), (1,)), ((), ())),
                           preferred_element_type=F32)


def _full(shape):
    return pl.BlockSpec(shape, lambda *_: (0,) * len(shape))


def _norm_proj_body(n_out, x_ref, g_ref, *refs):
    w_refs, o_refs = refs[:n_out], refs[n_out:]
    xn = _rms(x_ref[...], g_ref[...]).astype(BF16)
    for w_ref, o_ref in zip(w_refs, o_refs):
        o_ref[...] = jnp.dot(xn, w_ref[...], preferred_element_type=F32).astype(o_ref.dtype)


def _norm_proj(x, gain, weights, tm):
    m, d = x.shape
    n_out = len(weights)
    return pl.pallas_call(
        functools.partial(_norm_proj_body, n_out),
        grid=(m // tm,),
        in_specs=[pl.BlockSpec((tm, d), lambda i: (i, 0)), _full((1, d))]
        + [_full(w.shape) for w in weights],
        out_specs=[pl.BlockSpec((tm, w.shape[1]), lambda i: (i, 0)) for w in weights],
        out_shape=[jax.ShapeDtypeStruct((m, w.shape[1]), F32) for w in weights],
        compiler_params=_params("parallel"),
        name="norm_proj",
    )(x, gain.reshape(1, d), *weights)


CONV_PAD = 8


def _ssd_body(real_len, has_init, xbc_ref, z_ref, dt_ref, cs_ref, s0_ref, cw_ref, cb_ref, dtb_ref,
              alog_ref, dsk_ref, nw_ref, exp_ref, y_ref, cso_ref, so_ref, xpad_ref, st_ref):
    c = pl.program_id(1)
    nc = pl.num_programs(1)
    q = SSD_CHUNK
    n_pairs = SSD_HEADS // 2

    @pl.when(c == 0)
    def _():
        if has_init:
            xpad_ref[pl.ds(0, CONV_PAD), :] = jnp.zeros((CONV_PAD, CONV_CH), F32)
            xpad_ref[pl.ds(CONV_PAD - (CONV_WIDTH - 1), CONV_WIDTH - 1), :] = cs_ref[...]
            for p in range(n_pairs):
                st_ref[:, p * LANES:(p + 1) * LANES] = s0_ref[pl.ds(p * LANES, LANES), :].T
        else:
            xpad_ref[pl.ds(0, CONV_PAD), :] = jnp.zeros((CONV_PAD, CONV_CH), F32)
            st_ref[...] = jnp.zeros_like(st_ref)

    xpad_ref[pl.ds(CONV_PAD, q), :] = xbc_ref[...]

    acc = cb_ref[...] + xpad_ref[pl.ds(CONV_PAD - 3, q), :] * cw_ref[0:1, :]
    acc = acc + xpad_ref[pl.ds(CONV_PAD - 2, q), :] * cw_ref[1:2, :]
    acc = acc + xpad_ref[pl.ds(CONV_PAD - 1, q), :] * cw_ref[2:3, :]
    acc = acc + xpad_ref[pl.ds(CONV_PAD, q), :] * cw_ref[3:4, :]
    xc = acc * (1.0 / (1.0 + jnp.exp(-acc)))
    new_tail = xpad_ref[pl.ds(CONV_PAD + real_len - (CONV_WIDTH - 1), CONV_WIDTH - 1), :]

    @pl.when(c == nc - 1)
    def _():
        cso_ref[...] = new_tail

    xpad_ref[pl.ds(CONV_PAD - (CONV_WIDTH - 1), CONV_WIDTH - 1), :] = new_tail

    xs = xc[:, :SSD_WIDTH]
    gn = SSD_GROUPS * SSD_STATE

    dtx = dt_ref[...] + dtb_ref[...]
    dt = jnp.maximum(dtx, 0.0) + jnp.log1p(jnp.exp(-jnp.abs(dtx)))
    if real_len < q:
        row = lax.broadcasted_iota(jnp.int32, (q, LANES), 0)
        dt = jnp.where(row < real_len, dt, 0.0)
    a = -jnp.exp(alog_ref[...])
    da = dt * a
    ii = lax.broadcasted_iota(jnp.int32, (q, q), 0)
    jj = lax.broadcasted_iota(jnp.int32, (q, q), 1)
    causal = ii >= jj
    tril = jnp.where(causal, 1.0, 0.0).astype(F32)
    acum = jnp.dot(tril, da, preferred_element_type=F32, precision=HI)
    acum_t = acum.T
    dt_t = dt.T
    alast = acum[q - 1:q, :]
    ea = jnp.exp(acum)
    wdec = jnp.exp(alast - acum) * dt
    alast8 = jnp.broadcast_to(alast, (8, LANES))
    dec_e = jnp.exp(jnp.dot(alast8, exp_ref[...], preferred_element_type=F32, precision=HI)[0:1, :])

    lane = lax.broadcasted_iota(jnp.int32, (q, LANES), 1)
    first = lane < SSD_HEAD_DIM
    y_parts = []
    xw_parts = []
    for g in range(SSD_GROUPS):
        bg = xc[:, SSD_WIDTH + g * SSD_STATE:SSD_WIDTH + (g + 1) * SSD_STATE]
        cg = xc[:, SSD_WIDTH + gn + g * SSD_STATE:SSD_WIDTH + gn + (g + 1) * SSD_STATE]
        bg_t = bg.T
        scores = _dot(cg, bg_t)
        per_group = SSD_HEADS // SSD_GROUPS
        for pp in range(per_group // 2):
            p = g * (per_group // 2) + pp
            x_pair = xs[:, p * LANES:(p + 1) * LANES]
            st_pair = st_ref[:, p * LANES:(p + 1) * LANES]
            rhs = jnp.concatenate([x_pair, st_pair], axis=0).astype(BF16)
            outs = []
            for k in range(2):
                h = 2 * p + k
                a_col = acum[:, h:h + 1]
                a_row = acum_t[h:h + 1, :]
                seg = jnp.where(causal, a_col - a_row, 0.0)
                m_h = jnp.where(causal, jnp.exp(seg), 0.0) * scores * dt_t[h:h + 1, :]
                c_h = cg * ea[:, h:h + 1]
                lhs = jnp.concatenate([m_h, c_h], axis=1).astype(BF16)
                outs.append(jnp.dot(lhs, rhs, preferred_element_type=F32))
            y_parts.append(jnp.where(first, outs[0], outs[1]))
            w_pair = jnp.where(first, wdec[:, 2 * p:2 * p + 1], wdec[:, 2 * p + 1:2 * p + 2])
            xw_parts.append(x_pair * w_pair)
        half = SSD_WIDTH // SSD_GROUPS
        xw_g = jnp.concatenate(xw_parts[-(per_group // 2):], axis=1)
        upd = _dot(bg_t, xw_g)
        st_ref[:, g * half:(g + 1) * half] = (
            st_ref[:, g * half:(g + 1) * half] * dec_e[:, g * half:(g + 1) * half] + upd)

    y = jnp.concatenate(y_parts, axis=1) + xs * dsk_ref[...]
    zz = z_ref[...]
    y = y * (zz * (1.0 / (1.0 + jnp.exp(-zz))))
    y_ref[...] = _rms(y, nw_ref[...])

    @pl.when(c == nc - 1)
    def _():
        for p in range(n_pairs):
            so_ref[pl.ds(p * LANES, LANES), :] = st_ref[:, p * LANES:(p + 1) * LANES].T


def _ssd(xbc, z, dt, conv_state, ssm_state, conv_w, conv_b, dt_bias, a_log, d_skip, norm_w, bsz, seq):
    has_init = conv_state is not None
    real_len = min(seq, SSD_CHUNK)
    nc = max(seq // SSD_CHUNK, 1)
    if not has_init:
        conv_state = jnp.zeros((bsz, CONV_WIDTH - 1, CONV_CH), F32)
        ssm_state = jnp.zeros((bsz, SSD_HEADS, SSD_HEAD_DIM, SSD_STATE), F32)
    s0 = ssm_state.reshape(bsz, SSD_WIDTH, SSD_STATE)
    pad_h = LANES - SSD_HEADS
    dtb = jnp.pad(dt_bias, (0, pad_h)).reshape(1, LANES)
    alog = jnp.pad(a_log, (0, pad_h)).reshape(1, LANES)
    dsk = jnp.repeat(d_skip, SSD_HEAD_DIM).reshape(1, SSD_WIDTH)
    expand = (np.arange(LANES)[:, None] == (np.arange(SSD_WIDTH) // SSD_HEAD_DIM)[None, :]).astype(np.float32)
    xbc3 = xbc.reshape(bsz, seq, CONV_CH)
    z3 = z.reshape(bsz, seq, SSD_WIDTH)
    dt3 = dt.reshape(bsz, seq, LANES)
    seq_p = nc * SSD_CHUNK
    if seq_p != seq:
        pad = ((0, 0), (0, seq_p - seq), (0, 0))
        xbc3, z3, dt3 = jnp.pad(xbc3, pad), jnp.pad(z3, pad), jnp.pad(dt3, pad)

    def tok(width):
        return pl.BlockSpec((None, SSD_CHUNK, width), lambda b, c: (b, c, 0))

    y, cso, so = pl.pallas_call(
        functools.partial(_ssd_body, real_len, has_init),
        grid=(bsz, nc),
        in_specs=[tok(CONV_CH), tok(SSD_WIDTH), tok(LANES),
                  pl.BlockSpec((None, CONV_WIDTH - 1, CONV_CH), lambda b, c: (b, 0, 0)),
                  pl.BlockSpec((None, SSD_WIDTH, SSD_STATE), lambda b, c: (b, 0, 0)),
                  _full((CONV_WIDTH, CONV_CH)), _full((1, CONV_CH)), _full((1, LANES)), _full((1, LANES)),
                  _full((1, SSD_WIDTH)), _full((1, SSD_WIDTH)), _full((LANES, SSD_WIDTH))],
        out_specs=[tok(SSD_WIDTH),
                   pl.BlockSpec((None, CONV_WIDTH - 1, CONV_CH), lambda b, c: (b, 0, 0)),
                   pl.BlockSpec((None, SSD_WIDTH, SSD_STATE), lambda b, c: (b, 0, 0))],
        out_shape=[jax.ShapeDtypeStruct((bsz, seq_p, SSD_WIDTH), F32),
                   jax.ShapeDtypeStruct((bsz, CONV_WIDTH - 1, CONV_CH), F32),
                   jax.ShapeDtypeStruct((bsz, SSD_WIDTH, SSD_STATE), F32)],
        scratch_shapes=[pltpu.VMEM((CONV_PAD + SSD_CHUNK, CONV_CH), F32),
                        pltpu.VMEM((SSD_STATE, SSD_WIDTH), F32)],
        compiler_params=_params("parallel", "arbitrary"),
        name="ssd_scan",
    )(xbc3, z3, dt3, conv_state, s0, conv_w, conv_b.reshape(1, CONV_CH), dtb, alog, dsk,
      norm_w.reshape(1, SSD_WIDTH), jnp.asarray(expand))
    return (y[:, :seq].reshape(bsz * seq, SSD_WIDTH), cso,
            so.reshape(bsz, SSD_HEADS, SSD_HEAD_DIM, SSD_STATE))


def _t5_bucket_np(rel):
    n = np.maximum(rel, 0)
    max_exact = REL_BUCKETS // 2
    nf = np.maximum(n, 1).astype(np.float32)
    large = max_exact + (np.log(nf / np.float32(max_exact)) / np.float32(math.log(REL_MAX_DIST / max_exact))
                         * np.float32(REL_BUCKETS - max_exact)).astype(np.int32)
    large = np.minimum(large, REL_BUCKETS - 1)
    return np.where(n < max_exact, n, large).astype(np.int32)


def _bias_tiles_body(bk_ref, rb_ref, o_ref):
    h = pl.program_id(0)
    ii = lax.broadcasted_iota(jnp.int32, (MOBA_BLOCK, MOBA_BLOCK), 0)
    jj = lax.broadcasted_iota(jnp.int32, (MOBA_BLOCK, MOBA_BLOCK), 1)
    for t in range(2):
        bk = bk_ref[t]
        acc = jnp.zeros((MOBA_BLOCK, MOBA_BLOCK), F32)
        for u in range(REL_BUCKETS):
            acc = jnp.where(bk == u, rb_ref[u, h], acc)
        if t == 0:
            acc = jnp.where(ii >= jj, acc, NEG)
        o_ref[t] = acc


def _bias_tiles(rel_bias):
    i = np.arange(MOBA_BLOCK)[:, None]
    j = np.arange(MOBA_BLOCK)[None, :]
    buckets = np.stack([_t5_bucket_np(i - j), _t5_bucket_np(MOBA_BLOCK + i - j)])
    return pl.pallas_call(
        _bias_tiles_body,
        grid=(ATT_HEADS,),
        in_specs=[_full((2, MOBA_BLOCK, MOBA_BLOCK)),
                  pl.BlockSpec(memory_space=pltpu.SMEM)],
        out_specs=pl.BlockSpec((None, 2, MOBA_BLOCK, MOBA_BLOCK), lambda h: (h, 0, 0, 0)),
        out_shape=jax.ShapeDtypeStruct((ATT_HEADS, 2, MOBA_BLOCK, MOBA_BLOCK), F32),
        compiler_params=_params("parallel"),
        name="bias_tiles",
    )(jnp.asarray(buckets), rel_bias)


def _block_rank(g, blk):
    rank = jnp.zeros(g.shape, jnp.int32)
    for m in range(g.shape[1]):
        gm = g[:, m:m + 1]
        rank = rank + ((gm > g) | ((gm == g) & (m < blk))).astype(jnp.int32)
    return rank


def _moba_prompt_body(nblk, q_ref, k_ref, v_ref, bias_ref, cfar_ref, o_ref, km_ref):
    pair = pl.program_id(0)
    blk_rows = MOBA_BLOCK
    km_ref[...] = jnp.zeros_like(km_ref)
    for n in range(nblk):
        km_ref[n:n + 1, :] = jnp.mean(k_ref[pl.ds(n * blk_rows, blk_rows), :], axis=0, keepdims=True)
    kmean = km_ref[...]
    lane = lax.broadcasted_iota(jnp.int32, (blk_rows, LANES), 1)
    blk = lax.broadcasted_iota(jnp.int32, (blk_rows, nblk), 1)
    scale = ATT_HEAD_DIM ** -0.5

    def q_tile(i, carry):
        qs = pl.multiple_of(i * blk_rows, blk_rows)
        qt = q_ref[pl.ds(qs, blk_rows), :]
        outs = []
        for kk in range(2):
            hm = (lane >= kk * ATT_HEAD_DIM) & (lane < (kk + 1) * ATT_HEAD_DIM)
            qh = jnp.where(hm, qt, 0.0)
            gate = lax.dot_general(qh, kmean[:nblk], (((1,), (1,)), ((), ())),
                                   preferred_element_type=F32, precision=HI)
            past = blk < i
            gate = jnp.where(past, gate, NEG)
            sel = past & (_block_rank(gate, blk) < MOBA_TOPK)
            pen = jnp.where(sel, 0.0, NEG)
            qs_h = (qh * scale).astype(BF16)
            cfar = cfar_ref[2 * pair + kk]

            def update(st, s, vb):
                m_run, l_run, acc = st
                m_new = jnp.maximum(m_run, jnp.max(s, axis=1, keepdims=True))
                alpha = jnp.exp(m_run - m_new)
                pr = jnp.exp(s - m_new)
                l_new = alpha * l_run + jnp.sum(pr, axis=1, keepdims=True)
                acc = alpha * acc + jnp.dot(pr.astype(BF16), vb.astype(BF16), preferred_element_type=F32)
                return m_new, l_new, acc

            def past_block(n, st):
                ks = pl.multiple_of(n * blk_rows, blk_rows)
                kb = k_ref[pl.ds(ks, blk_rows), :]
                vb = v_ref[pl.ds(ks, blk_rows), :]
                s = lax.dot_general(qs_h, kb.astype(BF16), (((1,), (1,)), ((), ())),
                                    preferred_element_type=F32)
                bias = jnp.where(n == i - 1, bias_ref[kk, 1], cfar)
                col = jnp.sum(jnp.where(blk == n, pen, 0.0), axis=1, keepdims=True)
                return update(st, s + bias + col, vb)

            st0 = (jnp.full((blk_rows, 1), NEG, F32), jnp.zeros((blk_rows, 1), F32),
                   jnp.zeros((blk_rows, LANES), F32))
            st = lax.fori_loop(0, i, past_block, st0)
            kb = k_ref[pl.ds(qs, blk_rows), :]
            vb = v_ref[pl.ds(qs, blk_rows), :]
            s = lax.dot_general(qs_h, kb.astype(BF16), (((1,), (1,)), ((), ())),
                                preferred_element_type=F32)
            _, l_fin, acc = update(st, s + bias_ref[kk, 0], vb)
            outs.append(acc / l_fin)
        o_ref[pl.ds(qs, blk_rows), :] = jnp.where(lane < ATT_HEAD_DIM, outs[0], outs[1])
        return carry

    lax.fori_loop(0, nblk, q_tile, 0)


def _moba_prompt(q, k, v, bias_tiles, cfar, bsz, seq):
    nblk = seq // MOBA_BLOCK
    n_pairs = ATT_HEADS // 2
    tok = pl.BlockSpec((seq, LANES), lambda p, b: (b, p))
    return pl.pallas_call(
        functools.partial(_moba_prompt_body, nblk),
        grid=(n_pairs, bsz),
        in_specs=[tok, tok, tok,
                  pl.BlockSpec((2, 2, MOBA_BLOCK, MOBA_BLOCK), lambda p, b: (p, 0, 0, 0)),
                  pl.BlockSpec(memory_space=pltpu.SMEM)],
        out_specs=tok,
        out_shape=jax.ShapeDtypeStruct((bsz * seq, ATT_WIDTH), F32),
        scratch_shapes=[pltpu.VMEM((max(nblk, 8), LANES), F32)],
        compiler_params=_params("parallel", "parallel"),
        name="moba_prompt",
    )(q, k, v, bias_tiles, cfar)


PAGES_PER_STEP = 16
ROWS8 = 8


def _sample_moba_body(n_new, n_k, page_size, pt_ref, q_ref, kn_ref, vn_ref, tprev_ref, town_ref, cfar_ref, *rest):
    k_refs = rest[:PAGES_PER_STEP]
    v_refs = rest[PAGES_PER_STEP:2 * PAGES_PER_STEP]
    o_ref, s_ref, acc_ref, l_ref = rest[2 * PAGES_PER_STEP:]
    j = pl.program_id(1)
    n_tok = s_ref.shape[2]
    n_blk = n_tok // MOBA_BLOCK
    q = q_ref[...]
    qh = [q[:, h * ATT_HEAD_DIM:(h + 1) * ATT_HEAD_DIM] for h in range(ATT_HEADS)]
    scale = ATT_HEAD_DIM ** -0.5

    @pl.when(j < n_k)
    def _():
        for r in range(PAGES_PER_STEP):
            start = pl.multiple_of((j * PAGES_PER_STEP + r) * page_size, page_size)
            for h in range(ATT_HEADS):
                s_ref[h, :, pl.ds(start, page_size)] = _dot(qh[h], k_refs[r][h])

    @pl.when(j == n_k - 1)
    def _():
        blk = lax.broadcasted_iota(jnp.int32, (ROWS8, n_blk), 1)
        col = lax.broadcasted_iota(jnp.int32, (ROWS8, LANES), 1)
        for h in range(ATT_HEADS):
            gate = jnp.zeros((ROWS8, n_blk), F32)
            for n in range(n_blk):
                tot = jnp.sum(s_ref[h, :, n * MOBA_BLOCK:(n + 1) * MOBA_BLOCK], axis=1, keepdims=True)
                gate = jnp.where(blk == n, tot * (1.0 / MOBA_BLOCK), gate)
            sel = _block_rank(gate, blk) < MOBA_TOPK
            pen = jnp.where(sel, 0.0, NEG)
            cfar = cfar_ref[h]
            m = jnp.full((ROWS8, 1), NEG, F32)
            for n in range(n_blk):
                seg = s_ref[h, :, n * MOBA_BLOCK:(n + 1) * MOBA_BLOCK] * scale
                bias = tprev_ref[h] if n == n_blk - 1 else cfar
                seg = seg + bias + pen[:, n:n + 1]
                s_ref[h, :, n * MOBA_BLOCK:(n + 1) * MOBA_BLOCK] = seg
                m = jnp.maximum(m, jnp.max(seg, axis=1, keepdims=True))
            kn_h = kn_ref[:, h * ATT_HEAD_DIM:(h + 1) * ATT_HEAD_DIM]
            vn_h = vn_ref[:, h * ATT_HEAD_DIM:(h + 1) * ATT_HEAD_DIM]
            s_own = jnp.where(col < n_new, _dot_t(qh[h] * scale, kn_h) + town_ref[h], NEG)
            m = jnp.maximum(m, jnp.max(s_own, axis=1, keepdims=True))
            p_own = jnp.exp(s_own - m)
            l = jnp.sum(p_own, axis=1, keepdims=True)
            for n in range(n_blk):
                p = jnp.exp(s_ref[h, :, n * MOBA_BLOCK:(n + 1) * MOBA_BLOCK] - m)
                s_ref[h, :, n * MOBA_BLOCK:(n + 1) * MOBA_BLOCK] = p
                l = l + jnp.sum(p, axis=1, keepdims=True)
            acc_ref[h] = _dot(p_own, vn_h)
            l_ref[h] = jnp.broadcast_to(l, (ROWS8, LANES))

    @pl.when(j >= n_k)
    def _():
        for h in range(ATT_HEADS):
            acc = acc_ref[h]
            for r in range(PAGES_PER_STEP):
                start = pl.multiple_of(((j - n_k) * PAGES_PER_STEP + r) * page_size, page_size)
                acc = acc + _dot_t(s_ref[h, :, pl.ds(start, page_size)], v_refs[r][h])
            acc_ref[h] = acc

    @pl.when(j == 2 * n_k - 1)
    def _():
        o_ref[...] = jnp.concatenate(
            [acc_ref[h] / l_ref[h][:, :ATT_HEAD_DIM] for h in range(ATT_HEADS)], axis=1)


def _sample_moba(layer, q8, kn, vn, page_table, cache_kt, cache_vt, bias_tiles, cfar, n_new):
    bsz = q8.shape[0]
    page_size = cache_kt.shape[4]
    n_pages = page_table.shape[1]
    n_k = n_pages // PAGES_PER_STEP
    tprev = bias_tiles[:, 1, :ROWS8, :]
    town = bias_tiles[:, 0, :ROWS8, :LANES]
    page_blk = (None, None, ATT_HEADS, ATT_HEAD_DIM, page_size)

    def k_spec(r):
        return pl.BlockSpec(page_blk, lambda b, j, pt: (
            layer, pt[b, jnp.minimum(j, n_k - 1) * PAGES_PER_STEP + r], 0, 0, 0))

    def v_spec(r):
        return pl.BlockSpec(page_blk, lambda b, j, pt: (
            layer, pt[b, jnp.maximum(j - n_k, 0) * PAGES_PER_STEP + r], 0, 0, 0))

    grid_spec = pltpu.PrefetchScalarGridSpec(
        num_scalar_prefetch=1,
        grid=(bsz, 2 * n_k),
        in_specs=[pl.BlockSpec((None, ROWS8, ATT_WIDTH), lambda b, j, pt: (b, 0, 0)),
                  pl.BlockSpec((None, LANES, ATT_WIDTH), lambda b, j, pt: (b, 0, 0)),
                  pl.BlockSpec((None, LANES, ATT_WIDTH), lambda b, j, pt: (b, 0, 0)),
                  pl.BlockSpec((ATT_HEADS, ROWS8, MOBA_BLOCK), lambda b, j, pt: (0, 0, 0)),
                  pl.BlockSpec((ATT_HEADS, ROWS8, LANES), lambda b, j, pt: (0, 0, 0)),
                  pl.BlockSpec(memory_space=pltpu.SMEM)]
        + [k_spec(r) for r in range(PAGES_PER_STEP)] + [v_spec(r) for r in range(PAGES_PER_STEP)],
        out_specs=pl.BlockSpec((None, ROWS8, ATT_WIDTH), lambda b, j, pt: (b, 0, 0)),
        scratch_shapes=[pltpu.VMEM((ATT_HEADS, ROWS8, n_pages * page_size), F32),
                        pltpu.VMEM((ATT_HEADS, ROWS8, ATT_HEAD_DIM), F32),
                        pltpu.VMEM((ATT_HEADS, ROWS8, LANES), F32)],
    )
    return pl.pallas_call(
        functools.partial(_sample_moba_body, n_new, n_k, page_size),
        grid_spec=grid_spec,
        out_shape=jax.ShapeDtypeStruct((bsz, ROWS8, ATT_WIDTH), F32),
        compiler_params=_params("arbitrary", "arbitrary"),
        name="sample_moba",
    )(page_table, q8, kn, vn, tprev, town, cfar, *([cache_kt] * PAGES_PER_STEP), *([cache_vt] * PAGES_PER_STEP))


def _mix_ca_body(rows_per_batch, h_ref, ys_ref, ya_ref, mk_ref, mv_ref, wo1_ref, wo2_ref, g_ref,
                 wq_ref, wco_ref, o_ref):
    tm = h_ref.shape[0]
    h1 = (h_ref[...] + jnp.dot(ys_ref[...].astype(BF16), wo1_ref[...], preferred_element_type=F32)
          + jnp.dot(ya_ref[...].astype(BF16), wo2_ref[...], preferred_element_type=F32))
    hn = _rms(h1, g_ref[...]).astype(BF16)
    qq = jnp.dot(hn, wq_ref[...], preferred_element_type=F32) * (CA_HEAD_DIM ** -0.5)
    nkeys = mk_ref.shape[0] * mk_ref.shape[1]
    mk = mk_ref[...].reshape(nkeys, CA_WIDTH)
    mv = mv_ref[...].reshape(nkeys, CA_WIDTH)
    if rows_per_batch < tm:
        r = lax.broadcasted_iota(jnp.int32, (tm, nkeys), 0) // rows_per_batch
        c = lax.broadcasted_iota(jnp.int32, (tm, nkeys), 1) // MEM_LEN
        same = r == c
    outs = []
    for hd in range(CA_HEADS):
        sl = slice(hd * CA_HEAD_DIM, (hd + 1) * CA_HEAD_DIM)
        s = _dot_t(qq[:, sl], mk[:, sl])
        if rows_per_batch < tm:
            s = jnp.where(same, s, NEG)
        s = s - jnp.max(s, axis=1, keepdims=True)
        p = jnp.exp(s)
        p = p / jnp.sum(p, axis=1, keepdims=True)
        outs.append(_dot(p, mv[:, sl]))
    o = jnp.concatenate(outs, axis=1).astype(BF16)
    o_ref[...] = h1 + jnp.dot(o, wco_ref[...], preferred_element_type=F32)


def _mix_ca(h, ys, ya, mk, mv, wo1, wo2, g_ca, wq, wco, rows_per_batch, tm):
    m = h.shape[0]
    nbat = max(tm // rows_per_batch, 1)
    per = rows_per_batch // tm if rows_per_batch >= tm else 1

    def mem_map(i):
        return (i // per, 0, 0)

    def row(width):
        return pl.BlockSpec((tm, width), lambda i: (i, 0))

    return pl.pallas_call(
        functools.partial(_mix_ca_body, rows_per_batch),
        grid=(m // tm,),
        in_specs=[row(D_MODEL), row(SSD_WIDTH), row(ATT_WIDTH),
                  pl.BlockSpec((nbat, MEM_LEN, CA_WIDTH), mem_map),
                  pl.BlockSpec((nbat, MEM_LEN, CA_WIDTH), mem_map),
                  _full(wo1.shape), _full(wo2.shape), _full((1, D_MODEL)), _full(wq.shape), _full(wco.shape)],
        out_specs=row(D_MODEL),
        out_shape=jax.ShapeDtypeStruct((m, D_MODEL), F32),
        compiler_params=_params("parallel"),
        name="mix_ca",
    )(h, ys, ya, mk, mv, wo1, wo2, g_ca.reshape(1, D_MODEL), wq, wco)


FF_CHUNK = 1024


def _mlp_body(final, h_ref, g_ref, wu_ref, wd_ref, gf_ref, o_ref):
    h = h_ref[...]
    xn = _rms(h, g_ref[...]).astype(BF16)
    acc = h
    for c in range(FF // FF_CHUNK):
        u = jnp.dot(xn, wu_ref[:, c * FF_CHUNK:(c + 1) * FF_CHUNK], preferred_element_type=F32)
        u = jnp.maximum(u, 0.0)
        acc = acc + jnp.dot((u * u).astype(BF16), wd_ref[c * FF_CHUNK:(c + 1) * FF_CHUNK, :],
                            preferred_element_type=F32)
    o_ref[...] = _rms(acc, gf_ref[...]) if final else acc


def _mlp(h, g, wu, wd, g_final, final, tm):
    m = h.shape[0]
    row = pl.BlockSpec((tm, D_MODEL), lambda i: (i, 0))
    return pl.pallas_call(
        functools.partial(_mlp_body, final),
        grid=(m // tm,),
        in_specs=[row, _full((1, D_MODEL)), _full(wu.shape), _full(wd.shape), _full((1, D_MODEL))],
        out_specs=row,
        out_shape=jax.ShapeDtypeStruct((m, D_MODEL), F32),
        compiler_params=_params("parallel"),
        name="mlp",
    )(h, g.reshape(1, D_MODEL), wu, wd, g_final.reshape(1, D_MODEL))


def _layer_weights(l, w_in, w_out, w_ca_q, w_ca_kv, w_ca_o, w_up, w_down):
    w = w_in[l]
    o = 0
    parts = []
    for width in (SSD_WIDTH, CONV_CH, SSD_HEADS, ATT_WIDTH, ATT_WIDTH, ATT_WIDTH):
        parts.append(w[:, o:o + width])
        o += width
    parts[2] = jnp.pad(parts[2], ((0, 0), (0, LANES - SSD_HEADS)))
    wo = w_out[l]
    return dict(
        w_in=[p.astype(BF16) for p in parts],
        wo1=wo[:SSD_WIDTH].astype(BF16), wo2=wo[SSD_WIDTH:].astype(BF16),
        wq=w_ca_q[l].astype(BF16), wco=w_ca_o[l].astype(BF16),
        wkv=[w_ca_kv[l][:, :CA_WIDTH].astype(BF16), w_ca_kv[l][:, CA_WIDTH:].astype(BF16)],
        wu=w_up[l].astype(BF16), wd=w_down[l].astype(BF16))


def kernel(x_prompt, x_sample, cache_k, cache_v, cache_mem_k, cache_mem_v, state_conv, state_ssm,
           page_table, mem_prompt, norm_mix, w_in, conv_w, conv_b, dt_bias, a_log, d_skip, ssd_norm,
           rel_bias, w_out, norm_ca, norm_mem, w_ca_q, w_ca_kv, w_ca_o, norm_mlp, w_up, w_down, norm_final):
    depth = w_in.shape[0]
    pb, seq, _ = x_prompt.shape
    sb, dseq, _ = x_sample.shape
    page_size = cache_k.shape[2]
    tm = 512

    bias_tiles = _bias_tiles(rel_bias)
    cfar = rel_bias[REL_BUCKETS - 1]
    cache_kt = jnp.transpose(cache_k, (0, 1, 3, 4, 2))
    cache_vt = jnp.transpose(cache_v, (0, 1, 3, 4, 2))
    mem2d = mem_prompt.reshape(pb * MEM_LEN, D_MODEL)

    def pad_rows(a, rows):
        return jnp.pad(a.reshape(sb, dseq, ATT_WIDTH), ((0, 0), (0, rows - dseq), (0, 0)))

    h = x_prompt.reshape(pb * seq, D_MODEL)
    g = x_sample.reshape(sb * dseq, D_MODEL)
    outs = {n: [] for n in ("pk", "pv", "pmk", "pmv", "pconv", "pssm", "sk", "sv", "sconv", "sssm")}
    for l in range(depth):
        lw = _layer_weights(l, w_in, w_out, w_ca_q, w_ca_kv, w_ca_o, w_up, w_down)
        last = l == depth - 1

        z, xbc, dt, q, k, v = _norm_proj(h, norm_mix[l], lw["w_in"], tm)
        y_ssd, c_new, s_new = _ssd(xbc, z, dt, None, None, conv_w[l], conv_b[l], dt_bias[l], a_log[l],
                                   d_skip[l], ssd_norm[l], pb, seq)
        y_att = _moba_prompt(q, k, v, bias_tiles, cfar, pb, seq)
        mk, mv = _norm_proj(mem2d, norm_mem[l], lw["wkv"], min(tm, pb * MEM_LEN))
        h = _mix_ca(h, y_ssd, y_att, mk.reshape(pb, MEM_LEN, CA_WIDTH), mv.reshape(pb, MEM_LEN, CA_WIDTH),
                    lw["wo1"], lw["wo2"], norm_ca[l], lw["wq"], lw["wco"], seq, tm)
        h = _mlp(h, norm_mlp[l], lw["wu"], lw["wd"], norm_final, last, tm)
        outs["pk"].append(k.reshape(pb, seq, ATT_HEADS, ATT_HEAD_DIM))
        outs["pv"].append(v.reshape(pb, seq, ATT_HEADS, ATT_HEAD_DIM))
        outs["pmk"].append(mk.reshape(pb, MEM_LEN, CA_HEADS, CA_HEAD_DIM))
        outs["pmv"].append(mv.reshape(pb, MEM_LEN, CA_HEADS, CA_HEAD_DIM))
        outs["pconv"].append(c_new)
        outs["pssm"].append(s_new)

        ms = sb * dseq
        z, xbc, dt, q, k, v = _norm_proj(g, norm_mix[l], lw["w_in"], ms)
        y_ssd, c_new, s_new = _ssd(xbc, z, dt, state_conv[l], state_ssm[l], conv_w[l], conv_b[l], dt_bias[l],
                                   a_log[l], d_skip[l], ssd_norm[l], sb, dseq)
        y_att = _sample_moba(l, pad_rows(q, ROWS8), pad_rows(k, LANES), pad_rows(v, LANES), page_table,
                             cache_kt, cache_vt, bias_tiles, cfar, dseq)[:, :dseq].reshape(ms, ATT_WIDTH)
        g = _mix_ca(g, y_ssd, y_att, cache_mem_k[l].reshape(sb, MEM_LEN, CA_WIDTH),
                    cache_mem_v[l].reshape(sb, MEM_LEN, CA_WIDTH), lw["wo1"], lw["wo2"], norm_ca[l],
                    lw["wq"], lw["wco"], dseq, min(32, ms))
        g = _mlp(g, norm_mlp[l], lw["wu"], lw["wd"], norm_final, last, ms)
        outs["sk"].append(k.reshape(sb, dseq, ATT_HEADS, ATT_HEAD_DIM))
        outs["sv"].append(v.reshape(sb, dseq, ATT_HEADS, ATT_HEAD_DIM))
        outs["sconv"].append(c_new)
        outs["sssm"].append(s_new)

    st = {n: jnp.stack(vs) for n, vs in outs.items()}
    return (h.reshape(pb, seq, D_MODEL), g.reshape(sb, dseq, D_MODEL), st["pk"], st["pv"], st["pmk"], st["pmv"],
            st["pconv"], st["pssm"], st["sk"], st["sv"], st["sconv"], st["sssm"])
```

```python
import functools
import math

import numpy as np
import jax
import jax.numpy as jnp
from jax import lax
from jax.experimental import pallas as pl
from jax.experimental.pallas import tpu as pltpu

D_MODEL = 1024
SSD_HEADS = 16
SSD_HEAD_DIM = 64
SSD_WIDTH = SSD_HEADS * SSD_HEAD_DIM
SSD_GROUPS = 2
SSD_STATE = 128
SSD_CHUNK = 128
CONV_WIDTH = 4
CONV_CH = SSD_WIDTH + 2 * SSD_GROUPS * SSD_STATE
ATT_HEADS = 8
ATT_HEAD_DIM = 64
ATT_WIDTH = ATT_HEADS * ATT_HEAD_DIM
MOBA_BLOCK = 256
MOBA_TOPK = 3
REL_BUCKETS = 32
REL_MAX_DIST = 128
MEM_LEN = 256
CA_HEADS = 4
CA_HEAD_DIM = 128
CA_WIDTH = CA_HEADS * CA_HEAD_DIM
FF = 4 * D_MODEL
EPS = 1e-5

LANES = 128
VMEM_LIMIT = 56 * 1024 * 1024
NEG = -1e30

BF16 = jnp.bfloat16
F32 = jnp.float32
HI = lax.Precision.HIGHEST


def _params(*sem):
    return pltpu.CompilerParams(dimension_semantics=sem, vmem_limit_bytes=VMEM_LIMIT)


def _rms(x, gain):
    return x * lax.rsqrt(jnp.mean(x * x, axis=-1, keepdims=True) + EPS) * gain


def _dot(a, b):
    return jnp.dot(a.astype(BF16), b.astype(BF16), preferred_element_type=F32)


def _dot_t(a, b):
    return lax.dot_general(a.astype(BF16), b.astype(BF16), (((1,), (1,)), ((), ())),
                           preferred_element_type=F32)


def _full(shape):
    return pl.BlockSpec(shape, lambda *_: (0,) * len(shape))


def _norm_proj_body(n_out, x_ref, g_ref, *refs):
    w_refs, o_refs = refs[:n_out], refs[n_out:]
    xn = _rms(x_ref[...], g_ref[...]).astype(BF16)
    for w_ref, o_ref in zip(w_refs, o_refs):
        o_ref[...] = jnp.dot(xn, w_ref[...], preferred_element_type=F32).astype(o_ref.dtype)


def _norm_proj(x, gain, weights, tm):
    m, d = x.shape
    n_out = len(weights)
    return pl.pallas_call(
        functools.partial(_norm_proj_body, n_out),
        grid=(m // tm,),
        in_specs=[pl.BlockSpec((tm, d), lambda i: (i, 0)), _full((1, d))]
        + [_full(w.shape) for w in weights],
        out_specs=[pl.BlockSpec((tm, w.shape[1]), lambda i: (i, 0)) for w in weights],
        out_shape=[jax.ShapeDtypeStruct((m, w.shape[1]), F32) for w in weights],
        compiler_params=_params("parallel"),
        name="norm_proj",
    )(x, gain.reshape(1, d), *weights)


CONV_PAD = 8


def _ssd_body(real_len, has_init, xbc_ref, z_ref, dt_ref, cs_ref, s0_ref, cw_ref, cb_ref, dtb_ref,
              alog_ref, dsk_ref, nw_ref, exp_ref, y_ref, cso_ref, so_ref, xpad_ref, st_ref):
    c = pl.program_id(1)
    nc = pl.num_programs(1)
    q = SSD_CHUNK
    n_pairs = SSD_HEADS // 2

    @pl.when(c == 0)
    def _():
        if has_init:
            xpad_ref[pl.ds(0, CONV_PAD), :] = jnp.zeros((CONV_PAD, CONV_CH), F32)
            xpad_ref[pl.ds(CONV_PAD - (CONV_WIDTH - 1), CONV_WIDTH - 1), :] = cs_ref[...]
            for p in range(n_pairs):
                st_ref[:, p * LANES:(p + 1) * LANES] = s0_ref[pl.ds(p * LANES, LANES), :].T
        else:
            xpad_ref[pl.ds(0, CONV_PAD), :] = jnp.zeros((CONV_PAD, CONV_CH), F32)
            st_ref[...] = jnp.zeros_like(st_ref)

    xpad_ref[pl.ds(CONV_PAD, q), :] = xbc_ref[...]

    acc = cb_ref[...] + xpad_ref[pl.ds(CONV_PAD - 3, q), :] * cw_ref[0:1, :]
    acc = acc + xpad_ref[pl.ds(CONV_PAD - 2, q), :] * cw_ref[1:2, :]
    acc = acc + xpad_ref[pl.ds(CONV_PAD - 1, q), :] * cw_ref[2:3, :]
    acc = acc + xpad_ref[pl.ds(CONV_PAD, q), :] * cw_ref[3:4, :]
    xc = acc * (1.0 / (1.0 + jnp.exp(-acc)))
    new_tail = xpad_ref[pl.ds(CONV_PAD + real_len - (CONV_WIDTH - 1), CONV_WIDTH - 1), :]

    @pl.when(c == nc - 1)
    def _():
        cso_ref[...] = new_tail

    xpad_ref[pl.ds(CONV_PAD - (CONV_WIDTH - 1), CONV_WIDTH - 1), :] = new_tail

    xs = xc[:, :SSD_WIDTH]
    gn = SSD_GROUPS * SSD_STATE

    dtx = dt_ref[...] + dtb_ref[...]
    dt = jnp.maximum(dtx, 0.0) + jnp.log1p(jnp.exp(-jnp.abs(dtx)))
    if real_len < q:
        row = lax.broadcasted_iota(jnp.int32, (q, LANES), 0)
        dt = jnp.where(row < real_len, dt, 0.0)
    a = -jnp.exp(alog_ref[...])
    da = dt * a
    ii = lax.broadcasted_iota(jnp.int32, (q, q), 0)
    jj = lax.broadcasted_iota(jnp.int32, (q, q), 1)
    causal = ii >= jj
    tril = jnp.where(causal, 1.0, 0.0).astype(F32)
    acum = jnp.dot(tril, da, preferred_element_type=F32, precision=HI)
    acum_t = acum.T
    dt_t = dt.T
    alast = acum[q - 1:q, :]
    ea = jnp.exp(acum)
    wdec = jnp.exp(alast - acum) * dt
    alast8 = jnp.broadcast_to(alast, (8, LANES))
    dec_e = jnp.exp(jnp.dot(alast8, exp_ref[...], preferred_element_type=F32, precision=HI)[0:1, :])

    lane = lax.broadcasted_iota(jnp.int32, (q, LANES), 1)
    first = lane < SSD_HEAD_DIM
    y_parts = []
    xw_parts = []
    for g in range(SSD_GROUPS):
        bg = xc[:, SSD_WIDTH + g * SSD_STATE:SSD_WIDTH + (g + 1) * SSD_STATE]
        cg = xc[:, SSD_WIDTH + gn + g * SSD_STATE:SSD_WIDTH + gn + (g + 1) * SSD_STATE]
        bg_t = bg.T
        scores = _dot(cg, bg_t)
        per_group = SSD_HEADS // SSD_GROUPS
        for pp in range(per_group // 2):
            p = g * (per_group // 2) + pp
            x_pair = xs[:, p * LANES:(p + 1) * LANES]
            st_pair = st_ref[:, p * LANES:(p + 1) * LANES]
            rhs = jnp.concatenate([x_pair, st_pair], axis=0).astype(BF16)
            outs = []
            for k in range(2):
                h = 2 * p + k
                a_col = acum[:, h:h + 1]
                a_row = acum_t[h:h + 1, :]
                seg = jnp.where(causal, a_col - a_row, 0.0)
                m_h = jnp.where(causal, jnp.exp(seg), 0.0) * scores * dt_t[h:h + 1, :]
                c_h = cg * ea[:, h:h + 1]
                lhs = jnp.concatenate([m_h, c_h], axis=1).astype(BF16)
                outs.append(jnp.dot(lhs, rhs, preferred_element_type=F32))
            y_parts.append(jnp.where(first, outs[0], outs[1]))
            w_pair = jnp.where(first, wdec[:, 2 * p:2 * p + 1], wdec[:, 2 * p + 1:2 * p + 2])
            xw_parts.append(x_pair * w_pair)
        half = SSD_WIDTH // SSD_GROUPS
        xw_g = jnp.concatenate(xw_parts[-(per_group // 2):], axis=1)
        upd = _dot(bg_t, xw_g)
        st_ref[:, g * half:(g + 1) * half] = (
            st_ref[:, g * half:(g + 1) * half] * dec_e[:, g * half:(g + 1) * half] + upd)

    y = jnp.concatenate(y_parts, axis=1) + xs * dsk_ref[...]
    zz = z_ref[...]
    y = y * (zz * (1.0 / (1.0 + jnp.exp(-zz))))
    y_ref[...] = _rms(y, nw_ref[...])

    @pl.when(c == nc - 1)
    def _():
        for p in range(n_pairs):
            so_ref[pl.ds(p * LANES, LANES), :] = st_ref[:, p * LANES:(p + 1) * LANES].T


def _ssd(xbc, z, dt, conv_state, ssm_state, conv_w, conv_b, dt_bias, a_log, d_skip, norm_w, bsz, seq):
    has_init = conv_state is not None
    real_len = min(seq, SSD_CHUNK)
    nc = max(seq // SSD_CHUNK, 1)
    if not has_init:
        conv_state = jnp.zeros((bsz, CONV_WIDTH - 1, CONV_CH), F32)
        ssm_state = jnp.zeros((bsz, SSD_HEADS, SSD_HEAD_DIM, SSD_STATE), F32)
    s0 = ssm_state.reshape(bsz, SSD_WIDTH, SSD_STATE)
    pad_h = LANES - SSD_HEADS
    dtb = jnp.pad(dt_bias, (0, pad_h)).reshape(1, LANES)
    alog = jnp.pad(a_log, (0, pad_h)).reshape(1, LANES)
    dsk = jnp.repeat(d_skip, SSD_HEAD_DIM).reshape(1, SSD_WIDTH)
    expand = (np.arange(LANES)[:, None] == (np.arange(SSD_WIDTH) // SSD_HEAD_DIM)[None, :]).astype(np.float32)
    xbc3 = xbc.reshape(bsz, seq, CONV_CH)
    z3 = z.reshape(bsz, seq, SSD_WIDTH)
    dt3 = dt.reshape(bsz, seq, LANES)
    seq_p = nc * SSD_CHUNK
    if seq_p != seq:
        pad = ((0, 0), (0, seq_p - seq), (0, 0))
        xbc3, z3, dt3 = jnp.pad(xbc3, pad), jnp.pad(z3, pad), jnp.pad(dt3, pad)

    def tok(width):
        return pl.BlockSpec((None, SSD_CHUNK, width), lambda b, c: (b, c, 0))

    y, cso, so = pl.pallas_call(
        functools.partial(_ssd_body, real_len, has_init),
        grid=(bsz, nc),
        in_specs=[tok(CONV_CH), tok(SSD_WIDTH), tok(LANES),
                  pl.BlockSpec((None, CONV_WIDTH - 1, CONV_CH), lambda b, c: (b, 0, 0)),
                  pl.BlockSpec((None, SSD_WIDTH, SSD_STATE), lambda b, c: (b, 0, 0)),
                  _full((CONV_WIDTH, CONV_CH)), _full((1, CONV_CH)), _full((1, LANES)), _full((1, LANES)),
                  _full((1, SSD_WIDTH)), _full((1, SSD_WIDTH)), _full((LANES, SSD_WIDTH))],
        out_specs=[tok(SSD_WIDTH),
                   pl.BlockSpec((None, CONV_WIDTH - 1, CONV_CH), lambda b, c: (b, 0, 0)),
                   pl.BlockSpec((None, SSD_WIDTH, SSD_STATE), lambda b, c: (b, 0, 0))],
        out_shape=[jax.ShapeDtypeStruct((bsz, seq_p, SSD_WIDTH), F32),
                   jax.ShapeDtypeStruct((bsz, CONV_WIDTH - 1, CONV_CH), F32),
                   jax.ShapeDtypeStruct((bsz, SSD_WIDTH, SSD_STATE), F32)],
        scratch_shapes=[pltpu.VMEM((CONV_PAD + SSD_CHUNK, CONV_CH), F32),
                        pltpu.VMEM((SSD_STATE, SSD_WIDTH), F32)],
        compiler_params=_params("parallel", "arbitrary"),
        name="ssd_scan",
    )(xbc3, z3, dt3, conv_state, s0, conv_w, conv_b.reshape(1, CONV_CH), dtb, alog, dsk,
      norm_w.reshape(1, SSD_WIDTH), jnp.asarray(expand))
    return (y[:, :seq].reshape(bsz * seq, SSD_WIDTH), cso,
            so.reshape(bsz, SSD_HEADS, SSD_HEAD_DIM, SSD_STATE))


def _t5_bucket_np(rel):
    n = np.maximum(rel, 0)
    max_exact = REL_BUCKETS // 2
    nf = np.maximum(n, 1).astype(np.float32)
    large = max_exact + (np.log(nf / np.float32(max_exact)) / np.float32(math.log(REL_MAX_DIST / max_exact))
                         * np.float32(REL_BUCKETS - max_exact)).astype(np.int32)
    large = np.minimum(large, REL_BUCKETS - 1)
    return np.where(n < max_exact, n, large).astype(np.int32)


def _bias_tiles_body(bk_ref, rb_ref, o_ref):
    h = pl.program_id(0)
    ii = lax.broadcasted_iota(jnp.int32, (MOBA_BLOCK, MOBA_BLOCK), 0)
    jj = lax.broadcasted_iota(jnp.int32, (MOBA_BLOCK, MOBA_BLOCK), 1)
    for t in range(2):
        bk = bk_ref[t]
        acc = jnp.zeros((MOBA_BLOCK, MOBA_BLOCK), F32)
        for u in range(REL_BUCKETS):
            acc = jnp.where(bk == u, rb_ref[u, h], acc)
        if t == 0:
            acc = jnp.where(jj >= ii, acc, NEG)
        o_ref[t] = acc


def _bias_tiles(rel_bias):
    i = np.arange(MOBA_BLOCK)[None, :]
    j = np.arange(MOBA_BLOCK)[:, None]
    buckets = np.stack([_t5_bucket_np(i - j), _t5_bucket_np(MOBA_BLOCK + i - j)])
    return pl.pallas_call(
        _bias_tiles_body,
        grid=(ATT_HEADS,),
        in_specs=[_full((2, MOBA_BLOCK, MOBA_BLOCK)),
                  pl.BlockSpec(memory_space=pltpu.SMEM)],
        out_specs=pl.BlockSpec((None, 2, MOBA_BLOCK, MOBA_BLOCK), lambda h: (h, 0, 0, 0)),
        out_shape=jax.ShapeDtypeStruct((ATT_HEADS, 2, MOBA_BLOCK, MOBA_BLOCK), F32),
        compiler_params=_params("parallel"),
        name="bias_tiles",
    )(jnp.asarray(buckets), rel_bias)


def _block_rank(g, blk):
    rank = jnp.zeros(g.shape, jnp.int32)
    for m in range(g.shape[1]):
        gm = g[:, m:m + 1]
        rank = rank + ((gm > g) | ((gm == g) & (m < blk))).astype(jnp.int32)
    return rank


def _block_rank_t(g, blk):
    rank = jnp.zeros(g.shape, jnp.int32)
    for m in range(g.shape[0]):
        gm = g[m:m + 1, :]
        rank = rank + ((gm > g) | ((gm == g) & (m < blk))).astype(jnp.int32)
    return rank


def _moba_prompt_body(nblk, q_ref, k_ref, v_ref, bias_ref, cfar_ref, o_ref, km_ref, k16_ref, vt_ref, qt_ref):
    pair = pl.program_id(0)
    blk_rows = MOBA_BLOCK
    hd = ATT_HEAD_DIM
    scale = hd ** -0.5
    for n in range(nblk):
        kb = k_ref[pl.ds(n * blk_rows, blk_rows), :]
        km_ref[n:n + 1, :] = jnp.mean(kb, axis=0, keepdims=True)
        k16_ref[pl.ds(n * blk_rows, blk_rows), :] = kb.astype(BF16)
    for c in range(nblk * blk_rows // LANES):
        vt_ref[:, c * LANES:(c + 1) * LANES] = v_ref[pl.ds(c * LANES, LANES), :].T.astype(BF16)
        qt_ref[:, c * LANES:(c + 1) * LANES] = (q_ref[pl.ds(c * LANES, LANES), :] * scale).T.astype(BF16)
    kmean = km_ref[...].astype(BF16)
    row = lax.broadcasted_iota(jnp.int32, (LANES, blk_rows), 0)
    blk = lax.broadcasted_iota(jnp.int32, (nblk, blk_rows), 0)
    zero16 = jnp.zeros((LANES, blk_rows), BF16)

    for i in range(nblk):
        qt = qt_ref[:, i * blk_rows:(i + 1) * blk_rows]
        n_keys = (i + 1) * blk_rows
        halves = []
        for kk in range(2):
            qth = jnp.where(row < hd, qt, zero16) if kk == 0 else jnp.where(row >= hd, qt, zero16)
            gate = jnp.dot(kmean, qth, preferred_element_type=F32)
            past = blk < i
            gate = jnp.where(past, gate, NEG)
            sel = past & (_block_rank_t(gate, blk) < MOBA_TOPK)
            pen = jnp.where(sel, cfar_ref[2 * pair + kk], NEG)
            s = jnp.dot(k16_ref[0:n_keys, :], qth, preferred_element_type=F32)
            segs = []
            for n in range(i + 1):
                seg = s[n * blk_rows:(n + 1) * blk_rows]
                if n == i:
                    seg = seg + bias_ref[kk, 0]
                elif n == i - 1:
                    seg = seg + bias_ref[kk, 1] + jnp.where(sel[n:n + 1, :], 0.0, NEG)
                else:
                    seg = seg + pen[n:n + 1, :]
                segs.append(seg)
            m = segs[0]
            for seg in segs[1:]:
                m = jnp.maximum(m, seg)
            m = jnp.max(m, axis=0, keepdims=True)
            probs = [jnp.exp(seg - m) for seg in segs]
            tot = probs[0]
            for p in probs[1:]:
                tot = tot + p
            l = jnp.sum(tot, axis=0, keepdims=True)
            p16 = jnp.concatenate([p.astype(BF16) for p in probs], axis=0)
            acc = jnp.dot(vt_ref[kk * hd:(kk + 1) * hd, 0:n_keys], p16, preferred_element_type=F32)
            halves.append(acc / l)
        o_ref[pl.ds(i * blk_rows, blk_rows), :] = jnp.concatenate(halves, axis=0).T


def _moba_prompt(q, k, v, bias_tiles, cfar, bsz, seq):
    nblk = seq // MOBA_BLOCK
    n_pairs = ATT_HEADS // 2
    tok = pl.BlockSpec((seq, LANES), lambda p, b: (b, p))
    return pl.pallas_call(
        functools.partial(_moba_prompt_body, nblk),
        grid=(n_pairs, bsz),
        in_specs=[tok, tok, tok,
                  pl.BlockSpec((2, 2, MOBA_BLOCK, MOBA_BLOCK), lambda p, b: (p, 0, 0, 0)),
                  pl.BlockSpec(memory_space=pltpu.SMEM)],
        out_specs=tok,
        out_shape=jax.ShapeDtypeStruct((bsz * seq, ATT_WIDTH), F32),
        scratch_shapes=[pltpu.VMEM((nblk, LANES), F32), pltpu.VMEM((seq, LANES), BF16),
                        pltpu.VMEM((LANES, seq), BF16), pltpu.VMEM((LANES, seq), BF16)],
        compiler_params=_params("parallel", "parallel"),
        name="moba_prompt",
    )(q, k, v, bias_tiles, cfar)


PAGES_PER_STEP = 16
ROWS8 = 8


def _sample_moba_body(n_new, n_k, page_size, pt_ref, q_ref, kn_ref, vn_ref, tprev_ref, town_ref, cfar_ref, *rest):
    k_refs = rest[:PAGES_PER_STEP]
    v_refs = rest[PAGES_PER_STEP:2 * PAGES_PER_STEP]
    o_ref, s_ref, acc_ref, l_ref = rest[2 * PAGES_PER_STEP:]
    j = pl.program_id(1)
    n_tok = s_ref.shape[2]
    n_blk = n_tok // MOBA_BLOCK
    q = q_ref[...]
    qh = [q[:, h * ATT_HEAD_DIM:(h + 1) * ATT_HEAD_DIM] for h in range(ATT_HEADS)]
    scale = ATT_HEAD_DIM ** -0.5

    @pl.when(j < n_k)
    def _():
        for r in range(PAGES_PER_STEP):
            start = pl.multiple_of((j * PAGES_PER_STEP + r) * page_size, page_size)
            for h in range(ATT_HEADS):
                s_ref[h, :, pl.ds(start, page_size)] = _dot(qh[h], k_refs[r][h])

    @pl.when(j == n_k - 1)
    def _():
        blk = lax.broadcasted_iota(jnp.int32, (ROWS8, n_blk), 1)
        col = lax.broadcasted_iota(jnp.int32, (ROWS8, LANES), 1)
        halves_per_blk = MOBA_BLOCK // LANES
        for h in range(ATT_HEADS):
            s_all = s_ref[h]
            chunks = [s_all[:, c * LANES:(c + 1) * LANES] for c in range(n_tok // LANES)]
            gate = jnp.zeros((ROWS8, n_blk), F32)
            for n in range(n_blk):
                part = chunks[n * halves_per_blk]
                for c in chunks[n * halves_per_blk + 1:(n + 1) * halves_per_blk]:
                    part = part + c
                gate = jnp.where(blk == n, jnp.sum(part, axis=1, keepdims=True) * (1.0 / MOBA_BLOCK), gate)
            sel = _block_rank(gate, blk) < MOBA_TOPK
            pen = jnp.where(sel, cfar_ref[h], NEG)
            pen_last = jnp.where(sel[:, n_blk - 1:n_blk], 0.0, NEG)
            logits = []
            for c, x in enumerate(chunks):
                n = c // halves_per_blk
                if n == n_blk - 1:
                    off = (c % halves_per_blk) * LANES
                    logits.append(x * scale + tprev_ref[h, :, off:off + LANES] + pen_last)
                else:
                    logits.append(x * scale + pen[:, n:n + 1])
            kn_h = kn_ref[:, h * ATT_HEAD_DIM:(h + 1) * ATT_HEAD_DIM]
            vn_h = vn_ref[:, h * ATT_HEAD_DIM:(h + 1) * ATT_HEAD_DIM]
            s_own = jnp.where(col < n_new, _dot_t(qh[h] * scale, kn_h) + town_ref[h], NEG)
            m_el = s_own
            for x in logits:
                m_el = jnp.maximum(m_el, x)
            m = jnp.max(m_el, axis=1, keepdims=True)
            p_own = jnp.exp(s_own - m)
            tot = p_own
            for c, x in enumerate(logits):
                p = jnp.exp(x - m)
                s_ref[h, :, c * LANES:(c + 1) * LANES] = p
                tot = tot + p
            acc_ref[h] = _dot(p_own, vn_h)
            l_ref[h] = jnp.broadcast_to(jnp.sum(tot, axis=1, keepdims=True), (ROWS8, LANES))

    @pl.when(j >= n_k)
    def _():
        for h in range(ATT_HEADS):
            acc = acc_ref[h]
            for r in range(PAGES_PER_STEP):
                start = pl.multiple_of(((j - n_k) * PAGES_PER_STEP + r) * page_size, page_size)
                acc = acc + _dot_t(s_ref[h, :, pl.ds(start, page_size)], v_refs[r][h])
            acc_ref[h] = acc

    @pl.when(j == 2 * n_k - 1)
    def _():
        o_ref[...] = jnp.concatenate(
            [acc_ref[h] / l_ref[h][:, :ATT_HEAD_DIM] for h in range(ATT_HEADS)], axis=1)


def _sample_moba(layer, q8, kn, vn, page_table, cache_kt, cache_vt, bias_tiles, cfar, n_new):
    bsz = q8.shape[0]
    page_size = cache_kt.shape[4]
    n_pages = page_table.shape[1]
    n_k = n_pages // PAGES_PER_STEP
    tprev = jnp.transpose(bias_tiles[:, 1, :, :ROWS8], (0, 2, 1))
    town = jnp.transpose(bias_tiles[:, 0, :LANES, :ROWS8], (0, 2, 1))
    page_blk =(None, None, ATT_HEADS, ATT_HEAD_DIM, page_size)

    def k_spec(r):
        return pl.BlockSpec(page_blk, lambda b, j, pt: (
            layer, pt[b, jnp.minimum(j, n_k - 1) * PAGES_PER_STEP + r], 0, 0, 0))

    def v_spec(r):
        return pl.BlockSpec(page_blk, lambda b, j, pt: (
            layer, pt[b, jnp.maximum(j - n_k, 0) * PAGES_PER_STEP + r], 0, 0, 0))

    grid_spec = pltpu.PrefetchScalarGridSpec(
        num_scalar_prefetch=1,
        grid=(bsz, 2 * n_k),
        in_specs=[pl.BlockSpec((None, ROWS8, ATT_WIDTH), lambda b, j, pt: (b, 0, 0)),
                  pl.BlockSpec((None, LANES, ATT_WIDTH), lambda b, j, pt: (b, 0, 0)),
                  pl.BlockSpec((None, LANES, ATT_WIDTH), lambda b, j, pt: (b, 0, 0)),
                  pl.BlockSpec((ATT_HEADS, ROWS8, MOBA_BLOCK), lambda b, j, pt: (0, 0, 0)),
                  pl.BlockSpec((ATT_HEADS, ROWS8, LANES), lambda b, j, pt: (0, 0, 0)),
                  pl.BlockSpec(memory_space=pltpu.SMEM)]
        + [k_spec(r) for r in range(PAGES_PER_STEP)] + [v_spec(r) for r in range(PAGES_PER_STEP)],
        out_specs=pl.BlockSpec((None, ROWS8, ATT_WIDTH), lambda b, j, pt: (b, 0, 0)),
        scratch_shapes=[pltpu.VMEM((ATT_HEADS, ROWS8, n_pages * page_size), F32),
                        pltpu.VMEM((ATT_HEADS, ROWS8, ATT_HEAD_DIM), F32),
                        pltpu.VMEM((ATT_HEADS, ROWS8, LANES), F32)],
    )
    return pl.pallas_call(
        functools.partial(_sample_moba_body, n_new, n_k, page_size),
        grid_spec=grid_spec,
        out_shape=jax.ShapeDtypeStruct((bsz, ROWS8, ATT_WIDTH), F32),
        compiler_params=_params("arbitrary", "arbitrary"),
        name="sample_moba",
    )(page_table, q8, kn, vn, tprev, town, cfar, *([cache_kt] * PAGES_PER_STEP), *([cache_vt] * PAGES_PER_STEP))


def _mix_ca_body(rows_per_batch, h_ref, ys_ref, ya_ref, mk_ref, mv_ref, wo1_ref, wo2_ref, g_ref,
                 wq_ref, wco_ref, o_ref):
    tm = h_ref.shape[0]
    h1 = (h_ref[...] + jnp.dot(ys_ref[...].astype(BF16), wo1_ref[...], preferred_element_type=F32)
          + jnp.dot(ya_ref[...].astype(BF16), wo2_ref[...], preferred_element_type=F32))
    hn = _rms(h1, g_ref[...]).astype(BF16)
    qq = jnp.dot(hn, wq_ref[...], preferred_element_type=F32) * (CA_HEAD_DIM ** -0.5)
    nkeys = mk_ref.shape[0] * mk_ref.shape[1]
    mk = mk_ref[...].reshape(nkeys, CA_WIDTH)
    mv = mv_ref[...].reshape(nkeys, CA_WIDTH)
    if rows_per_batch < tm:
        r = lax.broadcasted_iota(jnp.int32, (tm, nkeys), 0) // rows_per_batch
        c = lax.broadcasted_iota(jnp.int32, (tm, nkeys), 1) // MEM_LEN
        same = r == c
    outs = []
    for hd in range(CA_HEADS):
        sl = slice(hd * CA_HEAD_DIM, (hd + 1) * CA_HEAD_DIM)
        s = _dot_t(qq[:, sl], mk[:, sl])
        if rows_per_batch < tm:
            s = jnp.where(same, s, NEG)
        s = s - jnp.max(s, axis=1, keepdims=True)
        p = jnp.exp(s)
        p = p / jnp.sum(p, axis=1, keepdims=True)
        outs.append(_dot(p, mv[:, sl]))
    o = jnp.concatenate(outs, axis=1).astype(BF16)
    o_ref[...] = h1 + jnp.dot(o, wco_ref[...], preferred_element_type=F32)


def _mix_ca(h, ys, ya, mk, mv, wo1, wo2, g_ca, wq, wco, rows_per_batch, tm):
    m = h.shape[0]
    nbat = max(tm // rows_per_batch, 1)
    per = rows_per_batch // tm if rows_per_batch >= tm else 1

    def mem_map(i):
        return (i // per, 0, 0)

    def row(width):
        return pl.BlockSpec((tm, width), lambda i: (i, 0))

    return pl.pallas_call(
        functools.partial(_mix_ca_body, rows_per_batch),
        grid=(m // tm,),
        in_specs=[row(D_MODEL), row(SSD_WIDTH), row(ATT_WIDTH),
                  pl.BlockSpec((nbat, MEM_LEN, CA_WIDTH), mem_map),
                  pl.BlockSpec((nbat, MEM_LEN, CA_WIDTH), mem_map),
                  _full(wo1.shape), _full(wo2.shape), _full((1, D_MODEL)), _full(wq.shape), _full(wco.shape)],
        out_specs=row(D_MODEL),
        out_shape=jax.ShapeDtypeStruct((m, D_MODEL), F32),
        compiler_params=_params("parallel"),
        name="mix_ca",
    )(h, ys, ya, mk, mv, wo1, wo2, g_ca.reshape(1, D_MODEL), wq, wco)


FF_CHUNK = 1024


def _mlp_body(final, h_ref, g_ref, wu_ref, wd_ref, gf_ref, o_ref):
    h = h_ref[...]
    xn = _rms(h, g_ref[...]).astype(BF16)
    acc = h
    for c in range(FF // FF_CHUNK):
        u = jnp.dot(xn, wu_ref[:, c * FF_CHUNK:(c + 1) * FF_CHUNK], preferred_element_type=F32)
        u = jnp.maximum(u, 0.0)
        acc = acc + jnp.dot((u * u).astype(BF16), wd_ref[c * FF_CHUNK:(c + 1) * FF_CHUNK, :],
                            preferred_element_type=F32)
    o_ref[...] = _rms(acc, gf_ref[...]) if final else acc


def _mlp(h, g, wu, wd, g_final, final, tm):
    m = h.shape[0]
    row = pl.BlockSpec((tm, D_MODEL), lambda i: (i, 0))
    return pl.pallas_call(
        functools.partial(_mlp_body, final),
        grid=(m // tm,),
        in_specs=[row, _full((1, D_MODEL)), _full(wu.shape), _full(wd.shape), _full((1, D_MODEL))],
        out_specs=row,
        out_shape=jax.ShapeDtypeStruct((m, D_MODEL), F32),
        compiler_params=_params("parallel"),
        name="mlp",
    )(h, g.reshape(1, D_MODEL), wu, wd, g_final.reshape(1, D_MODEL))


def _layer_weights(l, w_in, w_out, w_ca_q, w_ca_kv, w_ca_o, w_up, w_down):
    w = w_in[l]
    o = 0
    parts = []
    for width in (SSD_WIDTH, CONV_CH, SSD_HEADS, ATT_WIDTH, ATT_WIDTH, ATT_WIDTH):
        parts.append(w[:, o:o + width])
        o += width
    parts[2] = jnp.pad(parts[2], ((0, 0), (0, LANES - SSD_HEADS)))
    wo = w_out[l]
    return dict(
        w_in=[p.astype(BF16) for p in parts],
        wo1=wo[:SSD_WIDTH].astype(BF16), wo2=wo[SSD_WIDTH:].astype(BF16),
        wq=w_ca_q[l].astype(BF16), wco=w_ca_o[l].astype(BF16),
        wkv=[w_ca_kv[l][:, :CA_WIDTH].astype(BF16), w_ca_kv[l][:, CA_WIDTH:].astype(BF16)],
        wu=w_up[l].astype(BF16), wd=w_down[l].astype(BF16))


def kernel(x_prompt, x_sample, cache_k, cache_v, cache_mem_k, cache_mem_v, state_conv, state_ssm,
           page_table, mem_prompt, norm_mix, w_in, conv_w, conv_b, dt_bias, a_log, d_skip, ssd_norm,
           rel_bias, w_out, norm_ca, norm_mem, w_ca_q, w_ca_kv, w_ca_o, norm_mlp, w_up, w_down, norm_final):
    depth = w_in.shape[0]
    pb, seq, _ = x_prompt.shape
    sb, dseq, _ = x_sample.shape
    page_size = cache_k.shape[2]
    tm = 512

    bias_tiles = _bias_tiles(rel_bias)
    cfar = rel_bias[REL_BUCKETS - 1]
    cache_kt = jnp.transpose(cache_k, (0, 1, 3, 4, 2))
    cache_vt = jnp.transpose(cache_v, (0, 1, 3, 4, 2))
    mem2d = mem_prompt.reshape(pb * MEM_LEN, D_MODEL)

    def pad_rows(a, rows):
        return jnp.pad(a.reshape(sb, dseq, ATT_WIDTH), ((0, 0), (0, rows - dseq), (0, 0)))

    h = x_prompt.reshape(pb * seq, D_MODEL)
    g = x_sample.reshape(sb * dseq, D_MODEL)
    outs = {n: [] for n in ("pk", "pv", "pmk", "pmv", "pconv", "pssm", "sk", "sv", "sconv", "sssm")}
    for l in range(depth):
        lw = _layer_weights(l, w_in, w_out, w_ca_q, w_ca_kv, w_ca_o, w_up, w_down)
        last = l == depth - 1

        z, xbc, dt, q, k, v = _norm_proj(h, norm_mix[l], lw["w_in"], tm)
        y_ssd, c_new, s_new = _ssd(xbc, z, dt, None, None, conv_w[l], conv_b[l], dt_bias[l], a_log[l],
                                   d_skip[l], ssd_norm[l], pb, seq)
        y_att = _moba_prompt(q, k, v, bias_tiles, cfar, pb, seq)
        mk, mv = _norm_proj(mem2d, norm_mem[l], lw["wkv"], min(tm, pb * MEM_LEN))
        h = _mix_ca(h, y_ssd, y_att, mk.reshape(pb, MEM_LEN, CA_WIDTH), mv.reshape(pb, MEM_LEN, CA_WIDTH),
                    lw["wo1"], lw["wo2"], norm_ca[l], lw["wq"], lw["wco"], seq, tm)
        h = _mlp(h, norm_mlp[l], lw["wu"], lw["wd"], norm_final, last, tm)
        outs["pk"].append(k.reshape(pb, seq, ATT_HEADS, ATT_HEAD_DIM))
        outs["pv"].append(v.reshape(pb, seq, ATT_HEADS, ATT_HEAD_DIM))
        outs["pmk"].append(mk.reshape(pb, MEM_LEN, CA_HEADS, CA_HEAD_DIM))
        outs["pmv"].append(mv.reshape(pb, MEM_LEN, CA_HEADS, CA_HEAD_DIM))
        outs["pconv"].append(c_new)
        outs["pssm"].append(s_new)

        ms = sb * dseq
        z, xbc, dt, q, k, v = _norm_proj(g, norm_mix[l], lw["w_in"], ms)
        y_ssd, c_new, s_new = _ssd(xbc, z, dt, state_conv[l], state_ssm[l], conv_w[l], conv_b[l], dt_bias[l],
                                   a_log[l], d_skip[l], ssd_norm[l], sb, dseq)
        y_att = _sample_moba(l, pad_rows(q, ROWS8), pad_rows(k, LANES), pad_rows(v, LANES), page_table,
                             cache_kt, cache_vt, bias_tiles, cfar, dseq)[:, :dseq].reshape(ms, ATT_WIDTH)
        g = _mix_ca(g, y_ssd, y_att, cache_mem_k[l].reshape(sb, MEM_LEN, CA_WIDTH),
                    cache_mem_v[l].reshape(sb, MEM_LEN, CA_WIDTH), lw["wo1"], lw["wo2"], norm_ca[l],
                    lw["wq"], lw["wco"], dseq, min(32, ms))
        g = _mlp(g, norm_mlp[l], lw["wu"], lw["wd"], norm_final, last, ms)
        outs["sk"].append(k.reshape(sb, dseq, ATT_HEADS, ATT_HEAD_DIM))
        outs["sv"].append(v.reshape(sb, dseq, ATT_HEADS, ATT_HEAD_DIM))
        outs["sconv"].append(c_new)
        outs["sssm"].append(s_new)

    st = {n: jnp.stack(vs) for n, vs in outs.items()}
    return (h.reshape(pb, seq, D_MODEL), g.reshape(sb, dseq, D_MODEL), st["pk"], st["pv"], st["pmk"], st["pmv"],
            st["pconv"], st["pssm"], st["sk"], st["sv"], st["sconv"], st["sssm"])
```

```python
import functools
import math

import numpy as np
import jax
import jax.numpy as jnp
from jax import lax
from jax.experimental import pallas as pl
from jax.experimental.pallas import tpu as pltpu

D_MODEL = 1024
SSD_HEADS = 16
SSD_HEAD_DIM = 64
SSD_WIDTH = SSD_HEADS * SSD_HEAD_DIM
SSD_GROUPS = 2
SSD_STATE = 128
SSD_CHUNK = 128
CONV_WIDTH = 4
CONV_CH = SSD_WIDTH + 2 * SSD_GROUPS * SSD_STATE
ATT_HEADS = 8
ATT_HEAD_DIM = 64
ATT_WIDTH = ATT_HEADS * ATT_HEAD_DIM
MOBA_BLOCK = 256
MOBA_TOPK = 3
REL_BUCKETS = 32
REL_MAX_DIST = 128
MEM_LEN = 256
CA_HEADS = 4
CA_HEAD_DIM = 128
CA_WIDTH = CA_HEADS * CA_HEAD_DIM
FF = 4 * D_MODEL
EPS = 1e-5

LANES = 128
VMEM_LIMIT = 56 * 1024 * 1024
NEG = -1e30

BF16 = jnp.bfloat16
F32 = jnp.float32
HI = lax.Precision.HIGHEST


def _params(*sem):
    return pltpu.CompilerParams(dimension_semantics=sem, vmem_limit_bytes=VMEM_LIMIT)


def _rms(x, gain):
    return x * lax.rsqrt(jnp.mean(x * x, axis=-1, keepdims=True) + EPS) * gain


def _dot(a, b):
    return jnp.dot(a.astype(BF16), b.astype(BF16), preferred_element_type=F32)


def _dot_t(a, b):
    return lax.dot_general(a.astype(BF16), b.astype(BF16), (((1,), (1,)), ((), ())),
                           preferred_element_type=F32)


def _full(shape):
    return pl.BlockSpec(shape, lambda *_: (0,) * len(shape))


def _norm_proj_body(n_out, x_ref, g_ref, *refs):
    w_refs, o_refs = refs[:n_out], refs[len(refs) - n_out:]
    xn = _rms(x_ref[...], g_ref[...]).astype(BF16)
    for w_ref, o_ref in zip(w_refs, o_refs):
        o_ref[...] = jnp.dot(xn, w_ref[...], preferred_element_type=F32).astype(o_ref.dtype)


def _norm_proj(x, gain, weights, tm, stacked=(), layer=0, depth=1, prev=None):
    m, d = x.shape
    n_out = len(weights)
    out_specs, out_shape = [], []
    for idx, w in enumerate(weights):
        n = w.shape[1]
        if idx in stacked:
            out_specs.append(pl.BlockSpec((None, tm, n), lambda i: (layer, i, 0)))
            out_shape.append(jax.ShapeDtypeStruct((depth, m, n), F32))
        else:
            out_specs.append(pl.BlockSpec((tm, n), lambda i: (i, 0)))
            out_shape.append(jax.ShapeDtypeStruct((m, n), F32))
    prev = list(prev) if prev is not None else []
    n_in = 2 + n_out
    aliases = {n_in + k: idx for k, idx in enumerate(stacked)} if prev else {}
    return pl.pallas_call(
        functools.partial(_norm_proj_body, n_out),
        grid=(m // tm,),
        in_specs=[pl.BlockSpec((tm, d), lambda i: (i, 0)), _full((1, d))]
        + [_full(w.shape) for w in weights] + [pl.BlockSpec(memory_space=pl.ANY) for _ in prev],
        out_specs=out_specs,
        out_shape=out_shape,
        input_output_aliases=aliases,
        compiler_params=_params("parallel"),
        name="norm_proj",
    )(x, gain.reshape(1, d), *weights, *prev)


CONV_PAD = 8


def _ssd_body(real_len, has_init, xbc_ref, z_ref, dt_ref, cs_ref, s0_ref, cw_ref, cb_ref, dtb_ref,
              alog_ref, dsk_ref, nw_ref, exp_ref, y_ref, cso_ref, so_ref, xpad_ref, st_ref):
    c = pl.program_id(1)
    nc = pl.num_programs(1)
    q = SSD_CHUNK
    n_pairs = SSD_HEADS // 2

    @pl.when(c == 0)
    def _():
        if has_init:
            xpad_ref[pl.ds(0, CONV_PAD), :] = jnp.zeros((CONV_PAD, CONV_CH), F32)
            xpad_ref[pl.ds(CONV_PAD - (CONV_WIDTH - 1), CONV_WIDTH - 1), :] = cs_ref[...]
            for p in range(n_pairs):
                st_ref[:, p * LANES:(p + 1) * LANES] = s0_ref[pl.ds(p * LANES, LANES), :].T
        else:
            xpad_ref[pl.ds(0, CONV_PAD), :] = jnp.zeros((CONV_PAD, CONV_CH), F32)
            st_ref[...] = jnp.zeros_like(st_ref)

    xpad_ref[pl.ds(CONV_PAD, q), :] = xbc_ref[...]

    acc = cb_ref[...] + xpad_ref[pl.ds(CONV_PAD - 3, q), :] * cw_ref[0:1, :]
    acc = acc + xpad_ref[pl.ds(CONV_PAD - 2, q), :] * cw_ref[1:2, :]
    acc = acc + xpad_ref[pl.ds(CONV_PAD - 1, q), :] * cw_ref[2:3, :]
    acc = acc + xpad_ref[pl.ds(CONV_PAD, q), :] * cw_ref[3:4, :]
    xc = acc * (1.0 / (1.0 + jnp.exp(-acc)))
    new_tail = xpad_ref[pl.ds(CONV_PAD + real_len - (CONV_WIDTH - 1), CONV_WIDTH - 1), :]

    @pl.when(c == nc - 1)
    def _():
        cso_ref[...] = new_tail

    xpad_ref[pl.ds(CONV_PAD - (CONV_WIDTH - 1), CONV_WIDTH - 1), :] = new_tail

    xs = xc[:, :SSD_WIDTH]
    gn = SSD_GROUPS * SSD_STATE

    dtx = dt_ref[...] + dtb_ref[...]
    dt = jnp.maximum(dtx, 0.0) + jnp.log1p(jnp.exp(-jnp.abs(dtx)))
    if real_len < q:
        row = lax.broadcasted_iota(jnp.int32, (q, LANES), 0)
        dt = jnp.where(row < real_len, dt, 0.0)
    a = -jnp.exp(alog_ref[...])
    da = dt * a
    ii = lax.broadcasted_iota(jnp.int32, (q, q), 0)
    jj = lax.broadcasted_iota(jnp.int32, (q, q), 1)
    causal = ii >= jj
    tril = jnp.where(causal, 1.0, 0.0).astype(F32)
    acum = jnp.dot(tril, da, preferred_element_type=F32, precision=HI)
    acum_t = acum.T
    dt_t = dt.T
    alast = acum[q - 1:q, :]
    ea = jnp.exp(acum)
    wdec = jnp.exp(alast - acum) * dt
    alast8 = jnp.broadcast_to(alast, (8, LANES))
    dec_e = jnp.exp(jnp.dot(alast8, exp_ref[...], preferred_element_type=F32, precision=HI)[0:1, :])

    lane = lax.broadcasted_iota(jnp.int32, (q, LANES), 1)
    first = lane < SSD_HEAD_DIM
    y_parts = []
    xw_parts = []
    for g in range(SSD_GROUPS):
        bg = xc[:, SSD_WIDTH + g * SSD_STATE:SSD_WIDTH + (g + 1) * SSD_STATE]
        cg = xc[:, SSD_WIDTH + gn + g * SSD_STATE:SSD_WIDTH + gn + (g + 1) * SSD_STATE]
        bg_t = bg.T
        scores = _dot(cg, bg_t)
        per_group = SSD_HEADS // SSD_GROUPS
        for pp in range(per_group // 2):
            p = g * (per_group // 2) + pp
            x_pair = xs[:, p * LANES:(p + 1) * LANES]
            st_pair = st_ref[:, p * LANES:(p + 1) * LANES]
            rhs = jnp.concatenate([x_pair, st_pair], axis=0).astype(BF16)
            outs = []
            for k in range(2):
                h = 2 * p + k
                a_col = acum[:, h:h + 1]
                a_row = acum_t[h:h + 1, :]
                seg = jnp.where(causal, a_col - a_row, 0.0)
                m_h = jnp.where(causal, jnp.exp(seg), 0.0) * scores * dt_t[h:h + 1, :]
                c_h = cg * ea[:, h:h + 1]
                lhs = jnp.concatenate([m_h, c_h], axis=1).astype(BF16)
                outs.append(jnp.dot(lhs, rhs, preferred_element_type=F32))
            y_parts.append(jnp.where(first, outs[0], outs[1]))
            w_pair = jnp.where(first, wdec[:, 2 * p:2 * p + 1], wdec[:, 2 * p + 1:2 * p + 2])
            xw_parts.append(x_pair * w_pair)
        half = SSD_WIDTH // SSD_GROUPS
        xw_g = jnp.concatenate(xw_parts[-(per_group // 2):], axis=1)
        upd = _dot(bg_t, xw_g)
        st_ref[:, g * half:(g + 1) * half] = (
            st_ref[:, g * half:(g + 1) * half] * dec_e[:, g * half:(g + 1) * half] + upd)

    y = jnp.concatenate(y_parts, axis=1) + xs * dsk_ref[...]
    zz = z_ref[...]
    y = y * (zz * (1.0 / (1.0 + jnp.exp(-zz))))
    y_ref[...] = _rms(y, nw_ref[...])

    @pl.when(c == nc - 1)
    def _():
        for p in range(n_pairs):
            so_ref[pl.ds(p * LANES, LANES), :] = st_ref[:, p * LANES:(p + 1) * LANES].T


def _ssd(xbc, z, dt, conv_state, ssm_state, conv_w, conv_b, dt_bias, a_log, d_skip, norm_w, bsz, seq, layer=0):
    has_init = conv_state is not None
    real_len = min(seq, SSD_CHUNK)
    nc = max(seq // SSD_CHUNK, 1)
    if not has_init:
        conv_state = jnp.zeros((1, bsz, CONV_WIDTH - 1, CONV_CH), F32)
        ssm_state = jnp.zeros((1, bsz, SSD_HEADS, SSD_HEAD_DIM, SSD_STATE), F32)
    s0 = ssm_state.reshape(ssm_state.shape[0], bsz, SSD_WIDTH, SSD_STATE)
    pad_h = LANES - SSD_HEADS
    dtb = jnp.pad(dt_bias, (0, pad_h)).reshape(1, LANES)
    alog = jnp.pad(a_log, (0, pad_h)).reshape(1, LANES)
    dsk = jnp.repeat(d_skip, SSD_HEAD_DIM).reshape(1, SSD_WIDTH)
    expand = (np.arange(LANES)[:, None] == (np.arange(SSD_WIDTH) // SSD_HEAD_DIM)[None, :]).astype(np.float32)
    xbc3 = xbc.reshape(bsz, seq, CONV_CH)
    z3 = z.reshape(bsz, seq, SSD_WIDTH)
    dt3 = dt.reshape(bsz, seq, LANES)
    seq_p = nc * SSD_CHUNK
    if seq_p != seq:
        pad = ((0, 0), (0, seq_p - seq), (0, 0))
        xbc3, z3, dt3 = jnp.pad(xbc3, pad), jnp.pad(z3, pad), jnp.pad(dt3, pad)

    def tok(width):
        return pl.BlockSpec((None, SSD_CHUNK, width), lambda b, c: (b, c, 0))

    y, cso, so = pl.pallas_call(
        functools.partial(_ssd_body, real_len, has_init),
        grid=(bsz, nc),
        in_specs=[tok(CONV_CH), tok(SSD_WIDTH), tok(LANES),
                  pl.BlockSpec((None, None, CONV_WIDTH - 1, CONV_CH), lambda b, c: (layer, b, 0, 0)),
                  pl.BlockSpec((None, None, SSD_WIDTH, SSD_STATE), lambda b, c: (layer, b, 0, 0)),
                  _full((CONV_WIDTH, CONV_CH)), _full((1, CONV_CH)), _full((1, LANES)), _full((1, LANES)),
                  _full((1, SSD_WIDTH)), _full((1, SSD_WIDTH)), _full((LANES, SSD_WIDTH))],
        out_specs=[tok(SSD_WIDTH),
                   pl.BlockSpec((None, CONV_WIDTH - 1, CONV_CH), lambda b, c: (b, 0, 0)),
                   pl.BlockSpec((None, SSD_WIDTH, SSD_STATE), lambda b, c: (b, 0, 0))],
        out_shape=[jax.ShapeDtypeStruct((bsz, seq_p, SSD_WIDTH), F32),
                   jax.ShapeDtypeStruct((bsz, CONV_WIDTH - 1, CONV_CH), F32),
                   jax.ShapeDtypeStruct((bsz, SSD_WIDTH, SSD_STATE), F32)],
        scratch_shapes=[pltpu.VMEM((CONV_PAD + SSD_CHUNK, CONV_CH), F32),
                        pltpu.VMEM((SSD_STATE, SSD_WIDTH), F32)],
        compiler_params=_params("parallel", "arbitrary"),
        name="ssd_scan",
    )(xbc3, z3, dt3, conv_state, s0, conv_w, conv_b.reshape(1, CONV_CH), dtb, alog, dsk,
      norm_w.reshape(1, SSD_WIDTH), jnp.asarray(expand))
    return (y[:, :seq].reshape(bsz * seq, SSD_WIDTH), cso,
            so.reshape(bsz, SSD_HEADS, SSD_HEAD_DIM, SSD_STATE))


def _t5_bucket_np(rel):
    n = np.maximum(rel, 0)
    max_exact = REL_BUCKETS // 2
    nf = np.maximum(n, 1).astype(np.float32)
    large = max_exact + (np.log(nf / np.float32(max_exact)) / np.float32(math.log(REL_MAX_DIST / max_exact))
                         * np.float32(REL_BUCKETS - max_exact)).astype(np.int32)
    large = np.minimum(large, REL_BUCKETS - 1)
    return np.where(n < max_exact, n, large).astype(np.int32)


def _bias_tiles_body(bk_ref, rb_ref, o_ref):
    h = pl.program_id(0)
    ii = lax.broadcasted_iota(jnp.int32, (MOBA_BLOCK, MOBA_BLOCK), 0)
    jj = lax.broadcasted_iota(jnp.int32, (MOBA_BLOCK, MOBA_BLOCK), 1)
    for t in range(2):
        bk = bk_ref[t]
        acc = jnp.zeros((MOBA_BLOCK, MOBA_BLOCK), F32)
        for u in range(REL_BUCKETS):
            acc = jnp.where(bk == u, rb_ref[u, h], acc)
        if t == 0:
            acc = jnp.where(jj >= ii, acc, NEG)
        o_ref[t] = acc


def _bias_tiles(rel_bias):
    i = np.arange(MOBA_BLOCK)[None, :]
    j = np.arange(MOBA_BLOCK)[:, None]
    buckets = np.stack([_t5_bucket_np(i - j), _t5_bucket_np(MOBA_BLOCK + i - j)])
    return pl.pallas_call(
        _bias_tiles_body,
        grid=(ATT_HEADS,),
        in_specs=[_full((2, MOBA_BLOCK, MOBA_BLOCK)),
                  pl.BlockSpec(memory_space=pltpu.SMEM)],
        out_specs=pl.BlockSpec((None, 2, MOBA_BLOCK, MOBA_BLOCK), lambda h: (h, 0, 0, 0)),
        out_shape=jax.ShapeDtypeStruct((ATT_HEADS, 2, MOBA_BLOCK, MOBA_BLOCK), F32),
        compiler_params=_params("parallel"),
        name="bias_tiles",
    )(jnp.asarray(buckets), rel_bias)


def _block_rank(g, blk):
    rank = jnp.zeros(g.shape, jnp.int32)
    for m in range(g.shape[1]):
        gm = g[:, m:m + 1]
        rank = rank + ((gm > g) | ((gm == g) & (m < blk))).astype(jnp.int32)
    return rank


def _block_rank_t(g, blk):
    rank = jnp.zeros(g.shape, jnp.int32)
    for m in range(g.shape[0]):
        gm = g[m:m + 1, :]
        rank = rank + ((gm > g) | ((gm == g) & (m < blk))).astype(jnp.int32)
    return rank


def _moba_prompt_body(nblk, q_ref, k_ref, v_ref, bias_ref, cfar_ref, o_ref, km_ref, k16_ref, vt_ref, qt_ref):
    pair = pl.program_id(0)
    blk_rows = MOBA_BLOCK
    hd = ATT_HEAD_DIM
    scale = hd ** -0.5
    for n in range(nblk):
        kb = k_ref[pl.ds(n * blk_rows, blk_rows), :]
        km_ref[n:n + 1, :] = jnp.mean(kb, axis=0, keepdims=True)
        k16_ref[pl.ds(n * blk_rows, blk_rows), :] = kb.astype(BF16)
    for c in range(nblk * blk_rows // LANES):
        vt_ref[:, c * LANES:(c + 1) * LANES] = v_ref[pl.ds(c * LANES, LANES), :].T.astype(BF16)
        qt_ref[:, c * LANES:(c + 1) * LANES] = (q_ref[pl.ds(c * LANES, LANES), :] * scale).T.astype(BF16)
    kmean = km_ref[...].astype(BF16)
    row = lax.broadcasted_iota(jnp.int32, (LANES, blk_rows), 0)
    blk = lax.broadcasted_iota(jnp.int32, (nblk, blk_rows), 0)
    zero16 = jnp.zeros((LANES, blk_rows), BF16)

    for i in range(nblk):
        qt = qt_ref[:, i * blk_rows:(i + 1) * blk_rows]
        n_keys = (i + 1) * blk_rows
        halves = []
        for kk in range(2):
            qth = jnp.where(row < hd, qt, zero16) if kk == 0 else jnp.where(row >= hd, qt, zero16)
            gate = jnp.dot(kmean, qth, preferred_element_type=F32)
            past = blk < i
            gate = jnp.where(past, gate, NEG)
            sel = past & (_block_rank_t(gate, blk) < MOBA_TOPK)
            pen = jnp.where(sel, cfar_ref[2 * pair + kk], NEG)
            s = jnp.dot(k16_ref[0:n_keys, :], qth, preferred_element_type=F32)
            segs = []
            for n in range(i + 1):
                seg = s[n * blk_rows:(n + 1) * blk_rows]
                if n == i:
                    seg = seg + bias_ref[kk, 0]
                elif n == i - 1:
                    seg = seg + bias_ref[kk, 1] + jnp.where(sel[n:n + 1, :], 0.0, NEG)
                else:
                    seg = seg + pen[n:n + 1, :]
                segs.append(seg)
            m = segs[0]
            for seg in segs[1:]:
                m = jnp.maximum(m, seg)
            m = jnp.max(m, axis=0, keepdims=True)
            probs = [jnp.exp(seg - m) for seg in segs]
            tot = probs[0]
            for p in probs[1:]:
                tot = tot + p
            l = jnp.sum(tot, axis=0, keepdims=True)
            p16 = jnp.concatenate([p.astype(BF16) for p in probs], axis=0)
            acc = jnp.dot(vt_ref[kk * hd:(kk + 1) * hd, 0:n_keys], p16, preferred_element_type=F32)
            halves.append(acc / l)
        o_ref[pl.ds(i * blk_rows, blk_rows), :] = jnp.concatenate(halves, axis=0).T


def _moba_prompt(q, k, v, bias_tiles, cfar, bsz, seq, layer):
    nblk = seq // MOBA_BLOCK
    n_pairs = ATT_HEADS // 2
    tok = pl.BlockSpec((seq, LANES), lambda p, b: (b, p))
    tok_l = pl.BlockSpec((None, seq, LANES), lambda p, b: (layer, b, p))
    return pl.pallas_call(
        functools.partial(_moba_prompt_body, nblk),
        grid=(n_pairs, bsz),
        in_specs=[tok, tok_l, tok_l,
                  pl.BlockSpec((2, 2, MOBA_BLOCK, MOBA_BLOCK), lambda p, b: (p, 0, 0, 0)),
                  pl.BlockSpec(memory_space=pltpu.SMEM)],
        out_specs=tok,
        out_shape=jax.ShapeDtypeStruct((bsz * seq, ATT_WIDTH), F32),
        scratch_shapes=[pltpu.VMEM((nblk, LANES), F32), pltpu.VMEM((seq, LANES), BF16),
                        pltpu.VMEM((LANES, seq), BF16), pltpu.VMEM((LANES, seq), BF16)],
        compiler_params=_params("parallel", "parallel"),
        name="moba_prompt",
    )(q, k, v, bias_tiles, cfar)


PAGES_PER_STEP = 16
ROWS8 = 8


def _sample_moba_body(layer, n_new, n_k, page_size, pt_ref, q_ref, kn_ref, vn_ref, tprev_ref, town_ref, cfar_ref,
                      ck_ref, cv_ref, o_ref, buf_ref, sem_ref, s_ref, acc_ref, l_ref):
    b = pl.program_id(0)
    n_seq = pl.num_programs(0)
    n_chunks = 2 * n_k
    n_tok = s_ref.shape[2]
    n_blk = n_tok // MOBA_BLOCK
    q = q_ref[...]
    qh = [q[:, h * ATT_HEAD_DIM:(h + 1) * ATT_HEAD_DIM] for h in range(ATT_HEADS)]
    scale = ATT_HEAD_DIM ** -0.5

    def chunk_copies(seq, g):
        src = ck_ref if g < n_k else cv_ref
        slot = g % 2
        return [pltpu.make_async_copy(src.at[layer, pt_ref[seq, (g % n_k) * PAGES_PER_STEP + r]],
                                      buf_ref.at[slot, r], sem_ref.at[slot])
                for r in range(PAGES_PER_STEP)]

    def start_chunk(seq, g):
        for c in chunk_copies(seq, g):
            c.start()

    def advance(g):
        for c in chunk_copies(b, g):
            c.wait()
        if g + 1 < n_chunks:
            start_chunk(b, g + 1)
        else:
            @pl.when(b + 1 < n_seq)
            def _():
                start_chunk(b + 1, 0)

    @pl.when(b == 0)
    def _():
        start_chunk(b, 0)

    for g in range(n_k):
        advance(g)
        for r in range(PAGES_PER_STEP):
            start = (g * PAGES_PER_STEP + r) * page_size
            for h in range(ATT_HEADS):
                s_ref[h, :, start:start + page_size] = _dot(qh[h], buf_ref[g % 2, r, h])

    def softmax_over_selected():
        blk = lax.broadcasted_iota(jnp.int32, (ROWS8, n_blk), 1)
        col = lax.broadcasted_iota(jnp.int32, (ROWS8, LANES), 1)
        halves_per_blk = MOBA_BLOCK // LANES
        for h in range(ATT_HEADS):
            s_all = s_ref[h]
            chunks = [s_all[:, c * LANES:(c + 1) * LANES] for c in range(n_tok // LANES)]
            gate = jnp.zeros((ROWS8, n_blk), F32)
            for n in range(n_blk):
                part = chunks[n * halves_per_blk]
                for c in chunks[n * halves_per_blk + 1:(n + 1) * halves_per_blk]:
                    part = part + c
                gate = jnp.where(blk == n, jnp.sum(part, axis=1, keepdims=True) * (1.0 / MOBA_BLOCK), gate)
            sel = _block_rank(gate, blk) < MOBA_TOPK
            pen = jnp.where(sel, cfar_ref[h], NEG)
            pen_last = jnp.where(sel[:, n_blk - 1:n_blk], 0.0, NEG)
            logits = []
            for c, x in enumerate(chunks):
                n = c // halves_per_blk
                if n == n_blk - 1:
                    off = (c % halves_per_blk) * LANES
                    logits.append(x * scale + tprev_ref[h, :, off:off + LANES] + pen_last)
                else:
                    logits.append(x * scale + pen[:, n:n + 1])
            kn_h = kn_ref[:, h * ATT_HEAD_DIM:(h + 1) * ATT_HEAD_DIM]
            vn_h = vn_ref[:, h * ATT_HEAD_DIM:(h + 1) * ATT_HEAD_DIM]
            s_own = jnp.where(col < n_new, _dot_t(qh[h] * scale, kn_h) + town_ref[h], NEG)
            m_el = s_own
            for x in logits:
                m_el = jnp.maximum(m_el, x)
            m = jnp.max(m_el, axis=1, keepdims=True)
            p_own = jnp.exp(s_own - m)
            tot = p_own
            for c, x in enumerate(logits):
                p = jnp.exp(x - m)
                s_ref[h, :, c * LANES:(c + 1) * LANES] = p
                tot = tot + p
            acc_ref[h] = _dot(p_own, vn_h)
            l_ref[h] = jnp.broadcast_to(jnp.sum(tot, axis=1, keepdims=True), (ROWS8, LANES))

    softmax_over_selected()

    for g in range(n_k, n_chunks):
        advance(g)
        for h in range(ATT_HEADS):
            acc = acc_ref[h]
            for r in range(PAGES_PER_STEP):
                start = ((g - n_k) * PAGES_PER_STEP + r) * page_size
                acc = acc + _dot_t(s_ref[h, :, start:start + page_size], buf_ref[g % 2, r, h])
            acc_ref[h] = acc

    o_ref[...] = jnp.concatenate(
        [acc_ref[h] / l_ref[h][:, :ATT_HEAD_DIM] for h in range(ATT_HEADS)], axis=1)


def _sample_moba(layer, q8, kn, vn, page_table, cache_kt, cache_vt, bias_tiles, cfar, n_new):
    bsz = q8.shape[0]
    page_size = cache_kt.shape[4]
    n_pages = page_table.shape[1]
    n_k = n_pages // PAGES_PER_STEP
    tprev = jnp.transpose(bias_tiles[:, 1, :, :ROWS8], (0, 2, 1))
    town = jnp.transpose(bias_tiles[:, 0, :LANES, :ROWS8], (0, 2, 1))
    grid_spec = pltpu.PrefetchScalarGridSpec(
        num_scalar_prefetch=1,
        grid=(bsz,),
        in_specs=[pl.BlockSpec((None, ROWS8, ATT_WIDTH), lambda b, pt: (b, 0, 0)),
                  pl.BlockSpec((None, LANES, ATT_WIDTH), lambda b, pt: (b, 0, 0)),
                  pl.BlockSpec((None, LANES, ATT_WIDTH), lambda b, pt: (b, 0, 0)),
                  pl.BlockSpec((ATT_HEADS, ROWS8, MOBA_BLOCK), lambda b, pt: (0, 0, 0)),
                  pl.BlockSpec((ATT_HEADS, ROWS8, LANES), lambda b, pt: (0, 0, 0)),
                  pl.BlockSpec(memory_space=pltpu.SMEM),
                  pl.BlockSpec(memory_space=pl.ANY),
                  pl.BlockSpec(memory_space=pl.ANY)],
        out_specs=pl.BlockSpec((None, ROWS8, ATT_WIDTH), lambda b, pt: (b, 0, 0)),
        scratch_shapes=[pltpu.VMEM((2, PAGES_PER_STEP, ATT_HEADS, ATT_HEAD_DIM, page_size), F32),
                        pltpu.SemaphoreType.DMA((2,)),
                        pltpu.VMEM((ATT_HEADS, ROWS8, n_pages * page_size), F32),
                        pltpu.VMEM((ATT_HEADS, ROWS8, ATT_HEAD_DIM), F32),
                        pltpu.VMEM((ATT_HEADS, ROWS8, LANES), F32)],
    )
    return pl.pallas_call(
        functools.partial(_sample_moba_body, layer, n_new, n_k, page_size),
        grid_spec=grid_spec,
        out_shape=jax.ShapeDtypeStruct((bsz, ROWS8, ATT_WIDTH), F32),
        compiler_params=_params("arbitrary"),
        name="sample_moba",
    )(page_table, q8, kn, vn, tprev, town, cfar, cache_kt, cache_vt)


def _mix_ca_body(rows_per_batch, h_ref, ys_ref, ya_ref, mk_ref, mv_ref, wo1_ref, wo2_ref, g_ref,
                 wq_ref, wco_ref, o_ref):
    tm = h_ref.shape[0]
    h1 = (h_ref[...] + jnp.dot(ys_ref[...].astype(BF16), wo1_ref[...], preferred_element_type=F32)
          + jnp.dot(ya_ref[...].astype(BF16), wo2_ref[...], preferred_element_type=F32))
    hn = _rms(h1, g_ref[...]).astype(BF16)
    qq = jnp.dot(hn, wq_ref[...], preferred_element_type=F32) * (CA_HEAD_DIM ** -0.5)
    nkeys = mk_ref.shape[0] * mk_ref.shape[1]
    mk = mk_ref[...].reshape(nkeys, CA_WIDTH)
    mv = mv_ref[...].reshape(nkeys, CA_WIDTH)
    if rows_per_batch < tm:
        r = lax.broadcasted_iota(jnp.int32, (tm, nkeys), 0) // rows_per_batch
        c = lax.broadcasted_iota(jnp.int32, (tm, nkeys), 1) // MEM_LEN
        same = r == c
    outs = []
    for hd in range(CA_HEADS):
        sl = slice(hd * CA_HEAD_DIM, (hd + 1) * CA_HEAD_DIM)
        s = _dot_t(qq[:, sl], mk[:, sl])
        if rows_per_batch < tm:
            s = jnp.where(same, s, NEG)
        s = s - jnp.max(s, axis=1, keepdims=True)
        p = jnp.exp(s)
        p = p / jnp.sum(p, axis=1, keepdims=True)
        outs.append(_dot(p, mv[:, sl]))
    o = jnp.concatenate(outs, axis=1).astype(BF16)
    o_ref[...] = h1 + jnp.dot(o, wco_ref[...], preferred_element_type=F32)


def _mix_ca(h, ys, ya, mk, mv, wo1, wo2, g_ca, wq, wco, rows_per_batch, tm):
    m = h.shape[0]
    nbat = max(tm // rows_per_batch, 1)
    per = rows_per_batch // tm if rows_per_batch >= tm else 1

    def mem_map(i):
        return (i // per, 0, 0)

    def row(width):
        return pl.BlockSpec((tm, width), lambda i: (i, 0))

    return pl.pallas_call(
        functools.partial(_mix_ca_body, rows_per_batch),
        grid=(m // tm,),
        in_specs=[row(D_MODEL), row(SSD_WIDTH), row(ATT_WIDTH),
                  pl.BlockSpec((nbat, MEM_LEN, CA_WIDTH), mem_map),
                  pl.BlockSpec((nbat, MEM_LEN, CA_WIDTH), mem_map),
                  _full(wo1.shape), _full(wo2.shape), _full((1, D_MODEL)), _full(wq.shape), _full(wco.shape)],
        out_specs=row(D_MODEL),
        out_shape=jax.ShapeDtypeStruct((m, D_MODEL), F32),
        compiler_params=_params("parallel"),
        name="mix_ca",
    )(h, ys, ya, mk, mv, wo1, wo2, g_ca.reshape(1, D_MODEL), wq, wco)


FF_CHUNK = 1024


def _mlp_body(final, h_ref, g_ref, wu_ref, wd_ref, gf_ref, o_ref):
    h = h_ref[...]
    xn = _rms(h, g_ref[...]).astype(BF16)
    acc = h
    for c in range(FF // FF_CHUNK):
        u = jnp.dot(xn, wu_ref[:, c * FF_CHUNK:(c + 1) * FF_CHUNK], preferred_element_type=F32)
        u = jnp.maximum(u, 0.0)
        acc = acc + jnp.dot((u * u).astype(BF16), wd_ref[c * FF_CHUNK:(c + 1) * FF_CHUNK, :],
                            preferred_element_type=F32)
    o_ref[...] = _rms(acc, gf_ref[...]) if final else acc


def _mlp(h, g, wu, wd, g_final, final, tm):
    m = h.shape[0]
    row = pl.BlockSpec((tm, D_MODEL), lambda i: (i, 0))
    return pl.pallas_call(
        functools.partial(_mlp_body, final),
        grid=(m // tm,),
        in_specs=[row, _full((1, D_MODEL)), _full(wu.shape), _full(wd.shape), _full((1, D_MODEL))],
        out_specs=row,
        out_shape=jax.ShapeDtypeStruct((m, D_MODEL), F32),
        compiler_params=_params("parallel"),
        name="mlp",
    )(h, g.reshape(1, D_MODEL), wu, wd, g_final.reshape(1, D_MODEL))


def _layer_weights(l, w_in, w_out, w_ca_q, w_ca_kv, w_ca_o, w_up, w_down):
    w = w_in[l]
    o = 0
    parts = []
    for width in (SSD_WIDTH, CONV_CH, SSD_HEADS, ATT_WIDTH, ATT_WIDTH, ATT_WIDTH):
        parts.append(w[:, o:o + width])
        o += width
    parts[2] = jnp.pad(parts[2], ((0, 0), (0, LANES - SSD_HEADS)))
    wo = w_out[l]
    return dict(
        w_in=[p.astype(BF16) for p in parts],
        wo1=wo[:SSD_WIDTH].astype(BF16), wo2=wo[SSD_WIDTH:].astype(BF16),
        wq=w_ca_q[l].astype(BF16), wco=w_ca_o[l].astype(BF16),
        wkv=[w_ca_kv[l][:, :CA_WIDTH].astype(BF16), w_ca_kv[l][:, CA_WIDTH:].astype(BF16)],
        wu=w_up[l].astype(BF16), wd=w_down[l].astype(BF16))


def kernel(x_prompt, x_sample, cache_k, cache_v, cache_mem_k, cache_mem_v, state_conv, state_ssm,
           page_table, mem_prompt, norm_mix, w_in, conv_w, conv_b, dt_bias, a_log, d_skip, ssd_norm,
           rel_bias, w_out, norm_ca, norm_mem, w_ca_q, w_ca_kv, w_ca_o, norm_mlp, w_up, w_down, norm_final):
    depth = w_in.shape[0]
    pb, seq, _ = x_prompt.shape
    sb, dseq, _ = x_sample.shape
    page_size = cache_k.shape[2]
    tm = 512

    bias_tiles = _bias_tiles(rel_bias)
    cfar = rel_bias[REL_BUCKETS - 1]
    cache_kt = jnp.transpose(cache_k, (0, 1, 3, 4, 2))
    cache_vt = jnp.transpose(cache_v, (0, 1, 3, 4, 2))
    mem2d = mem_prompt.reshape(pb * MEM_LEN, D_MODEL)

    def pad_rows(a, rows):
        return jnp.pad(a.reshape(sb, dseq, ATT_WIDTH), ((0, 0), (0, rows - dseq), (0, 0)))

    h = x_prompt.reshape(pb * seq, D_MODEL)
    g = x_sample.reshape(sb * dseq, D_MODEL)
    outs = {n: [] for n in ("pmk", "pmv", "pconv", "pssm", "sk", "sv", "sconv", "sssm")}
    pk = jnp.zeros((depth, pb * seq, ATT_WIDTH), F32)
    pv = jnp.zeros((depth, pb * seq, ATT_WIDTH), F32)
    for l in range(depth):
        lw = _layer_weights(l, w_in, w_out, w_ca_q, w_ca_kv, w_ca_o, w_up, w_down)
        last = l == depth - 1

        z, xbc, dt, q, pk, pv = _norm_proj(h, norm_mix[l], lw["w_in"], tm, stacked=(4, 5), layer=l, depth=depth,
                                           prev=[pk, pv])
        y_ssd, c_new, s_new = _ssd(xbc, z, dt, None, None, conv_w[l], conv_b[l], dt_bias[l], a_log[l],
                                   d_skip[l], ssd_norm[l], pb, seq)
        y_att = _moba_prompt(q, pk, pv, bias_tiles, cfar, pb, seq, l)
        mk, mv = _norm_proj(mem2d, norm_mem[l], lw["wkv"], min(tm, pb * MEM_LEN))
        h = _mix_ca(h, y_ssd, y_att, mk.reshape(pb, MEM_LEN, CA_WIDTH), mv.reshape(pb, MEM_LEN, CA_WIDTH),
                    lw["wo1"], lw["wo2"], norm_ca[l], lw["wq"], lw["wco"], seq, tm)
        h = _mlp(h, norm_mlp[l], lw["wu"], lw["wd"], norm_final, last, tm)
        outs["pmk"].append(mk.reshape(pb, MEM_LEN, CA_HEADS, CA_HEAD_DIM))
        outs["pmv"].append(mv.reshape(pb, MEM_LEN, CA_HEADS, CA_HEAD_DIM))
        outs["pconv"].append(c_new)
        outs["pssm"].append(s_new)

        ms = sb * dseq
        z, xbc, dt, q, k, v = _norm_proj(g, norm_mix[l], lw["w_in"], ms)
        y_ssd, c_new, s_new = _ssd(xbc, z, dt, state_conv, state_ssm, conv_w[l], conv_b[l], dt_bias[l],
                                   a_log[l], d_skip[l], ssd_norm[l], sb, dseq, layer=l)
        y_att = _sample_moba(l, pad_rows(q, ROWS8), pad_rows(k, LANES), pad_rows(v, LANES), page_table,
                             cache_kt, cache_vt, bias_tiles, cfar, dseq)[:, :dseq].reshape(ms, ATT_WIDTH)
        g = _mix_ca(g, y_ssd, y_att, cache_mem_k[l].reshape(sb, MEM_LEN, CA_WIDTH),
                    cache_mem_v[l].reshape(sb, MEM_LEN, CA_WIDTH), lw["wo1"], lw["wo2"], norm_ca[l],
                    lw["wq"], lw["wco"], dseq, min(32, ms))
        g = _mlp(g, norm_mlp[l], lw["wu"], lw["wd"], norm_final, last, ms)
        outs["sk"].append(k.reshape(sb, dseq, ATT_HEADS, ATT_HEAD_DIM))
        outs["sv"].append(v.reshape(sb, dseq, ATT_HEADS, ATT_HEAD_DIM))
        outs["sconv"].append(c_new)
        outs["sssm"].append(s_new)

    st = {n: jnp.stack(vs) for n, vs in outs.items() if vs}
    kv_shape = (depth, pb, seq, ATT_HEADS, ATT_HEAD_DIM)
    return (h.reshape(pb, seq, D_MODEL), g.reshape(sb, dseq, D_MODEL), pk.reshape(kv_shape), pv.reshape(kv_shape),
            st["pmk"], st["pmv"],
            st["pconv"], st["pssm"], st["sk"], st["sv"], st["sconv"], st["sssm"])
```

```python
import functools
import math

import numpy as np
import jax
import jax.numpy as jnp
from jax import lax
from jax.experimental import pallas as pl
from jax.experimental.pallas import tpu as pltpu

D_MODEL = 1024
SSD_HEADS = 16
SSD_HEAD_DIM = 64
SSD_WIDTH = SSD_HEADS * SSD_HEAD_DIM
SSD_GROUPS = 2
SSD_STATE = 128
SSD_CHUNK = 128
CONV_WIDTH = 4
CONV_CH = SSD_WIDTH + 2 * SSD_GROUPS * SSD_STATE
ATT_HEADS = 8
ATT_HEAD_DIM = 64
ATT_WIDTH = ATT_HEADS * ATT_HEAD_DIM
MOBA_BLOCK = 256
MOBA_TOPK = 3
REL_BUCKETS = 32
REL_MAX_DIST = 128
MEM_LEN = 256
CA_HEADS = 4
CA_HEAD_DIM = 128
CA_WIDTH = CA_HEADS * CA_HEAD_DIM
FF = 4 * D_MODEL
EPS = 1e-5

LANES = 128
VMEM_LIMIT = 56 * 1024 * 1024
NEG = -1e30

BF16 = jnp.bfloat16
F32 = jnp.float32
HI = lax.Precision.HIGHEST


def _params(*sem):
    return pltpu.CompilerParams(dimension_semantics=sem, vmem_limit_bytes=VMEM_LIMIT)


def _rms(x, gain):
    return x * lax.rsqrt(jnp.mean(x * x, axis=-1, keepdims=True) + EPS) * gain


def _dot(a, b):
    return jnp.dot(a.astype(BF16), b.astype(BF16), preferred_element_type=F32)


def _dot_t(a, b):
    return lax.dot_general(a.astype(BF16), b.astype(BF16), (((1,), (1,)), ((), ())),
                           preferred_element_type=F32)


def _full(shape):
    return pl.BlockSpec(shape, lambda *_: (0,) * len(shape))


def _norm_proj_body(n_out, transposed, x_ref, g_ref, *refs):
    w_refs, o_refs = refs[:n_out], refs[len(refs) - n_out:]
    xn = _rms(x_ref[...], g_ref[...]).astype(BF16)
    for idx, (w_ref, o_ref) in enumerate(zip(w_refs, o_refs)):
        if idx in transposed:
            o_ref[...] = lax.dot_general(w_ref[...], xn, (((1,), (1,)), ((), ())), preferred_element_type=F32)
        else:
            o_ref[...] = jnp.dot(xn, w_ref[...], preferred_element_type=F32)


def _norm_proj(x, gain, weights, tm, transposed=(), layer=0, depth=1, seq=None, prev=None):
    m, d = x.shape
    n_out = len(weights)
    out_specs, out_shape = [], []
    for idx, w in enumerate(weights):
        if idx in transposed:
            n = w.shape[0]
            per = seq // tm
            out_specs.append(pl.BlockSpec((None, None, n, tm), lambda i: (layer, i // per, 0, i % per)))
            out_shape.append(jax.ShapeDtypeStruct((depth, m // seq, n, seq), F32))
        else:
            n = w.shape[1]
            out_specs.append(pl.BlockSpec((tm, n), lambda i: (i, 0)))
            out_shape.append(jax.ShapeDtypeStruct((m, n), F32))
    prev = list(prev) if prev is not None else []
    n_in = 2 + n_out
    aliases = {n_in + k: idx for k, idx in enumerate(transposed)} if prev else {}
    return pl.pallas_call(
        functools.partial(_norm_proj_body, n_out, tuple(transposed)),
        grid=(m // tm,),
        in_specs=[pl.BlockSpec((tm, d), lambda i: (i, 0)), _full((1, d))]
        + [_full(w.shape) for w in weights] + [pl.BlockSpec(memory_space=pl.ANY) for _ in prev],
        out_specs=out_specs,
        out_shape=out_shape,
        input_output_aliases=aliases,
        compiler_params=_params("parallel"),
        name="norm_proj",
    )(x, gain.reshape(1, d), *weights, *prev)


CONV_PAD = 8


def _ssd_body(real_len, has_init, xbc_ref, z_ref, dt_ref, cs_ref, s0_ref, cw_ref, cb_ref, dtb_ref,
              alog_ref, dsk_ref, nw_ref, exp_ref, y_ref, cso_ref, so_ref, xpad_ref, st_ref):
    c = pl.program_id(1)
    nc = pl.num_programs(1)
    q = SSD_CHUNK
    n_pairs = SSD_HEADS // 2

    @pl.when(c == 0)
    def _():
        if has_init:
            xpad_ref[pl.ds(0, CONV_PAD), :] = jnp.zeros((CONV_PAD, CONV_CH), F32)
            xpad_ref[pl.ds(CONV_PAD - (CONV_WIDTH - 1), CONV_WIDTH - 1), :] = cs_ref[...]
            for p in range(n_pairs):
                st_ref[:, p * LANES:(p + 1) * LANES] = s0_ref[pl.ds(p * LANES, LANES), :].T
        else:
            xpad_ref[pl.ds(0, CONV_PAD), :] = jnp.zeros((CONV_PAD, CONV_CH), F32)
            st_ref[...] = jnp.zeros_like(st_ref)

    xpad_ref[pl.ds(CONV_PAD, q), :] = xbc_ref[...]

    acc = cb_ref[...] + xpad_ref[pl.ds(CONV_PAD - 3, q), :] * cw_ref[0:1, :]
    acc = acc + xpad_ref[pl.ds(CONV_PAD - 2, q), :] * cw_ref[1:2, :]
    acc = acc + xpad_ref[pl.ds(CONV_PAD - 1, q), :] * cw_ref[2:3, :]
    acc = acc + xpad_ref[pl.ds(CONV_PAD, q), :] * cw_ref[3:4, :]
    xc = acc * (1.0 / (1.0 + jnp.exp(-acc)))
    new_tail = xpad_ref[pl.ds(CONV_PAD + real_len - (CONV_WIDTH - 1), CONV_WIDTH - 1), :]

    @pl.when(c == nc - 1)
    def _():
        cso_ref[...] = new_tail

    xpad_ref[pl.ds(CONV_PAD - (CONV_WIDTH - 1), CONV_WIDTH - 1), :] = new_tail

    xs = xc[:, :SSD_WIDTH]
    gn = SSD_GROUPS * SSD_STATE

    dtx = dt_ref[...] + dtb_ref[...]
    dt = jnp.maximum(dtx, 0.0) + jnp.log1p(jnp.exp(-jnp.abs(dtx)))
    if real_len < q:
        row = lax.broadcasted_iota(jnp.int32, (q, LANES), 0)
        dt = jnp.where(row < real_len, dt, 0.0)
    a = -jnp.exp(alog_ref[...])
    da = dt * a
    ii = lax.broadcasted_iota(jnp.int32, (q, q), 0)
    jj = lax.broadcasted_iota(jnp.int32, (q, q), 1)
    causal = ii >= jj
    tril = jnp.where(causal, 1.0, 0.0).astype(F32)
    acum = jnp.dot(tril, da, preferred_element_type=F32, precision=HI)
    acum_t = acum.T
    dt_t = dt.T
    alast = acum[q - 1:q, :]
    ea = jnp.exp(acum)
    wdec = jnp.exp(alast - acum) * dt
    alast8 = jnp.broadcast_to(alast, (8, LANES))
    dec_e = jnp.exp(jnp.dot(alast8, exp_ref[...], preferred_element_type=F32, precision=HI)[0:1, :])

    lane = lax.broadcasted_iota(jnp.int32, (q, LANES), 1)
    first = lane < SSD_HEAD_DIM
    y_parts = []
    xw_parts = []
    for g in range(SSD_GROUPS):
        bg = xc[:, SSD_WIDTH + g * SSD_STATE:SSD_WIDTH + (g + 1) * SSD_STATE]
        cg = xc[:, SSD_WIDTH + gn + g * SSD_STATE:SSD_WIDTH + gn + (g + 1) * SSD_STATE]
        bg_t = bg.T
        scores = _dot(cg, bg_t)
        per_group = SSD_HEADS // SSD_GROUPS
        for pp in range(per_group // 2):
            p = g * (per_group // 2) + pp
            x_pair = xs[:, p * LANES:(p + 1) * LANES]
            st_pair = st_ref[:, p * LANES:(p + 1) * LANES]
            rhs = jnp.concatenate([x_pair, st_pair], axis=0).astype(BF16)
            outs = []
            for k in range(2):
                h = 2 * p + k
                a_col = acum[:, h:h + 1]
                a_row = acum_t[h:h + 1, :]
                seg = jnp.where(causal, a_col - a_row, 0.0)
                m_h = jnp.where(causal, jnp.exp(seg), 0.0) * scores * dt_t[h:h + 1, :]
                c_h = cg * ea[:, h:h + 1]
                lhs = jnp.concatenate([m_h, c_h], axis=1).astype(BF16)
                outs.append(jnp.dot(lhs, rhs, preferred_element_type=F32))
            y_parts.append(jnp.where(first, outs[0], outs[1]))
            w_pair = jnp.where(first, wdec[:, 2 * p:2 * p + 1], wdec[:, 2 * p + 1:2 * p + 2])
            xw_parts.append(x_pair * w_pair)
        half = SSD_WIDTH // SSD_GROUPS
        xw_g = jnp.concatenate(xw_parts[-(per_group // 2):], axis=1)
        upd = _dot(bg_t, xw_g)
        st_ref[:, g * half:(g + 1) * half] = (
            st_ref[:, g * half:(g + 1) * half] * dec_e[:, g * half:(g + 1) * half] + upd)

    y = jnp.concatenate(y_parts, axis=1) + xs * dsk_ref[...]
    zz = z_ref[...]
    y = y * (zz * (1.0 / (1.0 + jnp.exp(-zz))))
    y_ref[...] = _rms(y, nw_ref[...])

    @pl.when(c == nc - 1)
    def _():
        for p in range(n_pairs):
            so_ref[pl.ds(p * LANES, LANES), :] = st_ref[:, p * LANES:(p + 1) * LANES].T


def _ssd(xbc, z, dt, conv_state, ssm_state, conv_w, conv_b, dt_bias, a_log, d_skip, norm_w, bsz, seq, layer=0):
    has_init = conv_state is not None
    real_len = min(seq, SSD_CHUNK)
    nc = max(seq // SSD_CHUNK, 1)
    if not has_init:
        conv_state = jnp.zeros((1, bsz, CONV_WIDTH - 1, CONV_CH), F32)
        ssm_state = jnp.zeros((1, bsz, SSD_HEADS, SSD_HEAD_DIM, SSD_STATE), F32)
    s0 = ssm_state.reshape(ssm_state.shape[0], bsz, SSD_WIDTH, SSD_STATE)
    pad_h = LANES - SSD_HEADS
    dtb = jnp.pad(dt_bias, (0, pad_h)).reshape(1, LANES)
    alog = jnp.pad(a_log, (0, pad_h)).reshape(1, LANES)
    dsk = jnp.repeat(d_skip, SSD_HEAD_DIM).reshape(1, SSD_WIDTH)
    expand = (np.arange(LANES)[:, None] == (np.arange(SSD_WIDTH) // SSD_HEAD_DIM)[None, :]).astype(np.float32)
    xbc3 = xbc.reshape(bsz, seq, CONV_CH)
    z3 = z.reshape(bsz, seq, SSD_WIDTH)
    dt3 = dt.reshape(bsz, seq, LANES)
    seq_p = nc * SSD_CHUNK
    if seq_p != seq:
        pad = ((0, 0), (0, seq_p - seq), (0, 0))
        xbc3, z3, dt3 = jnp.pad(xbc3, pad), jnp.pad(z3, pad), jnp.pad(dt3, pad)

    def tok(width):
        return pl.BlockSpec((None, SSD_CHUNK, width), lambda b, c: (b, c, 0))

    y, cso, so = pl.pallas_call(
        functools.partial(_ssd_body, real_len, has_init),
        grid=(bsz, nc),
        in_specs=[tok(CONV_CH), tok(SSD_WIDTH), tok(LANES),
                  pl.BlockSpec((None, None, CONV_WIDTH - 1, CONV_CH), lambda b, c: (layer, b, 0, 0)),
                  pl.BlockSpec((None, None, SSD_WIDTH, SSD_STATE), lambda b, c: (layer, b, 0, 0)),
                  _full((CONV_WIDTH, CONV_CH)), _full((1, CONV_CH)), _full((1, LANES)), _full((1, LANES)),
                  _full((1, SSD_WIDTH)), _full((1, SSD_WIDTH)), _full((LANES, SSD_WIDTH))],
        out_specs=[tok(SSD_WIDTH),
                   pl.BlockSpec((None, CONV_WIDTH - 1, CONV_CH), lambda b, c: (b, 0, 0)),
                   pl.BlockSpec((None, SSD_WIDTH, SSD_STATE), lambda b, c: (b, 0, 0))],
        out_shape=[jax.ShapeDtypeStruct((bsz, seq_p, SSD_WIDTH), F32),
                   jax.ShapeDtypeStruct((bsz, CONV_WIDTH - 1, CONV_CH), F32),
                   jax.ShapeDtypeStruct((bsz, SSD_WIDTH, SSD_STATE), F32)],
        scratch_shapes=[pltpu.VMEM((CONV_PAD + SSD_CHUNK, CONV_CH), F32),
                        pltpu.VMEM((SSD_STATE, SSD_WIDTH), F32)],
        compiler_params=_params("parallel", "arbitrary"),
        name="ssd_scan",
    )(xbc3, z3, dt3, conv_state, s0, conv_w, conv_b.reshape(1, CONV_CH), dtb, alog, dsk,
      norm_w.reshape(1, SSD_WIDTH), jnp.asarray(expand))
    return (y[:, :seq].reshape(bsz * seq, SSD_WIDTH), cso,
            so.reshape(bsz, SSD_HEADS, SSD_HEAD_DIM, SSD_STATE))


def _t5_bucket_np(rel):
    n = np.maximum(rel, 0)
    max_exact = REL_BUCKETS // 2
    nf = np.maximum(n, 1).astype(np.float32)
    large = max_exact + (np.log(nf / np.float32(max_exact)) / np.float32(math.log(REL_MAX_DIST / max_exact))
                         * np.float32(REL_BUCKETS - max_exact)).astype(np.int32)
    large = np.minimum(large, REL_BUCKETS - 1)
    return np.where(n < max_exact, n, large).astype(np.int32)


def _bias_tiles_body(bk_ref, rb_ref, o_ref):
    h = pl.program_id(0)
    ii = lax.broadcasted_iota(jnp.int32, (MOBA_BLOCK, MOBA_BLOCK), 0)
    jj = lax.broadcasted_iota(jnp.int32, (MOBA_BLOCK, MOBA_BLOCK), 1)
    for t in range(2):
        bk = bk_ref[t]
        acc = jnp.zeros((MOBA_BLOCK, MOBA_BLOCK), F32)
        for u in range(REL_BUCKETS):
            acc = jnp.where(bk == u, rb_ref[u, h], acc)
        if t == 0:
            acc = jnp.where(jj >= ii, acc, NEG)
        o_ref[t] = acc


def _bias_tiles(rel_bias):
    i = np.arange(MOBA_BLOCK)[None, :]
    j = np.arange(MOBA_BLOCK)[:, None]
    buckets = np.stack([_t5_bucket_np(i - j), _t5_bucket_np(MOBA_BLOCK + i - j)])
    return pl.pallas_call(
        _bias_tiles_body,
        grid=(ATT_HEADS,),
        in_specs=[_full((2, MOBA_BLOCK, MOBA_BLOCK)),
                  pl.BlockSpec(memory_space=pltpu.SMEM)],
        out_specs=pl.BlockSpec((None, 2, MOBA_BLOCK, MOBA_BLOCK), lambda h: (h, 0, 0, 0)),
        out_shape=jax.ShapeDtypeStruct((ATT_HEADS, 2, MOBA_BLOCK, MOBA_BLOCK), F32),
        compiler_params=_params("parallel"),
        name="bias_tiles",
    )(jnp.asarray(buckets), rel_bias)


def _block_rank(g, blk):
    rank = jnp.zeros(g.shape, jnp.int32)
    for m in range(g.shape[1]):
        gm = g[:, m:m + 1]
        rank = rank + ((gm > g) | ((gm == g) & (m < blk))).astype(jnp.int32)
    return rank


def _block_rank_t(g, blk):
    rank = jnp.zeros(g.shape, jnp.int32)
    for m in range(g.shape[0]):
        gm = g[m:m + 1, :]
        rank = rank + ((gm > g) | ((gm == g) & (m < blk))).astype(jnp.int32)
    return rank


def _moba_prompt_body(nblk, q_ref, k_ref, v_ref, bias_ref, cfar_ref, o_ref, km_ref, k16_ref, vt_ref, qt_ref):
    pair = pl.program_id(0)
    blk_rows = MOBA_BLOCK
    hd = ATT_HEAD_DIM
    scale = hd ** -0.5
    per_blk = blk_rows // LANES
    for n in range(nblk):
        tot = None
        for c in range(n * per_blk, (n + 1) * per_blk):
            kb = k_ref[:, c * LANES:(c + 1) * LANES].T
            k16_ref[pl.ds(c * LANES, LANES), :] = kb.astype(BF16)
            part = jnp.sum(kb, axis=0, keepdims=True)
            tot = part if tot is None else tot + part
        km_ref[n:n + 1, :] = tot * (1.0 / blk_rows)
    vt_ref[...] = v_ref[...].astype(BF16)
    for c in range(nblk * per_blk):
        qt_ref[:, c * LANES:(c + 1) * LANES] = (q_ref[pl.ds(c * LANES, LANES), :] * scale).T.astype(BF16)
    kmean = km_ref[...].astype(BF16)
    row = lax.broadcasted_iota(jnp.int32, (LANES, blk_rows), 0)
    blk = lax.broadcasted_iota(jnp.int32, (nblk, blk_rows), 0)
    zero16 = jnp.zeros((LANES, blk_rows), BF16)

    for i in range(nblk):
        qt = qt_ref[:, i * blk_rows:(i + 1) * blk_rows]
        n_keys = (i + 1) * blk_rows
        halves = []
        for kk in range(2):
            qth = jnp.where(row < hd, qt, zero16) if kk == 0 else jnp.where(row >= hd, qt, zero16)
            gate = jnp.dot(kmean, qth, preferred_element_type=F32)
            past = blk < i
            gate = jnp.where(past, gate, NEG)
            sel = past & (_block_rank_t(gate, blk) < MOBA_TOPK)
            pen = jnp.where(sel, cfar_ref[2 * pair + kk], NEG)
            pen_prev = jnp.where(sel[max(i - 1, 0):max(i - 1, 0) + 1, :], 0.0, NEG)

            def logits(n):
                seg = jnp.dot(k16_ref[n * blk_rows:(n + 1) * blk_rows, :], qth, preferred_element_type=F32)
                if n == i:
                    return seg + bias_ref[kk, 0]
                if n == i - 1:
                    return seg + bias_ref[kk, 1] + pen_prev
                return seg + pen[n:n + 1, :]

            m = logits(0)
            for n in range(1, i + 1):
                m = jnp.maximum(m, logits(n))
            m = jnp.max(m, axis=0, keepdims=True)
            tot = None
            acc = None
            for n in range(i + 1):
                p = jnp.exp(logits(n) - m)
                tot = p if tot is None else tot + p
                pv = jnp.dot(vt_ref[kk * hd:(kk + 1) * hd, n * blk_rows:(n + 1) * blk_rows], p.astype(BF16),
                             preferred_element_type=F32)
                acc = pv if acc is None else acc + pv
            halves.append(acc / jnp.sum(tot, axis=0, keepdims=True))
        o_ref[pl.ds(i * blk_rows, blk_rows), :] = jnp.concatenate(halves, axis=0).T


def _moba_prompt(q, k, v, bias_tiles, cfar, bsz, seq, layer):
    nblk = seq // MOBA_BLOCK
    n_pairs = ATT_HEADS // 2
    tok = pl.BlockSpec((seq, LANES), lambda p, b: (b, p))
    tok_l = pl.BlockSpec((None, None, LANES, seq), lambda p, b: (layer, b, p, 0))
    return pl.pallas_call(
        functools.partial(_moba_prompt_body, nblk),
        grid=(n_pairs, bsz),
        in_specs=[tok, tok_l, tok_l,
                  pl.BlockSpec((2, 2, MOBA_BLOCK, MOBA_BLOCK), lambda p, b: (p, 0, 0, 0)),
                  pl.BlockSpec(memory_space=pltpu.SMEM)],
        out_specs=tok,
        out_shape=jax.ShapeDtypeStruct((bsz * seq, ATT_WIDTH), F32),
        scratch_shapes=[pltpu.VMEM((nblk, LANES), F32), pltpu.VMEM((seq, LANES), BF16),
                        pltpu.VMEM((LANES, seq), BF16), pltpu.VMEM((LANES, seq), BF16)],
        compiler_params=_params("parallel", "parallel"),
        name="moba_prompt",
    )(q, k, v, bias_tiles, cfar)


PAGES_PER_STEP = 16
ROWS8 = 8


def _sample_moba_body(layer, n_new, n_k, page_size, pt_ref, q_ref, kn_ref, vn_ref, tprev_ref, town_ref, cfar_ref,
                      ck_ref, cv_ref, o_ref, buf_ref, sem_ref, s_ref, acc_ref, l_ref):
    b = pl.program_id(0)
    n_seq = pl.num_programs(0)
    n_chunks = 2 * n_k
    n_tok = s_ref.shape[2]
    n_blk = n_tok // MOBA_BLOCK
    q = q_ref[...]
    qh = [q[:, h * ATT_HEAD_DIM:(h + 1) * ATT_HEAD_DIM] for h in range(ATT_HEADS)]
    scale = ATT_HEAD_DIM ** -0.5

    def chunk_copies(seq, g):
        src = ck_ref if g < n_k else cv_ref
        slot = g % 2
        return [pltpu.make_async_copy(src.at[layer, pt_ref[seq, (g % n_k) * PAGES_PER_STEP + r]],
                                      buf_ref.at[slot, r], sem_ref.at[slot])
                for r in range(PAGES_PER_STEP)]

    def start_chunk(seq, g):
        for r, c in enumerate(chunk_copies(seq, g)):
            c.start(priority=r % 2)

    def advance(g):
        for c in chunk_copies(b, g):
            c.wait()
        if g + 1 < n_chunks:
            start_chunk(b, g + 1)
        else:
            @pl.when(b + 1 < n_seq)
            def _():
                start_chunk(b + 1, 0)

    @pl.when(b == 0)
    def _():
        start_chunk(b, 0)

    for g in range(n_k):
        advance(g)
        for r in range(PAGES_PER_STEP):
            start = (g * PAGES_PER_STEP + r) * page_size
            for h in range(ATT_HEADS):
                s_ref[h, :, start:start + page_size] = _dot(qh[h], buf_ref[g % 2, r, h])

    def softmax_over_selected():
        blk = lax.broadcasted_iota(jnp.int32, (ROWS8, n_blk), 1)
        col = lax.broadcasted_iota(jnp.int32, (ROWS8, LANES), 1)
        halves_per_blk = MOBA_BLOCK // LANES
        for h in range(ATT_HEADS):
            s_all = s_ref[h]
            chunks = [s_all[:, c * LANES:(c + 1) * LANES] for c in range(n_tok // LANES)]
            gate = jnp.zeros((ROWS8, n_blk), F32)
            for n in range(n_blk):
                part = chunks[n * halves_per_blk]
                for c in chunks[n * halves_per_blk + 1:(n + 1) * halves_per_blk]:
                    part = part + c
                gate = jnp.where(blk == n, jnp.sum(part, axis=1, keepdims=True) * (1.0 / MOBA_BLOCK), gate)
            sel = _block_rank(gate, blk) < MOBA_TOPK
            pen = jnp.where(sel, cfar_ref[h], NEG)
            pen_last = jnp.where(sel[:, n_blk - 1:n_blk], 0.0, NEG)
            logits = []
            for c, x in enumerate(chunks):
                n = c // halves_per_blk
                if n == n_blk - 1:
                    off = (c % halves_per_blk) * LANES
                    logits.append(x * scale + tprev_ref[h, :, off:off + LANES] + pen_last)
                else:
                    logits.append(x * scale + pen[:, n:n + 1])
            kn_h = kn_ref[:, h * ATT_HEAD_DIM:(h + 1) * ATT_HEAD_DIM]
            vn_h = vn_ref[:, h * ATT_HEAD_DIM:(h + 1) * ATT_HEAD_DIM]
            s_own = jnp.where(col < n_new, _dot_t(qh[h] * scale, kn_h) + town_ref[h], NEG)
            m_el = s_own
            for x in logits:
                m_el = jnp.maximum(m_el, x)
            m = jnp.max(m_el, axis=1, keepdims=True)
            p_own = jnp.exp(s_own - m)
            tot = p_own
            for c, x in enumerate(logits):
                p = jnp.exp(x - m)
                s_ref[h, :, c * LANES:(c + 1) * LANES] = p
                tot = tot + p
            acc_ref[h] = _dot(p_own, vn_h)
            l_ref[h] = jnp.broadcast_to(jnp.sum(tot, axis=1, keepdims=True), (ROWS8, LANES))

    softmax_over_selected()

    for g in range(n_k, n_chunks):
        advance(g)
        for h in range(ATT_HEADS):
            acc = acc_ref[h]
            for r in range(PAGES_PER_STEP):
                start = ((g - n_k) * PAGES_PER_STEP + r) * page_size
                acc = acc + _dot_t(s_ref[h, :, start:start + page_size], buf_ref[g % 2, r, h])
            acc_ref[h] = acc

    o_ref[...] = jnp.concatenate(
        [acc_ref[h] / l_ref[h][:, :ATT_HEAD_DIM] for h in range(ATT_HEADS)], axis=1)


def _sample_moba(layer, q8, kn, vn, page_table, cache_kt, cache_vt, bias_tiles, cfar, n_new):
    bsz = q8.shape[0]
    page_size = cache_kt.shape[4]
    n_pages = page_table.shape[1]
    n_k = n_pages // PAGES_PER_STEP
    tprev = jnp.transpose(bias_tiles[:, 1, :, :ROWS8], (0, 2, 1))
    town = jnp.transpose(bias_tiles[:, 0, :LANES, :ROWS8], (0, 2, 1))
    grid_spec = pltpu.PrefetchScalarGridSpec(
        num_scalar_prefetch=1,
        grid=(bsz,),
        in_specs=[pl.BlockSpec((None, ROWS8, ATT_WIDTH), lambda b, pt: (b, 0, 0)),
                  pl.BlockSpec((None, LANES, ATT_WIDTH), lambda b, pt: (b, 0, 0)),
                  pl.BlockSpec((None, LANES, ATT_WIDTH), lambda b, pt: (b, 0, 0)),
                  pl.BlockSpec((ATT_HEADS, ROWS8, MOBA_BLOCK), lambda b, pt: (0, 0, 0)),
                  pl.BlockSpec((ATT_HEADS, ROWS8, LANES), lambda b, pt: (0, 0, 0)),
                  pl.BlockSpec(memory_space=pltpu.SMEM),
                  pl.BlockSpec(memory_space=pl.ANY),
                  pl.BlockSpec(memory_space=pl.ANY)],
        out_specs=pl.BlockSpec((None, ROWS8, ATT_WIDTH), lambda b, pt: (b, 0, 0)),
        scratch_shapes=[pltpu.VMEM((2, PAGES_PER_STEP, ATT_HEADS, ATT_HEAD_DIM, page_size), F32),
                        pltpu.SemaphoreType.DMA((2,)),
                        pltpu.VMEM((ATT_HEADS, ROWS8, n_pages * page_size), F32),
                        pltpu.VMEM((ATT_HEADS, ROWS8, ATT_HEAD_DIM), F32),
                        pltpu.VMEM((ATT_HEADS, ROWS8, LANES), F32)],
    )
    return pl.pallas_call(
        functools.partial(_sample_moba_body, layer, n_new, n_k, page_size),
        grid_spec=grid_spec,
        out_shape=jax.ShapeDtypeStruct((bsz, ROWS8, ATT_WIDTH), F32),
        compiler_params=_params("arbitrary"),
        name="sample_moba",
    )(page_table, q8, kn, vn, tprev, town, cfar, cache_kt, cache_vt)


def _mix_ca_body(rows_per_batch, h_ref, ys_ref, ya_ref, mk_ref, mv_ref, wo1_ref, wo2_ref, g_ref,
                 wq_ref, wco_ref, o_ref):
    tm = h_ref.shape[0]
    h1 = (h_ref[...] + jnp.dot(ys_ref[...].astype(BF16), wo1_ref[...], preferred_element_type=F32)
          + jnp.dot(ya_ref[...].astype(BF16), wo2_ref[...], preferred_element_type=F32))
    hn = _rms(h1, g_ref[...]).astype(BF16)
    qq = jnp.dot(hn, wq_ref[...], preferred_element_type=F32) * (CA_HEAD_DIM ** -0.5)
    nkeys = mk_ref.shape[0] * mk_ref.shape[1]
    mk = mk_ref[...].reshape(nkeys, CA_WIDTH)
    mv = mv_ref[...].reshape(nkeys, CA_WIDTH)
    if rows_per_batch < tm:
        r = lax.broadcasted_iota(jnp.int32, (tm, nkeys), 0) // rows_per_batch
        c = lax.broadcasted_iota(jnp.int32, (tm, nkeys), 1) // MEM_LEN
        same = r == c
    outs = []
    for hd in range(CA_HEADS):
        sl = slice(hd * CA_HEAD_DIM, (hd + 1) * CA_HEAD_DIM)
        s = _dot_t(qq[:, sl], mk[:, sl])
        if rows_per_batch < tm:
            s = jnp.where(same, s, NEG)
        s = s - jnp.max(s, axis=1, keepdims=True)
        p = jnp.exp(s)
        p = p / jnp.sum(p, axis=1, keepdims=True)
        outs.append(_dot(p, mv[:, sl]))
    o = jnp.concatenate(outs, axis=1).astype(BF16)
    o_ref[...] = h1 + jnp.dot(o, wco_ref[...], preferred_element_type=F32)


def _mix_ca(h, ys, ya, mk, mv, wo1, wo2, g_ca, wq, wco, rows_per_batch, tm):
    m = h.shape[0]
    nbat = max(tm // rows_per_batch, 1)
    per = rows_per_batch // tm if rows_per_batch >= tm else 1

    def mem_map(i):
        return (i // per, 0, 0)

    def row(width):
        return pl.BlockSpec((tm, width), lambda i: (i, 0))

    return pl.pallas_call(
        functools.partial(_mix_ca_body, rows_per_batch),
        grid=(m // tm,),
        in_specs=[row(D_MODEL), row(SSD_WIDTH), row(ATT_WIDTH),
                  pl.BlockSpec((nbat, MEM_LEN, CA_WIDTH), mem_map),
                  pl.BlockSpec((nbat, MEM_LEN, CA_WIDTH), mem_map),
                  _full(wo1.shape), _full(wo2.shape), _full((1, D_MODEL)), _full(wq.shape), _full(wco.shape)],
        out_specs=row(D_MODEL),
        out_shape=jax.ShapeDtypeStruct((m, D_MODEL), F32),
        compiler_params=_params("parallel"),
        name="mix_ca",
    )(h, ys, ya, mk, mv, wo1, wo2, g_ca.reshape(1, D_MODEL), wq, wco)


FF_CHUNK = 1024


def _mlp_body(final, h_ref, g_ref, wu_ref, wd_ref, gf_ref, o_ref):
    h = h_ref[...]
    xn = _rms(h, g_ref[...]).astype(BF16)
    acc = h
    for c in range(FF // FF_CHUNK):
        u = jnp.dot(xn, wu_ref[:, c * FF_CHUNK:(c + 1) * FF_CHUNK], preferred_element_type=F32)
        u = jnp.maximum(u, 0.0)
        acc = acc + jnp.dot((u * u).astype(BF16), wd_ref[c * FF_CHUNK:(c + 1) * FF_CHUNK, :],
                            preferred_element_type=F32)
    o_ref[...] = _rms(acc, gf_ref[...]) if final else acc


def _mlp(h, g, wu, wd, g_final, final, tm):
    m = h.shape[0]
    row = pl.BlockSpec((tm, D_MODEL), lambda i: (i, 0))
    return pl.pallas_call(
        functools.partial(_mlp_body, final),
        grid=(m // tm,),
        in_specs=[row, _full((1, D_MODEL)), _full(wu.shape), _full(wd.shape), _full((1, D_MODEL))],
        out_specs=row,
        out_shape=jax.ShapeDtypeStruct((m, D_MODEL), F32),
        compiler_params=_params("parallel"),
        name="mlp",
    )(h, g.reshape(1, D_MODEL), wu, wd, g_final.reshape(1, D_MODEL))


def _layer_weights(l, w_in, w_out, w_ca_q, w_ca_kv, w_ca_o, w_up, w_down):
    w = w_in[l]
    o = 0
    parts = []
    for width in (SSD_WIDTH, CONV_CH, SSD_HEADS, ATT_WIDTH, ATT_WIDTH, ATT_WIDTH):
        parts.append(w[:, o:o + width])
        o += width
    parts[2] = jnp.pad(parts[2], ((0, 0), (0, LANES - SSD_HEADS)))
    wo = w_out[l]
    w_in = [p.astype(BF16) for p in parts]
    return dict(
        w_in=w_in,
        w_in_prompt=w_in[:4] + [w_in[4].T, w_in[5].T],
        wo1=wo[:SSD_WIDTH].astype(BF16), wo2=wo[SSD_WIDTH:].astype(BF16),
        wq=w_ca_q[l].astype(BF16), wco=w_ca_o[l].astype(BF16),
        wkv=[w_ca_kv[l][:, :CA_WIDTH].astype(BF16), w_ca_kv[l][:, CA_WIDTH:].astype(BF16)],
        wu=w_up[l].astype(BF16), wd=w_down[l].astype(BF16))


def kernel(x_prompt, x_sample, cache_k, cache_v, cache_mem_k, cache_mem_v, state_conv, state_ssm,
           page_table, mem_prompt, norm_mix, w_in, conv_w, conv_b, dt_bias, a_log, d_skip, ssd_norm,
           rel_bias, w_out, norm_ca, norm_mem, w_ca_q, w_ca_kv, w_ca_o, norm_mlp, w_up, w_down, norm_final):
    depth = w_in.shape[0]
    pb, seq, _ = x_prompt.shape
    sb, dseq, _ = x_sample.shape
    page_size = cache_k.shape[2]
    tm = 512

    bias_tiles = _bias_tiles(rel_bias)
    cfar = rel_bias[REL_BUCKETS - 1]
    cache_kt = jnp.transpose(cache_k, (0, 1, 3, 4, 2))
    cache_vt = jnp.transpose(cache_v, (0, 1, 3, 4, 2))
    mem2d = mem_prompt.reshape(pb * MEM_LEN, D_MODEL)

    def pad_rows(a, rows):
        return jnp.pad(a.reshape(sb, dseq, ATT_WIDTH), ((0, 0), (0, rows - dseq), (0, 0)))

    h = x_prompt.reshape(pb * seq, D_MODEL)
    g = x_sample.reshape(sb * dseq, D_MODEL)
    outs = {n: [] for n in ("pmk", "pmv", "pconv", "pssm", "sk", "sv", "sconv", "sssm")}
    pk = jnp.zeros((depth, pb, ATT_WIDTH, seq), F32)
    pv = jnp.zeros((depth, pb, ATT_WIDTH, seq), F32)
    for l in range(depth):
        lw = _layer_weights(l, w_in, w_out, w_ca_q, w_ca_kv, w_ca_o, w_up, w_down)
        last = l == depth - 1

        z, xbc, dt, q, pk, pv = _norm_proj(h, norm_mix[l], lw["w_in_prompt"], tm, transposed=(4, 5), layer=l,
                                           depth=depth, seq=seq, prev=[pk, pv])
        y_ssd, c_new, s_new = _ssd(xbc, z, dt, None, None, conv_w[l], conv_b[l], dt_bias[l], a_log[l],
                                   d_skip[l], ssd_norm[l], pb, seq)
        y_att = _moba_prompt(q, pk, pv, bias_tiles, cfar, pb, seq, l)
        mk, mv = _norm_proj(mem2d, norm_mem[l], lw["wkv"], min(tm, pb * MEM_LEN))
        h = _mix_ca(h, y_ssd, y_att, mk.reshape(pb, MEM_LEN, CA_WIDTH), mv.reshape(pb, MEM_LEN, CA_WIDTH),
                    lw["wo1"], lw["wo2"], norm_ca[l], lw["wq"], lw["wco"], seq, tm)
        h = _mlp(h, norm_mlp[l], lw["wu"], lw["wd"], norm_final, last, tm)
        outs["pmk"].append(mk.reshape(pb, MEM_LEN, CA_HEADS, CA_HEAD_DIM))
        outs["pmv"].append(mv.reshape(pb, MEM_LEN, CA_HEADS, CA_HEAD_DIM))
        outs["pconv"].append(c_new)
        outs["pssm"].append(s_new)

        ms = sb * dseq
        z, xbc, dt, q, k, v = _norm_proj(g, norm_mix[l], lw["w_in"], ms)
        y_ssd, c_new, s_new = _ssd(xbc, z, dt, state_conv, state_ssm, conv_w[l], conv_b[l], dt_bias[l],
                                   a_log[l], d_skip[l], ssd_norm[l], sb, dseq, layer=l)
        y_att = _sample_moba(l, pad_rows(q, ROWS8), pad_rows(k, LANES), pad_rows(v, LANES), page_table,
                             cache_kt, cache_vt, bias_tiles, cfar, dseq)[:, :dseq].reshape(ms, ATT_WIDTH)
        g = _mix_ca(g, y_ssd, y_att, cache_mem_k[l].reshape(sb, MEM_LEN, CA_WIDTH),
                    cache_mem_v[l].reshape(sb, MEM_LEN, CA_WIDTH), lw["wo1"], lw["wo2"], norm_ca[l],
                    lw["wq"], lw["wco"], dseq, min(32, ms))
        g = _mlp(g, norm_mlp[l], lw["wu"], lw["wd"], norm_final, last, ms)
        outs["sk"].append(k.reshape(sb, dseq, ATT_HEADS, ATT_HEAD_DIM))
        outs["sv"].append(v.reshape(sb, dseq, ATT_HEADS, ATT_HEAD_DIM))
        outs["sconv"].append(c_new)
        outs["sssm"].append(s_new)

    st = {n: jnp.stack(vs) for n, vs in outs.items() if vs}
    def token_major(a):
        return jnp.transpose(a.reshape(depth, pb, ATT_HEADS, ATT_HEAD_DIM, seq), (0, 1, 4, 2, 3))

    return (h.reshape(pb, seq, D_MODEL), g.reshape(sb, dseq, D_MODEL), token_major(pk), token_major(pv),
            st["pmk"], st["pmv"],
            st["pconv"], st["pssm"], st["sk"], st["sv"], st["sconv"], st["sssm"])
```

```python
import functools
import math

import numpy as np
import jax
import jax.numpy as jnp
from jax import lax
from jax.experimental import pallas as pl
from jax.experimental.pallas import tpu as pltpu

D_MODEL = 1024
SSD_HEADS = 16
SSD_HEAD_DIM = 64
SSD_WIDTH = SSD_HEADS * SSD_HEAD_DIM
SSD_GROUPS = 2
SSD_STATE = 128
SSD_CHUNK = 128
CONV_WIDTH = 4
CONV_CH = SSD_WIDTH + 2 * SSD_GROUPS * SSD_STATE
ATT_HEADS = 8
ATT_HEAD_DIM = 64
ATT_WIDTH = ATT_HEADS * ATT_HEAD_DIM
MOBA_BLOCK = 256
MOBA_TOPK = 3
REL_BUCKETS = 32
REL_MAX_DIST = 128
MEM_LEN = 256
CA_HEADS = 4
CA_HEAD_DIM = 128
CA_WIDTH = CA_HEADS * CA_HEAD_DIM
FF = 4 * D_MODEL
EPS = 1e-5

LANES = 128
VMEM_LIMIT = 56 * 1024 * 1024
NEG = -1e30

BF16 = jnp.bfloat16
F32 = jnp.float32
HI = lax.Precision.HIGHEST


def _params(*sem):
    return pltpu.CompilerParams(dimension_semantics=sem, vmem_limit_bytes=VMEM_LIMIT)


def _rms(x, gain):
    return x * lax.rsqrt(jnp.mean(x * x, axis=-1, keepdims=True) + EPS) * gain


def _dot(a, b):
    return jnp.dot(a.astype(BF16), b.astype(BF16), preferred_element_type=F32)


def _dot_t(a, b):
    return lax.dot_general(a.astype(BF16), b.astype(BF16), (((1,), (1,)), ((), ())),
                           preferred_element_type=F32)


def _full(shape):
    return pl.BlockSpec(shape, lambda *_: (0,) * len(shape))


def _norm_proj_body(n_out, transposed, x_ref, g_ref, *refs):
    w_refs, o_refs = refs[:n_out], refs[len(refs) - n_out:]
    xn = _rms(x_ref[...], g_ref[...]).astype(BF16)
    for idx, (w_ref, o_ref) in enumerate(zip(w_refs, o_refs)):
        if idx in transposed:
            o_ref[...] = lax.dot_general(w_ref[...], xn, (((1,), (1,)), ((), ())), preferred_element_type=F32)
        else:
            o_ref[...] = jnp.dot(xn, w_ref[...], preferred_element_type=F32)


def _norm_proj(x, gain, weights, tm, transposed=(), layer=0, depth=1, seq=None, prev=None):
    m, d = x.shape
    n_out = len(weights)
    out_specs, out_shape = [], []
    for idx, w in enumerate(weights):
        if idx in transposed:
            n = w.shape[0]
            per = seq // tm
            out_specs.append(pl.BlockSpec((None, None, n, tm), lambda i: (layer, i // per, 0, i % per)))
            out_shape.append(jax.ShapeDtypeStruct((depth, m // seq, n, seq), F32))
        else:
            n = w.shape[1]
            out_specs.append(pl.BlockSpec((tm, n), lambda i: (i, 0)))
            out_shape.append(jax.ShapeDtypeStruct((m, n), F32))
    prev = list(prev) if prev is not None else []
    n_in = 2 + n_out
    aliases = {n_in + k: idx for k, idx in enumerate(transposed)} if prev else {}
    return pl.pallas_call(
        functools.partial(_norm_proj_body, n_out, tuple(transposed)),
        grid=(m // tm,),
        in_specs=[pl.BlockSpec((tm, d), lambda i: (i, 0)), _full((1, d))]
        + [_full(w.shape) for w in weights] + [pl.BlockSpec(memory_space=pl.ANY) for _ in prev],
        out_specs=out_specs,
        out_shape=out_shape,
        input_output_aliases=aliases,
        compiler_params=_params("parallel"),
        name="norm_proj",
    )(x, gain.reshape(1, d), *weights, *prev)


CONV_PAD = 8


def _ssd_body(real_len, has_init, xbc_ref, z_ref, dt_ref, cs_ref, s0_ref, cw_ref, cb_ref, dtb_ref,
              alog_ref, dsk_ref, nw_ref, exp_ref, y_ref, cso_ref, so_ref, xpad_ref, st_ref):
    c = pl.program_id(1)
    nc = pl.num_programs(1)
    q = SSD_CHUNK
    n_pairs = SSD_HEADS // 2

    @pl.when(c == 0)
    def _():
        if has_init:
            xpad_ref[pl.ds(0, CONV_PAD), :] = jnp.zeros((CONV_PAD, CONV_CH), F32)
            xpad_ref[pl.ds(CONV_PAD - (CONV_WIDTH - 1), CONV_WIDTH - 1), :] = cs_ref[...]
            for p in range(n_pairs):
                st_ref[:, p * LANES:(p + 1) * LANES] = s0_ref[pl.ds(p * LANES, LANES), :].T
        else:
            xpad_ref[pl.ds(0, CONV_PAD), :] = jnp.zeros((CONV_PAD, CONV_CH), F32)
            st_ref[...] = jnp.zeros_like(st_ref)

    xpad_ref[pl.ds(CONV_PAD, q), :] = xbc_ref[...]

    acc = cb_ref[...] + xpad_ref[pl.ds(CONV_PAD - 3, q), :] * cw_ref[0:1, :]
    acc = acc + xpad_ref[pl.ds(CONV_PAD - 2, q), :] * cw_ref[1:2, :]
    acc = acc + xpad_ref[pl.ds(CONV_PAD - 1, q), :] * cw_ref[2:3, :]
    acc = acc + xpad_ref[pl.ds(CONV_PAD, q), :] * cw_ref[3:4, :]
    xc = acc * (1.0 / (1.0 + jnp.exp(-acc)))
    new_tail = xpad_ref[pl.ds(CONV_PAD + real_len - (CONV_WIDTH - 1), CONV_WIDTH - 1), :]

    @pl.when(c == nc - 1)
    def _():
        cso_ref[...] = new_tail

    xpad_ref[pl.ds(CONV_PAD - (CONV_WIDTH - 1), CONV_WIDTH - 1), :] = new_tail

    xs = xc[:, :SSD_WIDTH]
    gn = SSD_GROUPS * SSD_STATE

    dtx = dt_ref[...] + dtb_ref[...]
    dt = jnp.maximum(dtx, 0.0) + jnp.log1p(jnp.exp(-jnp.abs(dtx)))
    if real_len < q:
        row = lax.broadcasted_iota(jnp.int32, (q, LANES), 0)
        dt = jnp.where(row < real_len, dt, 0.0)
    a = -jnp.exp(alog_ref[...])
    da = dt * a
    ii = lax.broadcasted_iota(jnp.int32, (q, q), 0)
    jj = lax.broadcasted_iota(jnp.int32, (q, q), 1)
    causal = ii >= jj
    tril = jnp.where(causal, 1.0, 0.0).astype(F32)
    acum = jnp.dot(tril, da, preferred_element_type=F32, precision=HI)
    acum_t = acum.T
    dt_t = dt.T
    alast = acum[q - 1:q, :]
    ea = jnp.exp(acum)
    wdec = jnp.exp(alast - acum) * dt
    alast8 = jnp.broadcast_to(alast, (8, LANES))
    dec_e = jnp.exp(jnp.dot(alast8, exp_ref[...], preferred_element_type=F32, precision=HI)[0:1, :])

    lane = lax.broadcasted_iota(jnp.int32, (q, LANES), 1)
    first = lane < SSD_HEAD_DIM
    y_parts = []
    xw_parts = []
    for g in range(SSD_GROUPS):
        bg = xc[:, SSD_WIDTH + g * SSD_STATE:SSD_WIDTH + (g + 1) * SSD_STATE]
        cg = xc[:, SSD_WIDTH + gn + g * SSD_STATE:SSD_WIDTH + gn + (g + 1) * SSD_STATE]
        bg_t = bg.T
        scores = _dot(cg, bg_t)
        per_group = SSD_HEADS // SSD_GROUPS
        for pp in range(per_group // 2):
            p = g * (per_group // 2) + pp
            x_pair = xs[:, p * LANES:(p + 1) * LANES]
            st_pair = st_ref[:, p * LANES:(p + 1) * LANES]
            rhs = jnp.concatenate([x_pair, st_pair], axis=0).astype(BF16)
            outs = []
            for k in range(2):
                h = 2 * p + k
                a_col = acum[:, h:h + 1]
                a_row = acum_t[h:h + 1, :]
                seg = jnp.where(causal, a_col - a_row, 0.0)
                m_h = jnp.where(causal, jnp.exp(seg), 0.0) * scores * dt_t[h:h + 1, :]
                c_h = cg * ea[:, h:h + 1]
                lhs = jnp.concatenate([m_h, c_h], axis=1).astype(BF16)
                outs.append(jnp.dot(lhs, rhs, preferred_element_type=F32))
            y_parts.append(jnp.where(first, outs[0], outs[1]))
            w_pair = jnp.where(first, wdec[:, 2 * p:2 * p + 1], wdec[:, 2 * p + 1:2 * p + 2])
            xw_parts.append(x_pair * w_pair)
        half = SSD_WIDTH // SSD_GROUPS
        xw_g = jnp.concatenate(xw_parts[-(per_group // 2):], axis=1)
        upd = _dot(bg_t, xw_g)
        st_ref[:, g * half:(g + 1) * half] = (
            st_ref[:, g * half:(g + 1) * half] * dec_e[:, g * half:(g + 1) * half] + upd)

    y = jnp.concatenate(y_parts, axis=1) + xs * dsk_ref[...]
    zz = z_ref[...]
    y = y * (zz * (1.0 / (1.0 + jnp.exp(-zz))))
    y_ref[...] = _rms(y, nw_ref[...])

    @pl.when(c == nc - 1)
    def _():
        for p in range(n_pairs):
            so_ref[pl.ds(p * LANES, LANES), :] = st_ref[:, p * LANES:(p + 1) * LANES].T


def _ssd(xbc, z, dt, conv_state, ssm_state, conv_w, conv_b, dt_bias, a_log, d_skip, norm_w, bsz, seq, layer=0):
    has_init = conv_state is not None
    real_len = min(seq, SSD_CHUNK)
    nc = max(seq // SSD_CHUNK, 1)
    if not has_init:
        conv_state = jnp.zeros((1, bsz, CONV_WIDTH - 1, CONV_CH), F32)
        ssm_state = jnp.zeros((1, bsz, SSD_HEADS, SSD_HEAD_DIM, SSD_STATE), F32)
    s0 = ssm_state.reshape(ssm_state.shape[0], bsz, SSD_WIDTH, SSD_STATE)
    pad_h = LANES - SSD_HEADS
    dtb = jnp.pad(dt_bias, (0, pad_h)).reshape(1, LANES)
    alog = jnp.pad(a_log, (0, pad_h)).reshape(1, LANES)
    dsk = jnp.repeat(d_skip, SSD_HEAD_DIM).reshape(1, SSD_WIDTH)
    expand = (np.arange(LANES)[:, None] == (np.arange(SSD_WIDTH) // SSD_HEAD_DIM)[None, :]).astype(np.float32)
    xbc3 = xbc.reshape(bsz, seq, CONV_CH)
    z3 = z.reshape(bsz, seq, SSD_WIDTH)
    dt3 = dt.reshape(bsz, seq, LANES)
    seq_p = nc * SSD_CHUNK
    if seq_p != seq:
        pad = ((0, 0), (0, seq_p - seq), (0, 0))
        xbc3, z3, dt3 = jnp.pad(xbc3, pad), jnp.pad(z3, pad), jnp.pad(dt3, pad)

    def tok(width):
        return pl.BlockSpec((None, SSD_CHUNK, width), lambda b, c: (b, c, 0))

    y, cso, so = pl.pallas_call(
        functools.partial(_ssd_body, real_len, has_init),
        grid=(bsz, nc),
        in_specs=[tok(CONV_CH), tok(SSD_WIDTH), tok(LANES),
                  pl.BlockSpec((None, None, CONV_WIDTH - 1, CONV_CH), lambda b, c: (layer, b, 0, 0)),
                  pl.BlockSpec((None, None, SSD_WIDTH, SSD_STATE), lambda b, c: (layer, b, 0, 0)),
                  _full((CONV_WIDTH, CONV_CH)), _full((1, CONV_CH)), _full((1, LANES)), _full((1, LANES)),
                  _full((1, SSD_WIDTH)), _full((1, SSD_WIDTH)), _full((LANES, SSD_WIDTH))],
        out_specs=[tok(SSD_WIDTH),
                   pl.BlockSpec((None, CONV_WIDTH - 1, CONV_CH), lambda b, c: (b, 0, 0)),
                   pl.BlockSpec((None, SSD_WIDTH, SSD_STATE), lambda b, c: (b, 0, 0))],
        out_shape=[jax.ShapeDtypeStruct((bsz, seq_p, SSD_WIDTH), F32),
                   jax.ShapeDtypeStruct((bsz, CONV_WIDTH - 1, CONV_CH), F32),
                   jax.ShapeDtypeStruct((bsz, SSD_WIDTH, SSD_STATE), F32)],
        scratch_shapes=[pltpu.VMEM((CONV_PAD + SSD_CHUNK, CONV_CH), F32),
                        pltpu.VMEM((SSD_STATE, SSD_WIDTH), F32)],
        compiler_params=_params("parallel", "arbitrary"),
        name="ssd_scan",
    )(xbc3, z3, dt3, conv_state, s0, conv_w, conv_b.reshape(1, CONV_CH), dtb, alog, dsk,
      norm_w.reshape(1, SSD_WIDTH), jnp.asarray(expand))
    return (y[:, :seq].reshape(bsz * seq, SSD_WIDTH), cso,
            so.reshape(bsz, SSD_HEADS, SSD_HEAD_DIM, SSD_STATE))


def _t5_bucket_np(rel):
    n = np.maximum(rel, 0)
    max_exact = REL_BUCKETS // 2
    nf = np.maximum(n, 1).astype(np.float32)
    large = max_exact + (np.log(nf / np.float32(max_exact)) / np.float32(math.log(REL_MAX_DIST / max_exact))
                         * np.float32(REL_BUCKETS - max_exact)).astype(np.int32)
    large = np.minimum(large, REL_BUCKETS - 1)
    return np.where(n < max_exact, n, large).astype(np.int32)


def _bias_tiles_body(bk_ref, rb_ref, o_ref):
    h = pl.program_id(0)
    ii = lax.broadcasted_iota(jnp.int32, (MOBA_BLOCK, MOBA_BLOCK), 0)
    jj = lax.broadcasted_iota(jnp.int32, (MOBA_BLOCK, MOBA_BLOCK), 1)
    for t in range(2):
        bk = bk_ref[t]
        acc = jnp.zeros((MOBA_BLOCK, MOBA_BLOCK), F32)
        for u in range(REL_BUCKETS):
            acc = jnp.where(bk == u, rb_ref[u, h], acc)
        if t == 0:
            acc = jnp.where(jj >= ii, acc, NEG)
        o_ref[t] = acc


def _bias_tiles(rel_bias):
    i = np.arange(MOBA_BLOCK)[None, :]
    j = np.arange(MOBA_BLOCK)[:, None]
    buckets = np.stack([_t5_bucket_np(i - j), _t5_bucket_np(MOBA_BLOCK + i - j)])
    return pl.pallas_call(
        _bias_tiles_body,
        grid=(ATT_HEADS,),
        in_specs=[_full((2, MOBA_BLOCK, MOBA_BLOCK)),
                  pl.BlockSpec(memory_space=pltpu.SMEM)],
        out_specs=pl.BlockSpec((None, 2, MOBA_BLOCK, MOBA_BLOCK), lambda h: (h, 0, 0, 0)),
        out_shape=jax.ShapeDtypeStruct((ATT_HEADS, 2, MOBA_BLOCK, MOBA_BLOCK), F32),
        compiler_params=_params("parallel"),
        name="bias_tiles",
    )(jnp.asarray(buckets), rel_bias)


def _block_rank(g, blk):
    rank = jnp.zeros(g.shape, jnp.int32)
    for m in range(g.shape[1]):
        gm = g[:, m:m + 1]
        rank = rank + ((gm > g) | ((gm == g) & (m < blk))).astype(jnp.int32)
    return rank


def _block_rank_t(g, blk):
    rank = jnp.zeros(g.shape, jnp.int32)
    for m in range(g.shape[0]):
        gm = g[m:m + 1, :]
        rank = rank + ((gm > g) | ((gm == g) & (m < blk))).astype(jnp.int32)
    return rank


def _moba_prompt_body(nblk, q_ref, k_ref, v_ref, bias_ref, cfar_ref, o_ref, km_ref, k16_ref, vt_ref, qt_ref):
    pair = pl.program_id(0)
    blk_rows = MOBA_BLOCK
    hd = ATT_HEAD_DIM
    scale = hd ** -0.5
    per_blk = blk_rows // LANES
    for n in range(nblk):
        tot = None
        for c in range(n * per_blk, (n + 1) * per_blk):
            kb = k_ref[:, c * LANES:(c + 1) * LANES].T
            k16_ref[pl.ds(c * LANES, LANES), :] = kb.astype(BF16)
            part = jnp.sum(kb, axis=0, keepdims=True)
            tot = part if tot is None else tot + part
        km_ref[n:n + 1, :] = tot * (1.0 / blk_rows)
    vt_ref[...] = v_ref[...].astype(BF16)
    for c in range(nblk * per_blk):
        qt_ref[:, c * LANES:(c + 1) * LANES] = (q_ref[pl.ds(c * LANES, LANES), :] * scale).T.astype(BF16)
    kmean = km_ref[...].astype(BF16)
    row = lax.broadcasted_iota(jnp.int32, (LANES, blk_rows), 0)
    blk = lax.broadcasted_iota(jnp.int32, (nblk, blk_rows), 0)
    zero16 = jnp.zeros((LANES, blk_rows), BF16)

    for i in range(nblk):
        qt = qt_ref[:, i * blk_rows:(i + 1) * blk_rows]
        n_keys = (i + 1) * blk_rows
        halves = []
        for kk in range(2):
            qth = jnp.where(row < hd, qt, zero16) if kk == 0 else jnp.where(row >= hd, qt, zero16)
            gate = jnp.dot(kmean, qth, preferred_element_type=F32)
            past = blk < i
            gate = jnp.where(past, gate, NEG)
            sel = past & (_block_rank_t(gate, blk) < MOBA_TOPK)
            pen = jnp.where(sel, cfar_ref[2 * pair + kk], NEG)
            pen_prev = jnp.where(sel[max(i - 1, 0):max(i - 1, 0) + 1, :], 0.0, NEG)

            def logits(n):
                seg = jnp.dot(k16_ref[n * blk_rows:(n + 1) * blk_rows, :], qth, preferred_element_type=F32)
                if n == i:
                    return seg + bias_ref[kk, 0]
                if n == i - 1:
                    return seg + bias_ref[kk, 1] + pen_prev
                return seg + pen[n:n + 1, :]

            m = logits(0)
            for n in range(1, i + 1):
                m = jnp.maximum(m, logits(n))
            m = jnp.max(m, axis=0, keepdims=True)
            tot = None
            acc = None
            for n in range(i + 1):
                p = jnp.exp(logits(n) - m)
                tot = p if tot is None else tot + p
                pv = jnp.dot(vt_ref[kk * hd:(kk + 1) * hd, n * blk_rows:(n + 1) * blk_rows], p.astype(BF16),
                             preferred_element_type=F32)
                acc = pv if acc is None else acc + pv
            halves.append(acc / jnp.sum(tot, axis=0, keepdims=True))
        o_ref[pl.ds(i * blk_rows, blk_rows), :] = jnp.concatenate(halves, axis=0).T


def _moba_prompt(q, k, v, bias_tiles, cfar, bsz, seq, layer):
    nblk = seq // MOBA_BLOCK
    n_pairs = ATT_HEADS // 2
    tok = pl.BlockSpec((seq, LANES), lambda p, b: (b, p))
    tok_l = pl.BlockSpec((None, None, LANES, seq), lambda p, b: (layer, b, p, 0))
    return pl.pallas_call(
        functools.partial(_moba_prompt_body, nblk),
        grid=(n_pairs, bsz),
        in_specs=[tok, tok_l, tok_l,
                  pl.BlockSpec((2, 2, MOBA_BLOCK, MOBA_BLOCK), lambda p, b: (p, 0, 0, 0)),
                  pl.BlockSpec(memory_space=pltpu.SMEM)],
        out_specs=tok,
        out_shape=jax.ShapeDtypeStruct((bsz * seq, ATT_WIDTH), F32),
        scratch_shapes=[pltpu.VMEM((nblk, LANES), F32), pltpu.VMEM((seq, LANES), BF16),
                        pltpu.VMEM((LANES, seq), BF16), pltpu.VMEM((LANES, seq), BF16)],
        compiler_params=_params("parallel", "parallel"),
        name="moba_prompt",
    )(q, k, v, bias_tiles, cfar)


PAGES_PER_STEP = 16
PAGE_SLOTS = 4
PAGE_LOOKAHEAD = 2
ROWS8 = 8


def _sample_moba_body(layer, n_new, n_k, page_size, pt_ref, q_ref, kn_ref, vn_ref, tprev_ref, town_ref, cfar_ref,
                      ck_ref, cv_ref, o_ref, buf_ref, sem_ref, s_ref, acc_ref, l_ref):
    b = pl.program_id(0)
    n_seq = pl.num_programs(0)
    n_chunks = 2 * n_k
    n_tok = s_ref.shape[2]
    n_blk = n_tok // MOBA_BLOCK
    q = q_ref[...]
    qh = [q[:, h * ATT_HEAD_DIM:(h + 1) * ATT_HEAD_DIM] for h in range(ATT_HEADS)]
    scale = ATT_HEAD_DIM ** -0.5

    assert n_chunks % PAGE_SLOTS == 0 and PAGE_LOOKAHEAD < PAGE_SLOTS

    def chunk_copies(seq, g):
        src = ck_ref if g < n_k else cv_ref
        slot = g % PAGE_SLOTS
        return [pltpu.make_async_copy(src.at[layer, pt_ref[seq, (g % n_k) * PAGES_PER_STEP + r]],
                                      buf_ref.at[slot, r], sem_ref.at[slot])
                for r in range(PAGES_PER_STEP)]

    def start_chunk(seq, g):
        for c in chunk_copies(seq, g):
            c.start()

    def advance(g):
        for c in chunk_copies(b, g):
            c.wait()
        nxt = g + PAGE_LOOKAHEAD
        if nxt < n_chunks:
            start_chunk(b, nxt)
        else:
            @pl.when(b + 1 < n_seq)
            def _():
                start_chunk(b + 1, nxt - n_chunks)

    @pl.when(b == 0)
    def _():
        for g in range(PAGE_LOOKAHEAD):
            start_chunk(b, g)

    for g in range(n_k):
        advance(g)
        for r in range(PAGES_PER_STEP):
            start = (g * PAGES_PER_STEP + r) * page_size
            for h in range(ATT_HEADS):
                s_ref[h, :, start:start + page_size] = _dot(qh[h], buf_ref[g % PAGE_SLOTS, r, h])

    def softmax_over_selected():
        blk = lax.broadcasted_iota(jnp.int32, (ROWS8, n_blk), 1)
        col = lax.broadcasted_iota(jnp.int32, (ROWS8, LANES), 1)
        halves_per_blk = MOBA_BLOCK // LANES
        for h in range(ATT_HEADS):
            s_all = s_ref[h]
            chunks = [s_all[:, c * LANES:(c + 1) * LANES] for c in range(n_tok // LANES)]
            gate = jnp.zeros((ROWS8, n_blk), F32)
            for n in range(n_blk):
                part = chunks[n * halves_per_blk]
                for c in chunks[n * halves_per_blk + 1:(n + 1) * halves_per_blk]:
                    part = part + c
                gate = jnp.where(blk == n, jnp.sum(part, axis=1, keepdims=True) * (1.0 / MOBA_BLOCK), gate)
            sel = _block_rank(gate, blk) < MOBA_TOPK
            pen = jnp.where(sel, cfar_ref[h], NEG)
            pen_last = jnp.where(sel[:, n_blk - 1:n_blk], 0.0, NEG)
            logits = []
            for c, x in enumerate(chunks):
                n = c // halves_per_blk
                if n == n_blk - 1:
                    off = (c % halves_per_blk) * LANES
                    logits.append(x * scale + tprev_ref[h, :, off:off + LANES] + pen_last)
                else:
                    logits.append(x * scale + pen[:, n:n + 1])
            kn_h = kn_ref[:, h * ATT_HEAD_DIM:(h + 1) * ATT_HEAD_DIM]
            vn_h = vn_ref[:, h * ATT_HEAD_DIM:(h + 1) * ATT_HEAD_DIM]
            s_own = jnp.where(col < n_new, _dot_t(qh[h] * scale, kn_h) + town_ref[h], NEG)
            m_el = s_own
            for x in logits:
                m_el = jnp.maximum(m_el, x)
            m = jnp.max(m_el, axis=1, keepdims=True)
            p_own = jnp.exp(s_own - m)
            tot = p_own
            for c, x in enumerate(logits):
                p = jnp.exp(x - m)
                s_ref[h, :, c * LANES:(c + 1) * LANES] = p
                tot = tot + p
            acc_ref[h] = _dot(p_own, vn_h)
            l_ref[h] = jnp.broadcast_to(jnp.sum(tot, axis=1, keepdims=True), (ROWS8, LANES))

    softmax_over_selected()

    for g in range(n_k, n_chunks):
        advance(g)
        for h in range(ATT_HEADS):
            acc = acc_ref[h]
            for r in range(PAGES_PER_STEP):
                start = ((g - n_k) * PAGES_PER_STEP + r) * page_size
                acc = acc + _dot_t(s_ref[h, :, start:start + page_size], buf_ref[g % PAGE_SLOTS, r, h])
            acc_ref[h] = acc

    o_ref[...] = jnp.concatenate(
        [acc_ref[h] / l_ref[h][:, :ATT_HEAD_DIM] for h in range(ATT_HEADS)], axis=1)


def _sample_moba(layer, q8, kn, vn, page_table, cache_kt, cache_vt, bias_tiles, cfar, n_new):
    bsz = q8.shape[0]
    page_size = cache_kt.shape[4]
    n_pages = page_table.shape[1]
    n_k = n_pages // PAGES_PER_STEP
    tprev = jnp.transpose(bias_tiles[:, 1, :, :ROWS8], (0, 2, 1))
    town = jnp.transpose(bias_tiles[:, 0, :LANES, :ROWS8], (0, 2, 1))
    grid_spec = pltpu.PrefetchScalarGridSpec(
        num_scalar_prefetch=1,
        grid=(bsz,),
        in_specs=[pl.BlockSpec((None, ROWS8, ATT_WIDTH), lambda b, pt: (b, 0, 0)),
                  pl.BlockSpec((None, LANES, ATT_WIDTH), lambda b, pt: (b, 0, 0)),
                  pl.BlockSpec((None, LANES, ATT_WIDTH), lambda b, pt: (b, 0, 0)),
                  pl.BlockSpec((ATT_HEADS, ROWS8, MOBA_BLOCK), lambda b, pt: (0, 0, 0)),
                  pl.BlockSpec((ATT_HEADS, ROWS8, LANES), lambda b, pt: (0, 0, 0)),
                  pl.BlockSpec(memory_space=pltpu.SMEM),
                  pl.BlockSpec(memory_space=pl.ANY),
                  pl.BlockSpec(memory_space=pl.ANY)],
        out_specs=pl.BlockSpec((None, ROWS8, ATT_WIDTH), lambda b, pt: (b, 0, 0)),
        scratch_shapes=[pltpu.VMEM((PAGE_SLOTS, PAGES_PER_STEP, ATT_HEADS, ATT_HEAD_DIM, page_size), F32),
                        pltpu.SemaphoreType.DMA((PAGE_SLOTS,)),
                        pltpu.VMEM((ATT_HEADS, ROWS8, n_pages * page_size), F32),
                        pltpu.VMEM((ATT_HEADS, ROWS8, ATT_HEAD_DIM), F32),
                        pltpu.VMEM((ATT_HEADS, ROWS8, LANES), F32)],
    )
    return pl.pallas_call(
        functools.partial(_sample_moba_body, layer, n_new, n_k, page_size),
        grid_spec=grid_spec,
        out_shape=jax.ShapeDtypeStruct((bsz, ROWS8, ATT_WIDTH), F32),
        compiler_params=_params("arbitrary"),
        name="sample_moba",
    )(page_table, q8, kn, vn, tprev, town, cfar, cache_kt, cache_vt)


def _mix_ca_body(rows_per_batch, h_ref, ys_ref, ya_ref, mk_ref, mv_ref, wo1_ref, wo2_ref, g_ref,
                 wq_ref, wco_ref, o_ref):
    tm = h_ref.shape[0]
    h1 = (h_ref[...] + jnp.dot(ys_ref[...].astype(BF16), wo1_ref[...], preferred_element_type=F32)
          + jnp.dot(ya_ref[...].astype(BF16), wo2_ref[...], preferred_element_type=F32))
    hn = _rms(h1, g_ref[...]).astype(BF16)
    qq = jnp.dot(hn, wq_ref[...], preferred_element_type=F32) * (CA_HEAD_DIM ** -0.5)
    nkeys = mk_ref.shape[0] * mk_ref.shape[1]
    mk = mk_ref[...].reshape(nkeys, CA_WIDTH)
    mv = mv_ref[...].reshape(nkeys, CA_WIDTH)
    if rows_per_batch < tm:
        r = lax.broadcasted_iota(jnp.int32, (tm, nkeys), 0) // rows_per_batch
        c = lax.broadcasted_iota(jnp.int32, (tm, nkeys), 1) // MEM_LEN
        same = r == c
    outs = []
    for hd in range(CA_HEADS):
        sl = slice(hd * CA_HEAD_DIM, (hd + 1) * CA_HEAD_DIM)
        s = _dot_t(qq[:, sl], mk[:, sl])
        if rows_per_batch < tm:
            s = jnp.where(same, s, NEG)
        s = s - jnp.max(s, axis=1, keepdims=True)
        p = jnp.exp(s)
        p = p / jnp.sum(p, axis=1, keepdims=True)
        outs.append(_dot(p, mv[:, sl]))
    o = jnp.concatenate(outs, axis=1).astype(BF16)
    o_ref[...] = h1 + jnp.dot(o, wco_ref[...], preferred_element_type=F32)


def _mix_ca(h, ys, ya, mk, mv, wo1, wo2, g_ca, wq, wco, rows_per_batch, tm):
    m = h.shape[0]
    nbat = max(tm // rows_per_batch, 1)
    per = rows_per_batch // tm if rows_per_batch >= tm else 1

    def mem_map(i):
        return (i // per, 0, 0)

    def row(width):
        return pl.BlockSpec((tm, width), lambda i: (i, 0))

    return pl.pallas_call(
        functools.partial(_mix_ca_body, rows_per_batch),
        grid=(m // tm,),
        in_specs=[row(D_MODEL), row(SSD_WIDTH), row(ATT_WIDTH),
                  pl.BlockSpec((nbat, MEM_LEN, CA_WIDTH), mem_map),
                  pl.BlockSpec((nbat, MEM_LEN, CA_WIDTH), mem_map),
                  _full(wo1.shape), _full(wo2.shape), _full((1, D_MODEL)), _full(wq.shape), _full(wco.shape)],
        out_specs=row(D_MODEL),
        out_shape=jax.ShapeDtypeStruct((m, D_MODEL), F32),
        compiler_params=_params("parallel"),
        name="mix_ca",
    )(h, ys, ya, mk, mv, wo1, wo2, g_ca.reshape(1, D_MODEL), wq, wco)


FF_CHUNK = 1024


def _mlp_body(final, h_ref, g_ref, wu_ref, wd_ref, gf_ref, o_ref):
    h = h_ref[...]
    xn = _rms(h, g_ref[...]).astype(BF16)
    acc = h
    for c in range(FF // FF_CHUNK):
        u = jnp.dot(xn, wu_ref[:, c * FF_CHUNK:(c + 1) * FF_CHUNK], preferred_element_type=F32)
        u = jnp.maximum(u, 0.0)
        acc = acc + jnp.dot((u * u).astype(BF16), wd_ref[c * FF_CHUNK:(c + 1) * FF_CHUNK, :],
                            preferred_element_type=F32)
    o_ref[...] = _rms(acc, gf_ref[...]) if final else acc


def _mlp(h, g, wu, wd, g_final, final, tm):
    m = h.shape[0]
    row = pl.BlockSpec((tm, D_MODEL), lambda i: (i, 0))
    return pl.pallas_call(
        functools.partial(_mlp_body, final),
        grid=(m // tm,),
        in_specs=[row, _full((1, D_MODEL)), _full(wu.shape), _full(wd.shape), _full((1, D_MODEL))],
        out_specs=row,
        out_shape=jax.ShapeDtypeStruct((m, D_MODEL), F32),
        compiler_params=_params("parallel"),
        name="mlp",
    )(h, g.reshape(1, D_MODEL), wu, wd, g_final.reshape(1, D_MODEL))


def _layer_weights(l, w_in, w_out, w_ca_q, w_ca_kv, w_ca_o, w_up, w_down):
    w = w_in[l]
    o = 0
    parts = []
    for width in (SSD_WIDTH, CONV_CH, SSD_HEADS, ATT_WIDTH, ATT_WIDTH, ATT_WIDTH):
        parts.append(w[:, o:o + width])
        o += width
    parts[2] = jnp.pad(parts[2], ((0, 0), (0, LANES - SSD_HEADS)))
    wo = w_out[l]
    w_in = [p.astype(BF16) for p in parts]
    return dict(
        w_in=w_in,
        w_in_prompt=w_in[:4] + [w_in[4].T, w_in[5].T],
        wo1=wo[:SSD_WIDTH].astype(BF16), wo2=wo[SSD_WIDTH:].astype(BF16),
        wq=w_ca_q[l].astype(BF16), wco=w_ca_o[l].astype(BF16),
        wkv=[w_ca_kv[l][:, :CA_WIDTH].astype(BF16), w_ca_kv[l][:, CA_WIDTH:].astype(BF16)],
        wu=w_up[l].astype(BF16), wd=w_down[l].astype(BF16))


def kernel(x_prompt, x_sample, cache_k, cache_v, cache_mem_k, cache_mem_v, state_conv, state_ssm,
           page_table, mem_prompt, norm_mix, w_in, conv_w, conv_b, dt_bias, a_log, d_skip, ssd_norm,
           rel_bias, w_out, norm_ca, norm_mem, w_ca_q, w_ca_kv, w_ca_o, norm_mlp, w_up, w_down, norm_final):
    depth = w_in.shape[0]
    pb, seq, _ = x_prompt.shape
    sb, dseq, _ = x_sample.shape
    page_size = cache_k.shape[2]
    tm = 512

    bias_tiles = _bias_tiles(rel_bias)
    cfar = rel_bias[REL_BUCKETS - 1]
    cache_kt = jnp.transpose(cache_k, (0, 1, 3, 4, 2))
    cache_vt = jnp.transpose(cache_v, (0, 1, 3, 4, 2))
    mem2d = mem_prompt.reshape(pb * MEM_LEN, D_MODEL)

    def pad_rows(a, rows):
        return jnp.pad(a.reshape(sb, dseq, ATT_WIDTH), ((0, 0), (0, rows - dseq), (0, 0)))

    h = x_prompt.reshape(pb * seq, D_MODEL)
    g = x_sample.reshape(sb * dseq, D_MODEL)
    outs = {n: [] for n in ("pmk", "pmv", "pconv", "pssm", "sk", "sv", "sconv", "sssm")}
    pk = jnp.zeros((depth, pb, ATT_WIDTH, seq), F32)
    pv = jnp.zeros((depth, pb, ATT_WIDTH, seq), F32)
    for l in range(depth):
        lw = _layer_weights(l, w_in, w_out, w_ca_q, w_ca_kv, w_ca_o, w_up, w_down)
        last = l == depth - 1

        z, xbc, dt, q, pk, pv = _norm_proj(h, norm_mix[l], lw["w_in_prompt"], tm, transposed=(4, 5), layer=l,
                                           depth=depth, seq=seq, prev=[pk, pv])
        y_ssd, c_new, s_new = _ssd(xbc, z, dt, None, None, conv_w[l], conv_b[l], dt_bias[l], a_log[l],
                                   d_skip[l], ssd_norm[l], pb, seq)
        y_att = _moba_prompt(q, pk, pv, bias_tiles, cfar, pb, seq, l)
        mk, mv = _norm_proj(mem2d, norm_mem[l], lw["wkv"], min(tm, pb * MEM_LEN))
        h = _mix_ca(h, y_ssd, y_att, mk.reshape(pb, MEM_LEN, CA_WIDTH), mv.reshape(pb, MEM_LEN, CA_WIDTH),
                    lw["wo1"], lw["wo2"], norm_ca[l], lw["wq"], lw["wco"], seq, tm)
        h = _mlp(h, norm_mlp[l], lw["wu"], lw["wd"], norm_final, last, tm)
        outs["pmk"].append(mk.reshape(pb, MEM_LEN, CA_HEADS, CA_HEAD_DIM))
        outs["pmv"].append(mv.reshape(pb, MEM_LEN, CA_HEADS, CA_HEAD_DIM))
        outs["pconv"].append(c_new)
        outs["pssm"].append(s_new)

        ms = sb * dseq
        z, xbc, dt, q, k, v = _norm_proj(g, norm_mix[l], lw["w_in"], ms)
        y_ssd, c_new, s_new = _ssd(xbc, z, dt, state_conv, state_ssm, conv_w[l], conv_b[l], dt_bias[l],
                                   a_log[l], d_skip[l], ssd_norm[l], sb, dseq, layer=l)
        y_att = _sample_moba(l, pad_rows(q, ROWS8), pad_rows(k, LANES), pad_rows(v, LANES), page_table,
                             cache_kt, cache_vt, bias_tiles, cfar, dseq)[:, :dseq].reshape(ms, ATT_WIDTH)
        g = _mix_ca(g, y_ssd, y_att, cache_mem_k[l].reshape(sb, MEM_LEN, CA_WIDTH),
                    cache_mem_v[l].reshape(sb, MEM_LEN, CA_WIDTH), lw["wo1"], lw["wo2"], norm_ca[l],
                    lw["wq"], lw["wco"], dseq, min(32, ms))
        g = _mlp(g, norm_mlp[l], lw["wu"], lw["wd"], norm_final, last, ms)
        outs["sk"].append(k.reshape(sb, dseq, ATT_HEADS, ATT_HEAD_DIM))
        outs["sv"].append(v.reshape(sb, dseq, ATT_HEADS, ATT_HEAD_DIM))
        outs["sconv"].append(c_new)
        outs["sssm"].append(s_new)

    st = {n: jnp.stack(vs) for n, vs in outs.items() if vs}
    def token_major(a):
        return jnp.transpose(a.reshape(depth, pb, ATT_HEADS, ATT_HEAD_DIM, seq), (0, 1, 4, 2, 3))

    return (h.reshape(pb, seq, D_MODEL), g.reshape(sb, dseq, D_MODEL), token_major(pk), token_major(pv),
            st["pmk"], st["pmv"],
            st["pconv"], st["pssm"], st["sk"], st["sv"], st["sconv"], st["sssm"])
```

```python
import functools
import math

import numpy as np
import jax
import jax.numpy as jnp
from jax import lax
from jax.experimental import pallas as pl
from jax.experimental.pallas import tpu as pltpu

D_MODEL = 1024
SSD_HEADS = 16
SSD_HEAD_DIM = 64
SSD_WIDTH = SSD_HEADS * SSD_HEAD_DIM
SSD_GROUPS = 2
SSD_STATE = 128
SSD_CHUNK = 128
CONV_WIDTH = 4
CONV_CH = SSD_WIDTH + 2 * SSD_GROUPS * SSD_STATE
ATT_HEADS = 8
ATT_HEAD_DIM = 64
ATT_WIDTH = ATT_HEADS * ATT_HEAD_DIM
MOBA_BLOCK = 256
MOBA_TOPK = 3
REL_BUCKETS = 32
REL_MAX_DIST = 128
MEM_LEN = 256
CA_HEADS = 4
CA_HEAD_DIM = 128
CA_WIDTH = CA_HEADS * CA_HEAD_DIM
FF = 4 * D_MODEL
EPS = 1e-5

LANES = 128
VMEM_LIMIT = 56 * 1024 * 1024
NEG = -1e30
LOG2E = math.log2(math.e)

BF16 = jnp.bfloat16
F32 = jnp.float32
HI = lax.Precision.HIGHEST


def _params(*sem):
    return pltpu.CompilerParams(dimension_semantics=sem, vmem_limit_bytes=VMEM_LIMIT)


def _rms(x, gain):
    return x * lax.rsqrt(jnp.mean(x * x, axis=-1, keepdims=True) + EPS) * gain


def _dot(a, b):
    return jnp.dot(a.astype(BF16), b.astype(BF16), preferred_element_type=F32)


def _dot_t(a, b):
    return lax.dot_general(a.astype(BF16), b.astype(BF16), (((1,), (1,)), ((), ())),
                           preferred_element_type=F32)


def _full(shape):
    return pl.BlockSpec(shape, lambda *_: (0,) * len(shape))


def _norm_proj_body(n_out, transposed, x_ref, g_ref, *refs):
    w_refs, o_refs = refs[:n_out], refs[len(refs) - n_out:]
    xn = _rms(x_ref[...], g_ref[...]).astype(BF16)
    for idx, (w_ref, o_ref) in enumerate(zip(w_refs, o_refs)):
        if idx in transposed:
            o_ref[...] = lax.dot_general(w_ref[...], xn, (((1,), (1,)), ((), ())), preferred_element_type=F32)
        else:
            o_ref[...] = jnp.dot(xn, w_ref[...], preferred_element_type=F32)


def _norm_proj(x, gain, weights, tm, transposed=(), layer=0, depth=1, seq=None, prev=None):
    m, d = x.shape
    n_out = len(weights)
    out_specs, out_shape = [], []
    for idx, w in enumerate(weights):
        if idx in transposed:
            n = w.shape[0]
            per = seq // tm
            out_specs.append(pl.BlockSpec((None, None, n, tm), lambda i: (layer, i // per, 0, i % per)))
            out_shape.append(jax.ShapeDtypeStruct((depth, m // seq, n, seq), F32))
        else:
            n = w.shape[1]
            out_specs.append(pl.BlockSpec((tm, n), lambda i: (i, 0)))
            out_shape.append(jax.ShapeDtypeStruct((m, n), F32))
    prev = list(prev) if prev is not None else []
    n_in = 2 + n_out
    aliases = {n_in + k: idx for k, idx in enumerate(transposed)} if prev else {}
    return pl.pallas_call(
        functools.partial(_norm_proj_body, n_out, tuple(transposed)),
        grid=(m // tm,),
        in_specs=[pl.BlockSpec((tm, d), lambda i: (i, 0)), _full((1, d))]
        + [_full(w.shape) for w in weights] + [pl.BlockSpec(memory_space=pl.ANY) for _ in prev],
        out_specs=out_specs,
        out_shape=out_shape,
        input_output_aliases=aliases,
        compiler_params=_params("parallel"),
        name="norm_proj",
    )(x, gain.reshape(1, d), *weights, *prev)


CONV_PAD = 8


def _ssd_body(real_len, has_init, xbc_ref, z_ref, dt_ref, cs_ref, s0_ref, cw_ref, cb_ref, dtb_ref,
              alog_ref, dsk_ref, nw_ref, exp_ref, y_ref, cso_ref, so_ref, xpad_ref, st_ref):
    c = pl.program_id(1)
    nc = pl.num_programs(1)
    q = SSD_CHUNK
    n_pairs = SSD_HEADS // 2

    @pl.when(c == 0)
    def _():
        if has_init:
            xpad_ref[pl.ds(0, CONV_PAD), :] = jnp.zeros((CONV_PAD, CONV_CH), F32)
            xpad_ref[pl.ds(CONV_PAD - (CONV_WIDTH - 1), CONV_WIDTH - 1), :] = cs_ref[...]
            for p in range(n_pairs):
                st_ref[:, p * LANES:(p + 1) * LANES] = s0_ref[pl.ds(p * LANES, LANES), :].T
        else:
            xpad_ref[pl.ds(0, CONV_PAD), :] = jnp.zeros((CONV_PAD, CONV_CH), F32)
            st_ref[...] = jnp.zeros_like(st_ref)

    xpad_ref[pl.ds(CONV_PAD, q), :] = xbc_ref[...]

    acc = cb_ref[...] + xpad_ref[pl.ds(CONV_PAD - 3, q), :] * cw_ref[0:1, :]
    acc = acc + xpad_ref[pl.ds(CONV_PAD - 2, q), :] * cw_ref[1:2, :]
    acc = acc + xpad_ref[pl.ds(CONV_PAD - 1, q), :] * cw_ref[2:3, :]
    acc = acc + xpad_ref[pl.ds(CONV_PAD, q), :] * cw_ref[3:4, :]
    xc = acc * (1.0 / (1.0 + jnp.exp(-acc)))
    new_tail = xpad_ref[pl.ds(CONV_PAD + real_len - (CONV_WIDTH - 1), CONV_WIDTH - 1), :]

    @pl.when(c == nc - 1)
    def _():
        cso_ref[...] = new_tail

    xpad_ref[pl.ds(CONV_PAD - (CONV_WIDTH - 1), CONV_WIDTH - 1), :] = new_tail

    xs = xc[:, :SSD_WIDTH]
    gn = SSD_GROUPS * SSD_STATE

    dtx = dt_ref[...] + dtb_ref[...]
    dt = jnp.maximum(dtx, 0.0) + jnp.log1p(jnp.exp(-jnp.abs(dtx)))
    if real_len < q:
        row = lax.broadcasted_iota(jnp.int32, (q, LANES), 0)
        dt = jnp.where(row < real_len, dt, 0.0)
    a = -jnp.exp(alog_ref[...])
    da = dt * a
    ii = lax.broadcasted_iota(jnp.int32, (q, q), 0)
    jj = lax.broadcasted_iota(jnp.int32, (q, q), 1)
    causal = ii >= jj
    tril = jnp.where(causal, 1.0, 0.0).astype(F32)
    acum = jnp.dot(tril, da, preferred_element_type=F32, precision=HI)
    acum_t = acum.T
    dt_t = dt.T
    alast = acum[q - 1:q, :]
    ea = jnp.exp(acum)
    wdec = jnp.exp(alast - acum) * dt
    alast8 = jnp.broadcast_to(alast, (8, LANES))
    dec_e = jnp.exp(jnp.dot(alast8, exp_ref[...], preferred_element_type=F32, precision=HI)[0:1, :])

    lane = lax.broadcasted_iota(jnp.int32, (q, LANES), 1)
    first = lane < SSD_HEAD_DIM
    y_parts = []
    xw_parts = []
    for g in range(SSD_GROUPS):
        bg = xc[:, SSD_WIDTH + g * SSD_STATE:SSD_WIDTH + (g + 1) * SSD_STATE]
        cg = xc[:, SSD_WIDTH + gn + g * SSD_STATE:SSD_WIDTH + gn + (g + 1) * SSD_STATE]
        bg_t = bg.T
        scores = _dot(cg, bg_t)
        per_group = SSD_HEADS // SSD_GROUPS
        for pp in range(per_group // 2):
            p = g * (per_group // 2) + pp
            x_pair = xs[:, p * LANES:(p + 1) * LANES]
            st_pair = st_ref[:, p * LANES:(p + 1) * LANES]
            rhs = jnp.concatenate([x_pair, st_pair], axis=0).astype(BF16)
            outs = []
            for k in range(2):
                h = 2 * p + k
                a_col = acum[:, h:h + 1]
                a_row = acum_t[h:h + 1, :]
                seg = jnp.where(causal, a_col - a_row, 0.0)
                m_h = jnp.where(causal, jnp.exp(seg), 0.0) * scores * dt_t[h:h + 1, :]
                c_h = cg * ea[:, h:h + 1]
                lhs = jnp.concatenate([m_h, c_h], axis=1).astype(BF16)
                outs.append(jnp.dot(lhs, rhs, preferred_element_type=F32))
            y_parts.append(jnp.where(first, outs[0], outs[1]))
            w_pair = jnp.where(first, wdec[:, 2 * p:2 * p + 1], wdec[:, 2 * p + 1:2 * p + 2])
            xw_parts.append(x_pair * w_pair)
        half = SSD_WIDTH // SSD_GROUPS
        xw_g = jnp.concatenate(xw_parts[-(per_group // 2):], axis=1)
        upd = _dot(bg_t, xw_g)
        st_ref[:, g * half:(g + 1) * half] = (
            st_ref[:, g * half:(g + 1) * half] * dec_e[:, g * half:(g + 1) * half] + upd)

    y = jnp.concatenate(y_parts, axis=1) + xs * dsk_ref[...]
    zz = z_ref[...]
    y = y * (zz * (1.0 / (1.0 + jnp.exp(-zz))))
    y_ref[...] = _rms(y, nw_ref[...])

    @pl.when(c == nc - 1)
    def _():
        for p in range(n_pairs):
            so_ref[pl.ds(p * LANES, LANES), :] = st_ref[:, p * LANES:(p + 1) * LANES].T


def _ssd(xbc, z, dt, conv_state, ssm_state, conv_w, conv_b, dt_bias, a_log, d_skip, norm_w, bsz, seq, layer=0):
    has_init = conv_state is not None
    real_len = min(seq, SSD_CHUNK)
    nc = max(seq // SSD_CHUNK, 1)
    if not has_init:
        conv_state = jnp.zeros((1, bsz, CONV_WIDTH - 1, CONV_CH), F32)
        ssm_state = jnp.zeros((1, bsz, SSD_HEADS, SSD_HEAD_DIM, SSD_STATE), F32)
    s0 = ssm_state.reshape(ssm_state.shape[0], bsz, SSD_WIDTH, SSD_STATE)
    pad_h = LANES - SSD_HEADS
    dtb = jnp.pad(dt_bias, (0, pad_h)).reshape(1, LANES)
    alog = jnp.pad(a_log, (0, pad_h)).reshape(1, LANES)
    dsk = jnp.repeat(d_skip, SSD_HEAD_DIM).reshape(1, SSD_WIDTH)
    expand = (np.arange(LANES)[:, None] == (np.arange(SSD_WIDTH) // SSD_HEAD_DIM)[None, :]).astype(np.float32)
    xbc3 = xbc.reshape(bsz, seq, CONV_CH)
    z3 = z.reshape(bsz, seq, SSD_WIDTH)
    dt3 = dt.reshape(bsz, seq, LANES)
    seq_p = nc * SSD_CHUNK
    if seq_p != seq:
        pad = ((0, 0), (0, seq_p - seq), (0, 0))
        xbc3, z3, dt3 = jnp.pad(xbc3, pad), jnp.pad(z3, pad), jnp.pad(dt3, pad)

    def tok(width):
        return pl.BlockSpec((None, SSD_CHUNK, width), lambda b, c: (b, c, 0))

    y, cso, so = pl.pallas_call(
        functools.partial(_ssd_body, real_len, has_init),
        grid=(bsz, nc),
        in_specs=[tok(CONV_CH), tok(SSD_WIDTH), tok(LANES),
                  pl.BlockSpec((None, None, CONV_WIDTH - 1, CONV_CH), lambda b, c: (layer, b, 0, 0)),
                  pl.BlockSpec((None, None, SSD_WIDTH, SSD_STATE), lambda b, c: (layer, b, 0, 0)),
                  _full((CONV_WIDTH, CONV_CH)), _full((1, CONV_CH)), _full((1, LANES)), _full((1, LANES)),
                  _full((1, SSD_WIDTH)), _full((1, SSD_WIDTH)), _full((LANES, SSD_WIDTH))],
        out_specs=[tok(SSD_WIDTH),
                   pl.BlockSpec((None, CONV_WIDTH - 1, CONV_CH), lambda b, c: (b, 0, 0)),
                   pl.BlockSpec((None, SSD_WIDTH, SSD_STATE), lambda b, c: (b, 0, 0))],
        out_shape=[jax.ShapeDtypeStruct((bsz, seq_p, SSD_WIDTH), F32),
                   jax.ShapeDtypeStruct((bsz, CONV_WIDTH - 1, CONV_CH), F32),
                   jax.ShapeDtypeStruct((bsz, SSD_WIDTH, SSD_STATE), F32)],
        scratch_shapes=[pltpu.VMEM((CONV_PAD + SSD_CHUNK, CONV_CH), F32),
                        pltpu.VMEM((SSD_STATE, SSD_WIDTH), F32)],
        compiler_params=_params("parallel", "arbitrary"),
        name="ssd_scan",
    )(xbc3, z3, dt3, conv_state, s0, conv_w, conv_b.reshape(1, CONV_CH), dtb, alog, dsk,
      norm_w.reshape(1, SSD_WIDTH), jnp.asarray(expand))
    return (y[:, :seq].reshape(bsz * seq, SSD_WIDTH), cso,
            so.reshape(bsz, SSD_HEADS, SSD_HEAD_DIM, SSD_STATE))


def _t5_bucket_np(rel):
    n = np.maximum(rel, 0)
    max_exact = REL_BUCKETS // 2
    nf = np.maximum(n, 1).astype(np.float32)
    large = max_exact + (np.log(nf / np.float32(max_exact)) / np.float32(math.log(REL_MAX_DIST / max_exact))
                         * np.float32(REL_BUCKETS - max_exact)).astype(np.int32)
    large = np.minimum(large, REL_BUCKETS - 1)
    return np.where(n < max_exact, n, large).astype(np.int32)


def _bias_tiles_body(bk_ref, rb_ref, o_ref):
    h = pl.program_id(0)
    ii = lax.broadcasted_iota(jnp.int32, (MOBA_BLOCK, MOBA_BLOCK), 0)
    jj = lax.broadcasted_iota(jnp.int32, (MOBA_BLOCK, MOBA_BLOCK), 1)
    for t in range(2):
        bk = bk_ref[t]
        acc = jnp.zeros((MOBA_BLOCK, MOBA_BLOCK), F32)
        for u in range(REL_BUCKETS):
            acc = jnp.where(bk == u, rb_ref[u, h], acc)
        if t == 0:
            acc = jnp.where(jj >= ii, acc, NEG)
        o_ref[t] = acc


def _bias_tiles(rel_bias):
    i = np.arange(MOBA_BLOCK)[None, :]
    j = np.arange(MOBA_BLOCK)[:, None]
    buckets = np.stack([_t5_bucket_np(i - j), _t5_bucket_np(MOBA_BLOCK + i - j)])
    return pl.pallas_call(
        _bias_tiles_body,
        grid=(ATT_HEADS,),
        in_specs=[_full((2, MOBA_BLOCK, MOBA_BLOCK)),
                  pl.BlockSpec(memory_space=pltpu.SMEM)],
        out_specs=pl.BlockSpec((None, 2, MOBA_BLOCK, MOBA_BLOCK), lambda h: (h, 0, 0, 0)),
        out_shape=jax.ShapeDtypeStruct((ATT_HEADS, 2, MOBA_BLOCK, MOBA_BLOCK), F32),
        compiler_params=_params("parallel"),
        name="bias_tiles",
    )(jnp.asarray(buckets), rel_bias)


def _block_rank(g, blk):
    rank = jnp.zeros(g.shape, jnp.int32)
    for m in range(g.shape[1]):
        gm = g[:, m:m + 1]
        rank = rank + ((gm > g) | ((gm == g) & (m < blk))).astype(jnp.int32)
    return rank


def _block_rank_t(g, blk):
    rank = jnp.zeros(g.shape, jnp.int32)
    for m in range(g.shape[0]):
        gm = g[m:m + 1, :]
        rank = rank + ((gm > g) | ((gm == g) & (m < blk))).astype(jnp.int32)
    return rank


def _moba_prompt_body(nblk, q_ref, k_ref, v_ref, bias_ref, cfar_ref, o_ref, km_ref, k16_ref, vt_ref, qt_ref,
                      b2_ref):
    pair = pl.program_id(0)
    blk_rows = MOBA_BLOCK
    hd = ATT_HEAD_DIM
    scale = hd ** -0.5 * LOG2E
    for kk in range(2):
        for t in range(2):
            b2_ref[kk, t] = bias_ref[kk, t] * LOG2E
    per_blk = blk_rows // LANES
    for n in range(nblk):
        tot = None
        for c in range(n * per_blk, (n + 1) * per_blk):
            kb = k_ref[:, c * LANES:(c + 1) * LANES].T
            k16_ref[pl.ds(c * LANES, LANES), :] = kb.astype(BF16)
            part = jnp.sum(kb, axis=0, keepdims=True)
            tot = part if tot is None else tot + part
        km_ref[n:n + 1, :] = tot * (1.0 / blk_rows)
    vt_ref[...] = v_ref[...].astype(BF16)
    for c in range(nblk * per_blk):
        qt_ref[:, c * LANES:(c + 1) * LANES] = (q_ref[pl.ds(c * LANES, LANES), :] * scale).T.astype(BF16)
    kmean = km_ref[...].astype(BF16)
    row = lax.broadcasted_iota(jnp.int32, (LANES, blk_rows), 0)
    blk = lax.broadcasted_iota(jnp.int32, (nblk, blk_rows), 0)
    zero16 = jnp.zeros((LANES, blk_rows), BF16)

    for i in range(nblk):
        qt = qt_ref[:, i * blk_rows:(i + 1) * blk_rows]
        n_keys = (i + 1) * blk_rows
        halves = []
        for kk in range(2):
            qth = jnp.where(row < hd, qt, zero16) if kk == 0 else jnp.where(row >= hd, qt, zero16)
            gate = jnp.dot(kmean, qth, preferred_element_type=F32)
            past = blk < i
            gate = jnp.where(past, gate, NEG)
            sel = past & (_block_rank_t(gate, blk) < MOBA_TOPK)
            pen = jnp.where(sel, cfar_ref[2 * pair + kk] * LOG2E, NEG)
            pen_prev = jnp.where(sel[max(i - 1, 0):max(i - 1, 0) + 1, :], 0.0, NEG)

            def logits(n):
                seg = jnp.dot(k16_ref[n * blk_rows:(n + 1) * blk_rows, :], qth, preferred_element_type=F32)
                if n == i:
                    return seg + b2_ref[kk, 0]
                if n == i - 1:
                    return seg + b2_ref[kk, 1] + pen_prev
                return seg + pen[n:n + 1, :]

            m = logits(0)
            for n in range(1, i + 1):
                m = jnp.maximum(m, logits(n))
            m = jnp.max(m, axis=0, keepdims=True)
            tot = None
            acc = None
            for n in range(i + 1):
                p = jnp.exp2(logits(n) - m)
                tot = p if tot is None else tot + p
                pv = jnp.dot(vt_ref[kk * hd:(kk + 1) * hd, n * blk_rows:(n + 1) * blk_rows], p.astype(BF16),
                             preferred_element_type=F32)
                acc = pv if acc is None else acc + pv
            halves.append(acc / jnp.sum(tot, axis=0, keepdims=True))
        o_ref[pl.ds(i * blk_rows, blk_rows), :] = jnp.concatenate(halves, axis=0).T


def _moba_prompt(q, k, v, bias_tiles, cfar, bsz, seq, layer):
    nblk = seq // MOBA_BLOCK
    n_pairs = ATT_HEADS // 2
    tok = pl.BlockSpec((seq, LANES), lambda p, b: (b, p))
    tok_l = pl.BlockSpec((None, None, LANES, seq), lambda p, b: (layer, b, p, 0))
    return pl.pallas_call(
        functools.partial(_moba_prompt_body, nblk),
        grid=(n_pairs, bsz),
        in_specs=[tok, tok_l, tok_l,
                  pl.BlockSpec((2, 2, MOBA_BLOCK, MOBA_BLOCK), lambda p, b: (p, 0, 0, 0)),
                  pl.BlockSpec(memory_space=pltpu.SMEM)],
        out_specs=tok,
        out_shape=jax.ShapeDtypeStruct((bsz * seq, ATT_WIDTH), F32),
        scratch_shapes=[pltpu.VMEM((nblk, LANES), F32), pltpu.VMEM((seq, LANES), BF16),
                        pltpu.VMEM((LANES, seq), BF16), pltpu.VMEM((LANES, seq), BF16),
                        pltpu.VMEM((2, 2, MOBA_BLOCK, MOBA_BLOCK), F32)],
        compiler_params=_params("parallel", "parallel"),
        name="moba_prompt",
    )(q, k, v, bias_tiles, cfar)


PAGES_PER_STEP = 16
PAGE_SLOTS = 4
PAGE_LOOKAHEAD = 3
ROWS8 = 8


def _sample_moba_body(layer, n_new, n_k, page_size, pt_ref, q_ref, kn_ref, vn_ref, tprev_ref, town_ref, cfar_ref,
                      ck_ref, cv_ref, o_ref, buf_ref, sem_ref, s_ref, acc_ref, l_ref):
    b = pl.program_id(0)
    n_seq = pl.num_programs(0)
    n_chunks = 2 * n_k
    n_tok = s_ref.shape[2]
    n_blk = n_tok // MOBA_BLOCK
    q = q_ref[...]
    qh = [q[:, h * ATT_HEAD_DIM:(h + 1) * ATT_HEAD_DIM] for h in range(ATT_HEADS)]
    scale = ATT_HEAD_DIM ** -0.5

    assert n_chunks % PAGE_SLOTS == 0 and PAGE_LOOKAHEAD < PAGE_SLOTS

    def chunk_copies(seq, g):
        src = ck_ref if g < n_k else cv_ref
        slot = g % PAGE_SLOTS
        return [pltpu.make_async_copy(src.at[layer, pt_ref[seq, (g % n_k) * PAGES_PER_STEP + r]],
                                      buf_ref.at[slot, r], sem_ref.at[slot])
                for r in range(PAGES_PER_STEP)]

    def start_chunk(seq, g):
        for c in chunk_copies(seq, g):
            c.start()

    def advance(g):
        for c in chunk_copies(b, g):
            c.wait()
        nxt = g + PAGE_LOOKAHEAD
        if nxt < n_chunks:
            start_chunk(b, nxt)
        else:
            @pl.when(b + 1 < n_seq)
            def _():
                start_chunk(b + 1, nxt - n_chunks)

    @pl.when(b == 0)
    def _():
        for g in range(PAGE_LOOKAHEAD):
            start_chunk(b, g)

    for g in range(n_k):
        advance(g)
        for r in range(PAGES_PER_STEP):
            start = (g * PAGES_PER_STEP + r) * page_size
            for h in range(ATT_HEADS):
                s_ref[h, :, start:start + page_size] = _dot(qh[h], buf_ref[g % PAGE_SLOTS, r, h])

    def softmax_over_selected():
        blk = lax.broadcasted_iota(jnp.int32, (ROWS8, n_blk), 1)
        col = lax.broadcasted_iota(jnp.int32, (ROWS8, LANES), 1)
        halves_per_blk = MOBA_BLOCK // LANES
        for h in range(ATT_HEADS):
            s_all = s_ref[h]
            chunks = [s_all[:, c * LANES:(c + 1) * LANES] for c in range(n_tok // LANES)]
            gate = jnp.zeros((ROWS8, n_blk), F32)
            for n in range(n_blk):
                part = chunks[n * halves_per_blk]
                for c in chunks[n * halves_per_blk + 1:(n + 1) * halves_per_blk]:
                    part = part + c
                gate = jnp.where(blk == n, jnp.sum(part, axis=1, keepdims=True) * (1.0 / MOBA_BLOCK), gate)
            sel = _block_rank(gate, blk) < MOBA_TOPK
            pen = jnp.where(sel, cfar_ref[h], NEG)
            pen_last = jnp.where(sel[:, n_blk - 1:n_blk], 0.0, NEG)
            logits = []
            for c, x in enumerate(chunks):
                n = c // halves_per_blk
                if n == n_blk - 1:
                    off = (c % halves_per_blk) * LANES
                    logits.append(x * scale + tprev_ref[h, :, off:off + LANES] + pen_last)
                else:
                    logits.append(x * scale + pen[:, n:n + 1])
            kn_h = kn_ref[:, h * ATT_HEAD_DIM:(h + 1) * ATT_HEAD_DIM]
            vn_h = vn_ref[:, h * ATT_HEAD_DIM:(h + 1) * ATT_HEAD_DIM]
            s_own = jnp.where(col < n_new, _dot_t(qh[h] * scale, kn_h) + town_ref[h], NEG)
            m_el = s_own
            for x in logits:
                m_el = jnp.maximum(m_el, x)
            m = jnp.max(m_el, axis=1, keepdims=True)
            p_own = jnp.exp(s_own - m)
            tot = p_own
            for c, x in enumerate(logits):
                p = jnp.exp(x - m)
                s_ref[h, :, c * LANES:(c + 1) * LANES] = p
                tot = tot + p
            acc_ref[h] = _dot(p_own, vn_h)
            l_ref[h] = jnp.broadcast_to(jnp.sum(tot, axis=1, keepdims=True), (ROWS8, LANES))

    softmax_over_selected()

    for g in range(n_k, n_chunks):
        advance(g)
        for h in range(ATT_HEADS):
            acc = acc_ref[h]
            for r in range(PAGES_PER_STEP):
                start = ((g - n_k) * PAGES_PER_STEP + r) * page_size
                acc = acc + _dot_t(s_ref[h, :, start:start + page_size], buf_ref[g % PAGE_SLOTS, r, h])
            acc_ref[h] = acc

    o_ref[...] = jnp.concatenate(
        [acc_ref[h] / l_ref[h][:, :ATT_HEAD_DIM] for h in range(ATT_HEADS)], axis=1)


def _sample_moba(layer, q8, kn, vn, page_table, cache_kt, cache_vt, bias_tiles, cfar, n_new):
    bsz = q8.shape[0]
    page_size = cache_kt.shape[4]
    n_pages = page_table.shape[1]
    n_k = n_pages // PAGES_PER_STEP
    tprev = jnp.transpose(bias_tiles[:, 1, :, :ROWS8], (0, 2, 1))
    town = jnp.transpose(bias_tiles[:, 0, :LANES, :ROWS8], (0, 2, 1))
    grid_spec = pltpu.PrefetchScalarGridSpec(
        num_scalar_prefetch=1,
        grid=(bsz,),
        in_specs=[pl.BlockSpec((None, ROWS8, ATT_WIDTH), lambda b, pt: (b, 0, 0)),
                  pl.BlockSpec((None, LANES, ATT_WIDTH), lambda b, pt: (b, 0, 0)),
                  pl.BlockSpec((None, LANES, ATT_WIDTH), lambda b, pt: (b, 0, 0)),
                  pl.BlockSpec((ATT_HEADS, ROWS8, MOBA_BLOCK), lambda b, pt: (0, 0, 0)),
                  pl.BlockSpec((ATT_HEADS, ROWS8, LANES), lambda b, pt: (0, 0, 0)),
                  pl.BlockSpec(memory_space=pltpu.SMEM),
                  pl.BlockSpec(memory_space=pl.ANY),
                  pl.BlockSpec(memory_space=pl.ANY)],
        out_specs=pl.BlockSpec((None, ROWS8, ATT_WIDTH), lambda b, pt: (b, 0, 0)),
        scratch_shapes=[pltpu.VMEM((PAGE_SLOTS, PAGES_PER_STEP, ATT_HEADS, ATT_HEAD_DIM, page_size), F32),
                        pltpu.SemaphoreType.DMA((PAGE_SLOTS,)),
                        pltpu.VMEM((ATT_HEADS, ROWS8, n_pages * page_size), F32),
                        pltpu.VMEM((ATT_HEADS, ROWS8, ATT_HEAD_DIM), F32),
                        pltpu.VMEM((ATT_HEADS, ROWS8, LANES), F32)],
    )
    return pl.pallas_call(
        functools.partial(_sample_moba_body, layer, n_new, n_k, page_size),
        grid_spec=grid_spec,
        out_shape=jax.ShapeDtypeStruct((bsz, ROWS8, ATT_WIDTH), F32),
        compiler_params=_params("arbitrary"),
        name="sample_moba",
    )(page_table, q8, kn, vn, tprev, town, cfar, cache_kt, cache_vt)


def _mix_ca_body(rows_per_batch, h_ref, ys_ref, ya_ref, mk_ref, mv_ref, wo1_ref, wo2_ref, g_ref,
                 wq_ref, wco_ref, o_ref):
    tm = h_ref.shape[0]
    h1 = (h_ref[...] + jnp.dot(ys_ref[...].astype(BF16), wo1_ref[...], preferred_element_type=F32)
          + jnp.dot(ya_ref[...].astype(BF16), wo2_ref[...], preferred_element_type=F32))
    hn = _rms(h1, g_ref[...]).astype(BF16)
    qq = jnp.dot(hn, wq_ref[...], preferred_element_type=F32) * (CA_HEAD_DIM ** -0.5)
    nkeys = mk_ref.shape[0] * mk_ref.shape[1]
    if rows_per_batch < tm:
        r = lax.broadcasted_iota(jnp.int32, (tm, nkeys), 0) // rows_per_batch
        c = lax.broadcasted_iota(jnp.int32, (tm, nkeys), 1) // MEM_LEN
        same = r == c
    outs = []
    for hd in range(CA_HEADS):
        sl = slice(hd * CA_HEAD_DIM, (hd + 1) * CA_HEAD_DIM)
        s = _dot_t(qq[:, sl], mk_ref[:, :, hd, :].reshape(nkeys, CA_HEAD_DIM))
        if rows_per_batch < tm:
            s = jnp.where(same, s, NEG)
        s = s - jnp.max(s, axis=1, keepdims=True)
        p = jnp.exp(s)
        p = p / jnp.sum(p, axis=1, keepdims=True)
        outs.append(_dot(p, mv_ref[:, :, hd, :].reshape(nkeys, CA_HEAD_DIM)))
    o = jnp.concatenate(outs, axis=1).astype(BF16)
    o_ref[...] = h1 + jnp.dot(o, wco_ref[...], preferred_element_type=F32)


def _mix_ca(h, ys, ya, mk, mv, wo1, wo2, g_ca, wq, wco, rows_per_batch, tm, layer=0):
    m = h.shape[0]
    nbat = max(tm // rows_per_batch, 1)
    per = rows_per_batch // tm if rows_per_batch >= tm else 1

    def mem_map(i):
        return (layer, i // per, 0, 0, 0)

    mem_spec = pl.BlockSpec((None, nbat, MEM_LEN, CA_HEADS, CA_HEAD_DIM), mem_map)

    def row(width):
        return pl.BlockSpec((tm, width), lambda i: (i, 0))

    return pl.pallas_call(
        functools.partial(_mix_ca_body, rows_per_batch),
        grid=(m // tm,),
        in_specs=[row(D_MODEL), row(SSD_WIDTH), row(ATT_WIDTH),
                  mem_spec, mem_spec,
                  _full(wo1.shape), _full(wo2.shape), _full((1, D_MODEL)), _full(wq.shape), _full(wco.shape)],
        out_specs=row(D_MODEL),
        out_shape=jax.ShapeDtypeStruct((m, D_MODEL), F32),
        compiler_params=_params("parallel"),
        name="mix_ca",
    )(h, ys, ya, mk, mv, wo1, wo2, g_ca.reshape(1, D_MODEL), wq, wco)


FF_CHUNK = 1024


def _mlp_body(final, h_ref, g_ref, wu_ref, wd_ref, gf_ref, o_ref):
    h = h_ref[...]
    xn = _rms(h, g_ref[...]).astype(BF16)
    acc = h
    for c in range(FF // FF_CHUNK):
        u = jnp.dot(xn, wu_ref[:, c * FF_CHUNK:(c + 1) * FF_CHUNK], preferred_element_type=F32)
        u = jnp.maximum(u, 0.0)
        acc = acc + jnp.dot((u * u).astype(BF16), wd_ref[c * FF_CHUNK:(c + 1) * FF_CHUNK, :],
                            preferred_element_type=F32)
    o_ref[...] = _rms(acc, gf_ref[...]) if final else acc


def _mlp(h, g, wu, wd, g_final, final, tm):
    m = h.shape[0]
    row = pl.BlockSpec((tm, D_MODEL), lambda i: (i, 0))
    return pl.pallas_call(
        functools.partial(_mlp_body, final),
        grid=(m // tm,),
        in_specs=[row, _full((1, D_MODEL)), _full(wu.shape), _full(wd.shape), _full((1, D_MODEL))],
        out_specs=row,
        out_shape=jax.ShapeDtypeStruct((m, D_MODEL), F32),
        compiler_params=_params("parallel"),
        name="mlp",
    )(h, g.reshape(1, D_MODEL), wu, wd, g_final.reshape(1, D_MODEL))


def _layer_weights(l, w_in, w_out, w_ca_q, w_ca_kv, w_ca_o, w_up, w_down):
    w = w_in[l]
    o = 0
    parts = []
    for width in (SSD_WIDTH, CONV_CH, SSD_HEADS, ATT_WIDTH, ATT_WIDTH, ATT_WIDTH):
        parts.append(w[:, o:o + width])
        o += width
    parts[2] = jnp.pad(parts[2], ((0, 0), (0, LANES - SSD_HEADS)))
    wo = w_out[l]
    w_in = [p.astype(BF16) for p in parts]
    return dict(
        w_in=w_in,
        w_in_prompt=w_in[:4] + [w_in[4].T, w_in[5].T],
        wo1=wo[:SSD_WIDTH].astype(BF16), wo2=wo[SSD_WIDTH:].astype(BF16),
        wq=w_ca_q[l].astype(BF16), wco=w_ca_o[l].astype(BF16),
        wkv=[w_ca_kv[l][:, :CA_WIDTH].astype(BF16), w_ca_kv[l][:, CA_WIDTH:].astype(BF16)],
        wu=w_up[l].astype(BF16), wd=w_down[l].astype(BF16))


def kernel(x_prompt, x_sample, cache_k, cache_v, cache_mem_k, cache_mem_v, state_conv, state_ssm,
           page_table, mem_prompt, norm_mix, w_in, conv_w, conv_b, dt_bias, a_log, d_skip, ssd_norm,
           rel_bias, w_out, norm_ca, norm_mem, w_ca_q, w_ca_kv, w_ca_o, norm_mlp, w_up, w_down, norm_final):
    depth = w_in.shape[0]
    pb, seq, _ = x_prompt.shape
    sb, dseq, _ = x_sample.shape
    page_size = cache_k.shape[2]
    tm = 512

    bias_tiles = _bias_tiles(rel_bias)
    cfar = rel_bias[REL_BUCKETS - 1]
    cache_kt = jnp.transpose(cache_k, (0, 1, 3, 4, 2))
    cache_vt = jnp.transpose(cache_v, (0, 1, 3, 4, 2))
    mem2d = mem_prompt.reshape(pb * MEM_LEN, D_MODEL)

    def pad_rows(a, rows):
        return jnp.pad(a.reshape(sb, dseq, ATT_WIDTH), ((0, 0), (0, rows - dseq), (0, 0)))

    h = x_prompt.reshape(pb * seq, D_MODEL)
    g = x_sample.reshape(sb * dseq, D_MODEL)
    outs = {n: [] for n in ("pmk", "pmv", "pconv", "pssm", "sk", "sv", "sconv", "sssm")}
    pk = jnp.zeros((depth, pb, ATT_WIDTH, seq), F32)
    pv = jnp.zeros((depth, pb, ATT_WIDTH, seq), F32)
    for l in range(depth):
        lw = _layer_weights(l, w_in, w_out, w_ca_q, w_ca_kv, w_ca_o, w_up, w_down)
        last = l == depth - 1

        z, xbc, dt, q, pk, pv = _norm_proj(h, norm_mix[l], lw["w_in_prompt"], tm, transposed=(4, 5), layer=l,
                                           depth=depth, seq=seq, prev=[pk, pv])
        y_ssd, c_new, s_new = _ssd(xbc, z, dt, None, None, conv_w[l], conv_b[l], dt_bias[l], a_log[l],
                                   d_skip[l], ssd_norm[l], pb, seq)
        y_att = _moba_prompt(q, pk, pv, bias_tiles, cfar, pb, seq, l)
        mk, mv = _norm_proj(mem2d, norm_mem[l], lw["wkv"], min(tm, pb * MEM_LEN))
        mk = mk.reshape(1, pb, MEM_LEN, CA_HEADS, CA_HEAD_DIM)
        mv = mv.reshape(1, pb, MEM_LEN, CA_HEADS, CA_HEAD_DIM)
        h = _mix_ca(h, y_ssd, y_att, mk, mv, lw["wo1"], lw["wo2"], norm_ca[l], lw["wq"], lw["wco"], seq, tm)
        h = _mlp(h, norm_mlp[l], lw["wu"], lw["wd"], norm_final, last, tm)
        outs["pmk"].append(mk[0])
        outs["pmv"].append(mv[0])
        outs["pconv"].append(c_new)
        outs["pssm"].append(s_new)

        ms = sb * dseq
        z, xbc, dt, q, k, v = _norm_proj(g, norm_mix[l], lw["w_in"], ms)
        y_ssd, c_new, s_new = _ssd(xbc, z, dt, state_conv, state_ssm, conv_w[l], conv_b[l], dt_bias[l],
                                   a_log[l], d_skip[l], ssd_norm[l], sb, dseq, layer=l)
        y_att = _sample_moba(l, pad_rows(q, ROWS8), pad_rows(k, LANES), pad_rows(v, LANES), page_table,
                             cache_kt, cache_vt, bias_tiles, cfar, dseq)[:, :dseq].reshape(ms, ATT_WIDTH)
        g = _mix_ca(g, y_ssd, y_att, cache_mem_k, cache_mem_v, lw["wo1"], lw["wo2"], norm_ca[l],
                    lw["wq"], lw["wco"], dseq, min(32, ms), layer=l)
        g = _mlp(g, norm_mlp[l], lw["wu"], lw["wd"], norm_final, last, ms)
        outs["sk"].append(k.reshape(sb, dseq, ATT_HEADS, ATT_HEAD_DIM))
        outs["sv"].append(v.reshape(sb, dseq, ATT_HEADS, ATT_HEAD_DIM))
        outs["sconv"].append(c_new)
        outs["sssm"].append(s_new)

    st = {n: jnp.stack(vs) for n, vs in outs.items() if vs}
    def token_major(a):
        return jnp.transpose(a.reshape(depth, pb, ATT_HEADS, ATT_HEAD_DIM, seq), (0, 1, 4, 2, 3))

    return (h.reshape(pb, seq, D_MODEL), g.reshape(sb, dseq, D_MODEL), token_major(pk), token_major(pv),
            st["pmk"], st["pmv"],
            st["pconv"], st["pssm"], st["sk"], st["sv"], st["sconv"], st["sssm"])
```

```python
import functools
import math

import numpy as np
import jax
import jax.numpy as jnp
from jax import lax
from jax.experimental import pallas as pl
from jax.experimental.pallas import tpu as pltpu

D_MODEL = 1024
SSD_HEADS = 16
SSD_HEAD_DIM = 64
SSD_WIDTH = SSD_HEADS * SSD_HEAD_DIM
SSD_GROUPS = 2
SSD_STATE = 128
SSD_CHUNK = 128
CONV_WIDTH = 4
CONV_CH = SSD_WIDTH + 2 * SSD_GROUPS * SSD_STATE
ATT_HEADS = 8
ATT_HEAD_DIM = 64
ATT_WIDTH = ATT_HEADS * ATT_HEAD_DIM
MOBA_BLOCK = 256
MOBA_TOPK = 3
REL_BUCKETS = 32
REL_MAX_DIST = 128
MEM_LEN = 256
CA_HEADS = 4
CA_HEAD_DIM = 128
CA_WIDTH = CA_HEADS * CA_HEAD_DIM
FF = 4 * D_MODEL
EPS = 1e-5

LANES = 128
VMEM_LIMIT = 56 * 1024 * 1024
NEG = -1e30
LOG2E = math.log2(math.e)

BF16 = jnp.bfloat16
F32 = jnp.float32
HI = lax.Precision.HIGHEST


def _params(*sem):
    return pltpu.CompilerParams(dimension_semantics=sem, vmem_limit_bytes=VMEM_LIMIT)


def _rms(x, gain):
    return x * lax.rsqrt(jnp.mean(x * x, axis=-1, keepdims=True) + EPS) * gain


def _dot(a, b):
    return jnp.dot(a.astype(BF16), b.astype(BF16), preferred_element_type=F32)


def _dot_t(a, b):
    return lax.dot_general(a.astype(BF16), b.astype(BF16), (((1,), (1,)), ((), ())),
                           preferred_element_type=F32)


def _full(shape):
    return pl.BlockSpec(shape, lambda *_: (0,) * len(shape))


def _norm_proj_body(n_out, transposed, x_ref, g_ref, *refs):
    w_refs, o_refs = refs[:n_out], refs[len(refs) - n_out:]
    xn = _rms(x_ref[...], g_ref[...]).astype(BF16)
    for idx, (w_ref, o_ref) in enumerate(zip(w_refs, o_refs)):
        if idx in transposed:
            o_ref[...] = lax.dot_general(w_ref[...], xn, (((1,), (1,)), ((), ())), preferred_element_type=F32)
        else:
            o_ref[...] = jnp.dot(xn, w_ref[...], preferred_element_type=F32)


def _norm_proj(x, gain, weights, tm, transposed=(), layer=0, depth=1, seq=None, prev=None):
    m, d = x.shape
    n_out = len(weights)
    out_specs, out_shape = [], []
    for idx, w in enumerate(weights):
        if idx in transposed:
            n = w.shape[0]
            per = seq // tm
            out_specs.append(pl.BlockSpec((None, None, n, tm), lambda i: (layer, i // per, 0, i % per)))
            out_shape.append(jax.ShapeDtypeStruct((depth, m // seq, n, seq), F32))
        else:
            n = w.shape[1]
            out_specs.append(pl.BlockSpec((tm, n), lambda i: (i, 0)))
            out_shape.append(jax.ShapeDtypeStruct((m, n), F32))
    prev = list(prev) if prev is not None else []
    n_in = 2 + n_out
    aliases = {n_in + k: idx for k, idx in enumerate(transposed)} if prev else {}
    return pl.pallas_call(
        functools.partial(_norm_proj_body, n_out, tuple(transposed)),
        grid=(m // tm,),
        in_specs=[pl.BlockSpec((tm, d), lambda i: (i, 0)), _full((1, d))]
        + [_full(w.shape) for w in weights] + [pl.BlockSpec(memory_space=pl.ANY) for _ in prev],
        out_specs=out_specs,
        out_shape=out_shape,
        input_output_aliases=aliases,
        compiler_params=_params("parallel"),
        name="norm_proj",
    )(x, gain.reshape(1, d), *weights, *prev)


CONV_PAD = 8


def _ssd_body(real_len, has_init, xbc_ref, z_ref, dt_ref, cs_ref, s0_ref, cw_ref, cb_ref, dtb_ref,
              alog_ref, dsk_ref, nw_ref, exp_ref, y_ref, cso_ref, so_ref, xpad_ref, st_ref):
    c = pl.program_id(1)
    nc = pl.num_programs(1)
    q = SSD_CHUNK
    n_pairs = SSD_HEADS // 2

    @pl.when(c == 0)
    def _():
        if has_init:
            xpad_ref[pl.ds(0, CONV_PAD), :] = jnp.zeros((CONV_PAD, CONV_CH), F32)
            xpad_ref[pl.ds(CONV_PAD - (CONV_WIDTH - 1), CONV_WIDTH - 1), :] = cs_ref[...]
            for p in range(n_pairs):
                st_ref[:, p * LANES:(p + 1) * LANES] = s0_ref[pl.ds(p * LANES, LANES), :].T
        else:
            xpad_ref[pl.ds(0, CONV_PAD), :] = jnp.zeros((CONV_PAD, CONV_CH), F32)
            st_ref[...] = jnp.zeros_like(st_ref)

    xpad_ref[pl.ds(CONV_PAD, q), :] = xbc_ref[...]

    acc = cb_ref[...] + xpad_ref[pl.ds(CONV_PAD - 3, q), :] * cw_ref[0:1, :]
    acc = acc + xpad_ref[pl.ds(CONV_PAD - 2, q), :] * cw_ref[1:2, :]
    acc = acc + xpad_ref[pl.ds(CONV_PAD - 1, q), :] * cw_ref[2:3, :]
    acc = acc + xpad_ref[pl.ds(CONV_PAD, q), :] * cw_ref[3:4, :]
    xc = acc * (1.0 / (1.0 + jnp.exp(-acc)))
    new_tail = xpad_ref[pl.ds(CONV_PAD + real_len - (CONV_WIDTH - 1), CONV_WIDTH - 1), :]

    @pl.when(c == nc - 1)
    def _():
        cso_ref[...] = new_tail

    xpad_ref[pl.ds(CONV_PAD - (CONV_WIDTH - 1), CONV_WIDTH - 1), :] = new_tail

    xs = xc[:, :SSD_WIDTH]
    gn = SSD_GROUPS * SSD_STATE

    dtx = dt_ref[...] + dtb_ref[...]
    dt = jnp.maximum(dtx, 0.0) + jnp.log1p(jnp.exp(-jnp.abs(dtx)))
    if real_len < q:
        row = lax.broadcasted_iota(jnp.int32, (q, LANES), 0)
        dt = jnp.where(row < real_len, dt, 0.0)
    a = -jnp.exp(alog_ref[...])
    da = dt * a
    ii = lax.broadcasted_iota(jnp.int32, (q, q), 0)
    jj = lax.broadcasted_iota(jnp.int32, (q, q), 1)
    causal = ii >= jj
    tril = jnp.where(causal, 1.0, 0.0).astype(F32)
    acum = jnp.dot(tril, da, preferred_element_type=F32, precision=HI)
    acum_t = acum.T
    dt_t = dt.T
    alast = acum[q - 1:q, :]
    ea = jnp.exp(acum)
    wdec = jnp.exp(alast - acum) * dt
    alast8 = jnp.broadcast_to(alast, (8, LANES))
    dec_e = jnp.exp(jnp.dot(alast8, exp_ref[...], preferred_element_type=F32, precision=HI)[0:1, :])

    lane = lax.broadcasted_iota(jnp.int32, (q, LANES), 1)
    first = lane < SSD_HEAD_DIM
    y_parts = []
    xw_parts = []
    for g in range(SSD_GROUPS):
        bg = xc[:, SSD_WIDTH + g * SSD_STATE:SSD_WIDTH + (g + 1) * SSD_STATE]
        cg = xc[:, SSD_WIDTH + gn + g * SSD_STATE:SSD_WIDTH + gn + (g + 1) * SSD_STATE]
        bg_t = bg.T
        scores = _dot(cg, bg_t)
        per_group = SSD_HEADS // SSD_GROUPS
        for pp in range(per_group // 2):
            p = g * (per_group // 2) + pp
            x_pair = xs[:, p * LANES:(p + 1) * LANES]
            st_pair = st_ref[:, p * LANES:(p + 1) * LANES]
            rhs = jnp.concatenate([x_pair, st_pair], axis=0).astype(BF16)
            outs = []
            for k in range(2):
                h = 2 * p + k
                a_col = acum[:, h:h + 1]
                a_row = acum_t[h:h + 1, :]
                seg = jnp.where(causal, a_col - a_row, 0.0)
                m_h = jnp.where(causal, jnp.exp(seg), 0.0) * scores * dt_t[h:h + 1, :]
                c_h = cg * ea[:, h:h + 1]
                lhs = jnp.concatenate([m_h, c_h], axis=1).astype(BF16)
                outs.append(jnp.dot(lhs, rhs, preferred_element_type=F32))
            y_parts.append(jnp.where(first, outs[0], outs[1]))
            w_pair = jnp.where(first, wdec[:, 2 * p:2 * p + 1], wdec[:, 2 * p + 1:2 * p + 2])
            xw_parts.append(x_pair * w_pair)
        half = SSD_WIDTH // SSD_GROUPS
        xw_g = jnp.concatenate(xw_parts[-(per_group // 2):], axis=1)
        upd = _dot(bg_t, xw_g)
        st_ref[:, g * half:(g + 1) * half] = (
            st_ref[:, g * half:(g + 1) * half] * dec_e[:, g * half:(g + 1) * half] + upd)

    y = jnp.concatenate(y_parts, axis=1) + xs * dsk_ref[...]
    zz = z_ref[...]
    y = y * (zz * (1.0 / (1.0 + jnp.exp(-zz))))
    y_ref[...] = _rms(y, nw_ref[...])

    @pl.when(c == nc - 1)
    def _():
        for p in range(n_pairs):
            so_ref[pl.ds(p * LANES, LANES), :] = st_ref[:, p * LANES:(p + 1) * LANES].T


def _ssd(xbc, z, dt, conv_state, ssm_state, conv_w, conv_b, dt_bias, a_log, d_skip, norm_w, bsz, seq, layer=0):
    has_init = conv_state is not None
    real_len = min(seq, SSD_CHUNK)
    nc = max(seq // SSD_CHUNK, 1)
    if not has_init:
        conv_state = jnp.zeros((1, bsz, CONV_WIDTH - 1, CONV_CH), F32)
        ssm_state = jnp.zeros((1, bsz, SSD_HEADS, SSD_HEAD_DIM, SSD_STATE), F32)
    s0 = ssm_state.reshape(ssm_state.shape[0], bsz, SSD_WIDTH, SSD_STATE)
    pad_h = LANES - SSD_HEADS
    dtb = jnp.pad(dt_bias, (0, pad_h)).reshape(1, LANES)
    alog = jnp.pad(a_log, (0, pad_h)).reshape(1, LANES)
    dsk = jnp.repeat(d_skip, SSD_HEAD_DIM).reshape(1, SSD_WIDTH)
    expand = (np.arange(LANES)[:, None] == (np.arange(SSD_WIDTH) // SSD_HEAD_DIM)[None, :]).astype(np.float32)
    xbc3 = xbc.reshape(bsz, seq, CONV_CH)
    z3 = z.reshape(bsz, seq, SSD_WIDTH)
    dt3 = dt.reshape(bsz, seq, LANES)
    seq_p = nc * SSD_CHUNK
    if seq_p != seq:
        pad = ((0, 0), (0, seq_p - seq), (0, 0))
        xbc3, z3, dt3 = jnp.pad(xbc3, pad), jnp.pad(z3, pad), jnp.pad(dt3, pad)

    def tok(width):
        return pl.BlockSpec((None, SSD_CHUNK, width), lambda b, c: (b, c, 0))

    y, cso, so = pl.pallas_call(
        functools.partial(_ssd_body, real_len, has_init),
        grid=(bsz, nc),
        in_specs=[tok(CONV_CH), tok(SSD_WIDTH), tok(LANES),
                  pl.BlockSpec((None, None, CONV_WIDTH - 1, CONV_CH), lambda b, c: (layer, b, 0, 0)),
                  pl.BlockSpec((None, None, SSD_WIDTH, SSD_STATE), lambda b, c: (layer, b, 0, 0)),
                  _full((CONV_WIDTH, CONV_CH)), _full((1, CONV_CH)), _full((1, LANES)), _full((1, LANES)),
                  _full((1, SSD_WIDTH)), _full((1, SSD_WIDTH)), _full((LANES, SSD_WIDTH))],
        out_specs=[tok(SSD_WIDTH),
                   pl.BlockSpec((None, CONV_WIDTH - 1, CONV_CH), lambda b, c: (b, 0, 0)),
                   pl.BlockSpec((None, SSD_WIDTH, SSD_STATE), lambda b, c: (b, 0, 0))],
        out_shape=[jax.ShapeDtypeStruct((bsz, seq_p, SSD_WIDTH), F32),
                   jax.ShapeDtypeStruct((bsz, CONV_WIDTH - 1, CONV_CH), F32),
                   jax.ShapeDtypeStruct((bsz, SSD_WIDTH, SSD_STATE), F32)],
        scratch_shapes=[pltpu.VMEM((CONV_PAD + SSD_CHUNK, CONV_CH), F32),
                        pltpu.VMEM((SSD_STATE, SSD_WIDTH), F32)],
        compiler_params=_params("parallel", "arbitrary"),
        name="ssd_scan",
    )(xbc3, z3, dt3, conv_state, s0, conv_w, conv_b.reshape(1, CONV_CH), dtb, alog, dsk,
      norm_w.reshape(1, SSD_WIDTH), jnp.asarray(expand))
    return (y[:, :seq].reshape(bsz * seq, SSD_WIDTH), cso,
            so.reshape(bsz, SSD_HEADS, SSD_HEAD_DIM, SSD_STATE))


def _t5_bucket_np(rel):
    n = np.maximum(rel, 0)
    max_exact = REL_BUCKETS // 2
    nf = np.maximum(n, 1).astype(np.float32)
    large = max_exact + (np.log(nf / np.float32(max_exact)) / np.float32(math.log(REL_MAX_DIST / max_exact))
                         * np.float32(REL_BUCKETS - max_exact)).astype(np.int32)
    large = np.minimum(large, REL_BUCKETS - 1)
    return np.where(n < max_exact, n, large).astype(np.int32)


def _bias_tiles_body(bk_ref, rb_ref, o_ref):
    h = pl.program_id(0)
    ii = lax.broadcasted_iota(jnp.int32, (MOBA_BLOCK, MOBA_BLOCK), 0)
    jj = lax.broadcasted_iota(jnp.int32, (MOBA_BLOCK, MOBA_BLOCK), 1)
    for t in range(2):
        bk = bk_ref[t]
        acc = jnp.zeros((MOBA_BLOCK, MOBA_BLOCK), F32)
        for u in range(REL_BUCKETS):
            acc = jnp.where(bk == u, rb_ref[u, h], acc)
        if t == 0:
            acc = jnp.where(jj >= ii, acc, NEG)
        o_ref[t] = acc


def _bias_tiles(rel_bias):
    i = np.arange(MOBA_BLOCK)[None, :]
    j = np.arange(MOBA_BLOCK)[:, None]
    buckets = np.stack([_t5_bucket_np(i - j), _t5_bucket_np(MOBA_BLOCK + i - j)])
    return pl.pallas_call(
        _bias_tiles_body,
        grid=(ATT_HEADS,),
        in_specs=[_full((2, MOBA_BLOCK, MOBA_BLOCK)),
                  pl.BlockSpec(memory_space=pltpu.SMEM)],
        out_specs=pl.BlockSpec((None, 2, MOBA_BLOCK, MOBA_BLOCK), lambda h: (h, 0, 0, 0)),
        out_shape=jax.ShapeDtypeStruct((ATT_HEADS, 2, MOBA_BLOCK, MOBA_BLOCK), F32),
        compiler_params=_params("parallel"),
        name="bias_tiles",
    )(jnp.asarray(buckets), rel_bias)


def _block_rank(g, blk):
    rank = jnp.zeros(g.shape, jnp.int32)
    for m in range(g.shape[1]):
        gm = g[:, m:m + 1]
        rank = rank + ((gm > g) | ((gm == g) & (m < blk))).astype(jnp.int32)
    return rank


def _block_rank_t(g, blk):
    rank = jnp.zeros(g.shape, jnp.int32)
    for m in range(g.shape[0]):
        gm = g[m:m + 1, :]
        rank = rank + ((gm > g) | ((gm == g) & (m < blk))).astype(jnp.int32)
    return rank


def _moba_prompt_body(nblk, q_ref, k_ref, v_ref, bias_ref, cfar_ref, o_ref, km_ref, k16_ref, vt_ref, qt_ref,
                      b2_ref):
    pair = pl.program_id(0)
    blk_rows = MOBA_BLOCK
    hd = ATT_HEAD_DIM
    scale = hd ** -0.5 * LOG2E
    for kk in range(2):
        for t in range(2):
            b2_ref[kk, t] = bias_ref[kk, t] * LOG2E
    per_blk = blk_rows // LANES
    for n in range(nblk):
        tot = None
        for c in range(n * per_blk, (n + 1) * per_blk):
            kb = k_ref[:, c * LANES:(c + 1) * LANES].T
            k16_ref[pl.ds(c * LANES, LANES), :] = kb.astype(BF16)
            part = jnp.sum(kb, axis=0, keepdims=True)
            tot = part if tot is None else tot + part
        km_ref[n:n + 1, :] = tot * (1.0 / blk_rows)
    vt_ref[...] = v_ref[...].astype(BF16)
    for c in range(nblk * per_blk):
        qt_ref[:, c * LANES:(c + 1) * LANES] = (q_ref[pl.ds(c * LANES, LANES), :] * scale).T.astype(BF16)
    kmean = km_ref[...].astype(BF16)
    row = lax.broadcasted_iota(jnp.int32, (LANES, blk_rows), 0)
    blk = lax.broadcasted_iota(jnp.int32, (nblk, blk_rows), 0)
    zero16 = jnp.zeros((LANES, blk_rows), BF16)

    for i in range(nblk):
        qt = qt_ref[:, i * blk_rows:(i + 1) * blk_rows]
        n_keys = (i + 1) * blk_rows
        halves = []
        for kk in range(2):
            qth = jnp.where(row < hd, qt, zero16) if kk == 0 else jnp.where(row >= hd, qt, zero16)
            gate = jnp.dot(kmean, qth, preferred_element_type=F32)
            past = blk < i
            gate = jnp.where(past, gate, NEG)
            sel = past & (_block_rank_t(gate, blk) < MOBA_TOPK)
            pen = jnp.where(sel, cfar_ref[2 * pair + kk] * LOG2E, NEG)
            pen_prev = jnp.where(sel[max(i - 1, 0):max(i - 1, 0) + 1, :], 0.0, NEG)

            def logits(n):
                seg = jnp.dot(k16_ref[n * blk_rows:(n + 1) * blk_rows, :], qth, preferred_element_type=F32)
                if n == i:
                    return seg + b2_ref[kk, 0]
                if n == i - 1:
                    return seg + b2_ref[kk, 1] + pen_prev
                return seg + pen[n:n + 1, :]

            m = logits(0)
            for n in range(1, i + 1):
                m = jnp.maximum(m, logits(n))
            m = jnp.max(m, axis=0, keepdims=True)
            tot = None
            acc = None
            for n in range(i + 1):
                p = jnp.exp2(logits(n) - m)
                tot = p if tot is None else tot + p
                pv = jnp.dot(vt_ref[kk * hd:(kk + 1) * hd, n * blk_rows:(n + 1) * blk_rows], p.astype(BF16),
                             preferred_element_type=F32)
                acc = pv if acc is None else acc + pv
            halves.append(acc / jnp.sum(tot, axis=0, keepdims=True))
        o_ref[pl.ds(i * blk_rows, blk_rows), :] = jnp.concatenate(halves, axis=0).T


def _moba_prompt(q, k, v, bias_tiles, cfar, bsz, seq, layer):
    nblk = seq // MOBA_BLOCK
    n_pairs = ATT_HEADS // 2
    tok = pl.BlockSpec((seq, LANES), lambda p, b: (b, p))
    tok_l = pl.BlockSpec((None, None, LANES, seq), lambda p, b: (layer, b, p, 0))
    return pl.pallas_call(
        functools.partial(_moba_prompt_body, nblk),
        grid=(n_pairs, bsz),
        in_specs=[tok, tok_l, tok_l,
                  pl.BlockSpec((2, 2, MOBA_BLOCK, MOBA_BLOCK), lambda p, b: (p, 0, 0, 0)),
                  pl.BlockSpec(memory_space=pltpu.SMEM)],
        out_specs=tok,
        out_shape=jax.ShapeDtypeStruct((bsz * seq, ATT_WIDTH), F32),
        scratch_shapes=[pltpu.VMEM((nblk, LANES), F32), pltpu.VMEM((seq, LANES), BF16),
                        pltpu.VMEM((LANES, seq), BF16), pltpu.VMEM((LANES, seq), BF16),
                        pltpu.VMEM((2, 2, MOBA_BLOCK, MOBA_BLOCK), F32)],
        compiler_params=_params("parallel", "parallel"),
        name="moba_prompt",
    )(q, k, v, bias_tiles, cfar)


PAGES_PER_STEP = 16
PAGE_SLOTS = 4
PAGE_LOOKAHEAD = 3
ROWS8 = 8


def _sample_moba_body(layer, n_new, n_k, page_size, pt_ref, q_ref, kn_ref, vn_ref, tprev_ref, town_ref, cfar_ref,
                      ck_ref, cv_ref, o_ref, buf_ref, sem_ref, s_ref, acc_ref, l_ref, idxv_ref, idxs_ref, isem,
                      vbuf_ref, vsem):
    b = pl.program_id(0)
    n_seq = pl.num_programs(0)
    n_chunks = n_k
    pages_per_blk = MOBA_BLOCK // page_size
    n_tok = s_ref.shape[2]
    n_blk = n_tok // MOBA_BLOCK
    q = q_ref[...]
    qh = [q[:, h * ATT_HEAD_DIM:(h + 1) * ATT_HEAD_DIM] for h in range(ATT_HEADS)]
    scale = ATT_HEAD_DIM ** -0.5

    assert n_chunks % PAGE_SLOTS == 0 and PAGE_LOOKAHEAD < PAGE_SLOTS

    def chunk_copies(seq, g):
        slot = g % PAGE_SLOTS
        return [pltpu.make_async_copy(ck_ref.at[layer, pt_ref[seq, g * PAGES_PER_STEP + r]],
                                      buf_ref.at[slot, r], sem_ref.at[slot])
                for r in range(PAGES_PER_STEP)]

    def start_chunk(seq, g):
        for c in chunk_copies(seq, g):
            c.start()

    def advance(g):
        for c in chunk_copies(b, g):
            c.wait()
        nxt = g + PAGE_LOOKAHEAD
        if nxt < n_chunks:
            start_chunk(b, nxt)
        else:
            @pl.when(b + 1 < n_seq)
            def _():
                start_chunk(b + 1, nxt - n_chunks)

    @pl.when(b == 0)
    def _():
        for g in range(PAGE_LOOKAHEAD):
            start_chunk(b, g)

    for g in range(n_k):
        advance(g)
        for r in range(PAGES_PER_STEP):
            start = (g * PAGES_PER_STEP + r) * page_size
            for h in range(ATT_HEADS):
                s_ref[h, :, start:start + page_size] = _dot(qh[h], buf_ref[g % PAGE_SLOTS, r, h])

    blk = lax.broadcasted_iota(jnp.int32, (ROWS8, n_blk), 1)
    col = lax.broadcasted_iota(jnp.int32, (ROWS8, LANES), 1)
    rowi = lax.broadcasted_iota(jnp.int32, (ROWS8, LANES), 0)
    halves_per_blk = MOBA_BLOCK // LANES

    def head_chunks(h):
        s_all = s_ref[h]
        return [s_all[:, c * LANES:(c + 1) * LANES] for c in range(n_tok // LANES)]

    sels = []
    for h in range(ATT_HEADS):
        chunks = head_chunks(h)
        gate = jnp.zeros((ROWS8, n_blk), F32)
        for n in range(n_blk):
            part = chunks[n * halves_per_blk]
            for c in chunks[n * halves_per_blk + 1:(n + 1) * halves_per_blk]:
                part = part + c
            gate = jnp.where(blk == n, jnp.sum(part, axis=1, keepdims=True) * (1.0 / MOBA_BLOCK), gate)
        rank = _block_rank(gate, blk)
        sels.append(rank < MOBA_TOPK)
        ids = jnp.zeros((ROWS8, LANES), jnp.int32)
        for s in range(MOBA_TOPK):
            ids = jnp.where(col == s, jnp.sum(jnp.where(rank == s, blk, 0), axis=1, keepdims=True), ids)
        idxv_ref[h] = ids
    ids_copy = pltpu.make_async_copy(idxv_ref, idxs_ref, isem)
    ids_copy.start()
    ids_copy.wait()

    def value_copies():
        out = []
        for h in range(ATT_HEADS):
            for t in range(n_new):
                for s in range(MOBA_TOPK):
                    first_page = idxs_ref[h, t, s] * pages_per_blk
                    for r in range(pages_per_blk):
                        out.append(pltpu.make_async_copy(cv_ref.at[layer, pt_ref[b, first_page + r], h],
                                                         vbuf_ref.at[h, t * MOBA_TOPK + s, r], vsem))
        return out

    for c in value_copies():
        c.start()

    def softmax_over_selected():
        for h in range(ATT_HEADS):
            chunks = head_chunks(h)
            sel = sels[h]
            pen = jnp.where(sel, cfar_ref[h], NEG)
            pen_last = jnp.where(sel[:, n_blk - 1:n_blk], 0.0, NEG)
            logits = []
            for c, x in enumerate(chunks):
                n = c // halves_per_blk
                if n == n_blk - 1:
                    off = (c % halves_per_blk) * LANES
                    logits.append(x * scale + tprev_ref[h, :, off:off + LANES] + pen_last)
                else:
                    logits.append(x * scale + pen[:, n:n + 1])
            kn_h = kn_ref[:, h * ATT_HEAD_DIM:(h + 1) * ATT_HEAD_DIM]
            vn_h = vn_ref[:, h * ATT_HEAD_DIM:(h + 1) * ATT_HEAD_DIM]
            s_own = jnp.where(col < n_new, _dot_t(qh[h] * scale, kn_h) + town_ref[h], NEG)
            m_el = s_own
            for x in logits:
                m_el = jnp.maximum(m_el, x)
            m = jnp.max(m_el, axis=1, keepdims=True)
            p_own = jnp.exp(s_own - m)
            tot = p_own
            for c, x in enumerate(logits):
                p = jnp.exp(x - m)
                s_ref[h, :, c * LANES:(c + 1) * LANES] = p
                tot = tot + p
            acc_ref[h] = _dot(p_own, vn_h)
            l_ref[h] = jnp.broadcast_to(jnp.sum(tot, axis=1, keepdims=True), (ROWS8, LANES))

    softmax_over_selected()

    for c in value_copies():
        c.wait()

    outs = []
    for h in range(ATT_HEADS):
        acc = acc_ref[h]
        for t in range(n_new):
            for s in range(MOBA_TOPK):
                first_tok = idxs_ref[h, t, s] * MOBA_BLOCK
                for r in range(pages_per_blk):
                    start = pl.multiple_of(first_tok + r * page_size, page_size)
                    p = jnp.where(rowi == t, s_ref[h, :, pl.ds(start, page_size)], 0.0)
                    acc = acc + _dot_t(p, vbuf_ref[h, t * MOBA_TOPK + s, r])
        outs.append(acc / l_ref[h][:, :ATT_HEAD_DIM])
    o_ref[...] = jnp.concatenate(outs, axis=1)


def _sample_moba(layer, q8, kn, vn, page_table, cache_kt, cache_vt, bias_tiles, cfar, n_new):
    bsz = q8.shape[0]
    page_size = cache_kt.shape[4]
    n_pages = page_table.shape[1]
    n_k = n_pages // PAGES_PER_STEP
    tprev = jnp.transpose(bias_tiles[:, 1, :, :ROWS8], (0, 2, 1))
    town = jnp.transpose(bias_tiles[:, 0, :LANES, :ROWS8], (0, 2, 1))
    grid_spec = pltpu.PrefetchScalarGridSpec(
        num_scalar_prefetch=1,
        grid=(bsz,),
        in_specs=[pl.BlockSpec((None, ROWS8, ATT_WIDTH), lambda b, pt: (b, 0, 0)),
                  pl.BlockSpec((None, LANES, ATT_WIDTH), lambda b, pt: (b, 0, 0)),
                  pl.BlockSpec((None, LANES, ATT_WIDTH), lambda b, pt: (b, 0, 0)),
                  pl.BlockSpec((ATT_HEADS, ROWS8, MOBA_BLOCK), lambda b, pt: (0, 0, 0)),
                  pl.BlockSpec((ATT_HEADS, ROWS8, LANES), lambda b, pt: (0, 0, 0)),
                  pl.BlockSpec(memory_space=pltpu.SMEM),
                  pl.BlockSpec(memory_space=pl.ANY),
                  pl.BlockSpec(memory_space=pl.ANY)],
        out_specs=pl.BlockSpec((None, ROWS8, ATT_WIDTH), lambda b, pt: (b, 0, 0)),
        scratch_shapes=[pltpu.VMEM((PAGE_SLOTS, PAGES_PER_STEP, ATT_HEADS, ATT_HEAD_DIM, page_size), F32),
                        pltpu.SemaphoreType.DMA((PAGE_SLOTS,)),
                        pltpu.VMEM((ATT_HEADS, ROWS8, n_pages * page_size), F32),
                        pltpu.VMEM((ATT_HEADS, ROWS8, ATT_HEAD_DIM), F32),
                        pltpu.VMEM((ATT_HEADS, ROWS8, LANES), F32),
                        pltpu.VMEM((ATT_HEADS, ROWS8, LANES), jnp.int32),
                        pltpu.SMEM((ATT_HEADS, ROWS8, LANES), jnp.int32),
                        pltpu.SemaphoreType.DMA(()),
                        pltpu.VMEM((ATT_HEADS, n_new * MOBA_TOPK, MOBA_BLOCK // page_size, ATT_HEAD_DIM, page_size),
                                   F32),
                        pltpu.SemaphoreType.DMA(())],
    )
    return pl.pallas_call(
        functools.partial(_sample_moba_body, layer, n_new, n_k, page_size),
        grid_spec=grid_spec,
        out_shape=jax.ShapeDtypeStruct((bsz, ROWS8, ATT_WIDTH), F32),
        compiler_params=_params("arbitrary"),
        name="sample_moba",
    )(page_table, q8, kn, vn, tprev, town, cfar, cache_kt, cache_vt)


def _mix_ca_body(rows_per_batch, h_ref, ys_ref, ya_ref, mk_ref, mv_ref, wo1_ref, wo2_ref, g_ref,
                 wq_ref, wco_ref, o_ref):
    tm = h_ref.shape[0]
    h1 = (h_ref[...] + jnp.dot(ys_ref[...].astype(BF16), wo1_ref[...], preferred_element_type=F32)
          + jnp.dot(ya_ref[...].astype(BF16), wo2_ref[...], preferred_element_type=F32))
    hn = _rms(h1, g_ref[...]).astype(BF16)
    qq = jnp.dot(hn, wq_ref[...], preferred_element_type=F32) * (CA_HEAD_DIM ** -0.5)
    nkeys = mk_ref.shape[0] * mk_ref.shape[1]

    def head_rows(ref, hd):
        if len(ref.shape) == 3:
            return ref[:, :, hd * CA_HEAD_DIM:(hd + 1) * CA_HEAD_DIM].reshape(nkeys, CA_HEAD_DIM)
        return ref[:, :, hd, :].reshape(nkeys, CA_HEAD_DIM)

    if rows_per_batch < tm:
        r = lax.broadcasted_iota(jnp.int32, (tm, nkeys), 0) // rows_per_batch
        c = lax.broadcasted_iota(jnp.int32, (tm, nkeys), 1) // MEM_LEN
        same = r == c
    outs = []
    for hd in range(CA_HEADS):
        sl = slice(hd * CA_HEAD_DIM, (hd + 1) * CA_HEAD_DIM)
        s = _dot_t(qq[:, sl], head_rows(mk_ref, hd))
        if rows_per_batch < tm:
            s = jnp.where(same, s, NEG)
        s = s - jnp.max(s, axis=1, keepdims=True)
        p = jnp.exp(s)
        p = p / jnp.sum(p, axis=1, keepdims=True)
        outs.append(_dot(p, head_rows(mv_ref, hd)))
    o = jnp.concatenate(outs, axis=1).astype(BF16)
    o_ref[...] = h1 + jnp.dot(o, wco_ref[...], preferred_element_type=F32)


def _mix_ca(h, ys, ya, mk, mv, wo1, wo2, g_ca, wq, wco, rows_per_batch, tm, layer=0):
    m = h.shape[0]
    nbat = max(tm // rows_per_batch, 1)
    per = rows_per_batch // tm if rows_per_batch >= tm else 1
    if mk.ndim == 3:
        mem_spec = pl.BlockSpec((nbat, MEM_LEN, CA_WIDTH), lambda i: (i // per, 0, 0))
    else:
        mem_spec = pl.BlockSpec((None, nbat, MEM_LEN, CA_HEADS, CA_HEAD_DIM), lambda i: (layer, i // per, 0, 0, 0))

    def row(width):
        return pl.BlockSpec((tm, width), lambda i: (i, 0))

    return pl.pallas_call(
        functools.partial(_mix_ca_body, rows_per_batch),
        grid=(m // tm,),
        in_specs=[row(D_MODEL), row(SSD_WIDTH), row(ATT_WIDTH),
                  mem_spec, mem_spec,
                  _full(wo1.shape), _full(wo2.shape), _full((1, D_MODEL)), _full(wq.shape), _full(wco.shape)],
        out_specs=row(D_MODEL),
        out_shape=jax.ShapeDtypeStruct((m, D_MODEL), F32),
        compiler_params=_params("parallel"),
        name="mix_ca",
    )(h, ys, ya, mk, mv, wo1, wo2, g_ca.reshape(1, D_MODEL), wq, wco)


FF_CHUNK = 1024


def _mlp_body(final, h_ref, g_ref, wu_ref, wd_ref, gf_ref, o_ref):
    h = h_ref[...]
    xn = _rms(h, g_ref[...]).astype(BF16)
    acc = h
    for c in range(FF // FF_CHUNK):
        u = jnp.dot(xn, wu_ref[:, c * FF_CHUNK:(c + 1) * FF_CHUNK], preferred_element_type=F32)
        u = jnp.maximum(u, 0.0)
        acc = acc + jnp.dot((u * u).astype(BF16), wd_ref[c * FF_CHUNK:(c + 1) * FF_CHUNK, :],
                            preferred_element_type=F32)
    o_ref[...] = _rms(acc, gf_ref[...]) if final else acc


def _mlp(h, g, wu, wd, g_final, final, tm):
    m = h.shape[0]
    row = pl.BlockSpec((tm, D_MODEL), lambda i: (i, 0))
    return pl.pallas_call(
        functools.partial(_mlp_body, final),
        grid=(m // tm,),
        in_specs=[row, _full((1, D_MODEL)), _full(wu.shape), _full(wd.shape), _full((1, D_MODEL))],
        out_specs=row,
        out_shape=jax.ShapeDtypeStruct((m, D_MODEL), F32),
        compiler_params=_params("parallel"),
        name="mlp",
    )(h, g.reshape(1, D_MODEL), wu, wd, g_final.reshape(1, D_MODEL))


def _layer_weights(l, w_in, w_out, w_ca_q, w_ca_kv, w_ca_o, w_up, w_down):
    w = w_in[l]
    o = 0
    parts = []
    for width in (SSD_WIDTH, CONV_CH, SSD_HEADS, ATT_WIDTH, ATT_WIDTH, ATT_WIDTH):
        parts.append(w[:, o:o + width])
        o += width
    parts[2] = jnp.pad(parts[2], ((0, 0), (0, LANES - SSD_HEADS)))
    wo = w_out[l]
    w_in = [p.astype(BF16) for p in parts]
    return dict(
        w_in=w_in,
        w_in_prompt=w_in[:4] + [w_in[4].T, w_in[5].T],
        wo1=wo[:SSD_WIDTH].astype(BF16), wo2=wo[SSD_WIDTH:].astype(BF16),
        wq=w_ca_q[l].astype(BF16), wco=w_ca_o[l].astype(BF16),
        wkv=[w_ca_kv[l][:, :CA_WIDTH].astype(BF16), w_ca_kv[l][:, CA_WIDTH:].astype(BF16)],
        wu=w_up[l].astype(BF16), wd=w_down[l].astype(BF16))


def kernel(x_prompt, x_sample, cache_k, cache_v, cache_mem_k, cache_mem_v, state_conv, state_ssm,
           page_table, mem_prompt, norm_mix, w_in, conv_w, conv_b, dt_bias, a_log, d_skip, ssd_norm,
           rel_bias, w_out, norm_ca, norm_mem, w_ca_q, w_ca_kv, w_ca_o, norm_mlp, w_up, w_down, norm_final):
    depth = w_in.shape[0]
    pb, seq, _ = x_prompt.shape
    sb, dseq, _ = x_sample.shape
    page_size = cache_k.shape[2]
    tm = 512

    bias_tiles = _bias_tiles(rel_bias)
    cfar = rel_bias[REL_BUCKETS - 1]
    cache_kt = jnp.transpose(cache_k, (0, 1, 3, 4, 2))
    cache_vt = jnp.transpose(cache_v, (0, 1, 3, 4, 2))
    mem2d = mem_prompt.reshape(pb * MEM_LEN, D_MODEL)

    def pad_rows(a, rows):
        return jnp.pad(a.reshape(sb, dseq, ATT_WIDTH), ((0, 0), (0, rows - dseq), (0, 0)))

    h = x_prompt.reshape(pb * seq, D_MODEL)
    g = x_sample.reshape(sb * dseq, D_MODEL)
    outs = {n: [] for n in ("pmk", "pmv", "pconv", "pssm", "sk", "sv", "sconv", "sssm")}
    pk = jnp.zeros((depth, pb, ATT_WIDTH, seq), F32)
    pv = jnp.zeros((depth, pb, ATT_WIDTH, seq), F32)
    for l in range(depth):
        lw = _layer_weights(l, w_in, w_out, w_ca_q, w_ca_kv, w_ca_o, w_up, w_down)
        last = l == depth - 1

        z, xbc, dt, q, pk, pv = _norm_proj(h, norm_mix[l], lw["w_in_prompt"], tm, transposed=(4, 5), layer=l,
                                           depth=depth, seq=seq, prev=[pk, pv])
        y_ssd, c_new, s_new = _ssd(xbc, z, dt, None, None, conv_w[l], conv_b[l], dt_bias[l], a_log[l],
                                   d_skip[l], ssd_norm[l], pb, seq)
        y_att = _moba_prompt(q, pk, pv, bias_tiles, cfar, pb, seq, l)
        mk, mv = _norm_proj(mem2d, norm_mem[l], lw["wkv"], min(tm, pb * MEM_LEN))
        h = _mix_ca(h, y_ssd, y_att, mk.reshape(pb, MEM_LEN, CA_WIDTH), mv.reshape(pb, MEM_LEN, CA_WIDTH),
                    lw["wo1"], lw["wo2"], norm_ca[l], lw["wq"], lw["wco"], seq, tm)
        h = _mlp(h, norm_mlp[l], lw["wu"], lw["wd"], norm_final, last, tm)
        outs["pmk"].append(mk.reshape(pb, MEM_LEN, CA_HEADS, CA_HEAD_DIM))
        outs["pmv"].append(mv.reshape(pb, MEM_LEN, CA_HEADS, CA_HEAD_DIM))
        outs["pconv"].append(c_new)
        outs["pssm"].append(s_new)

        ms = sb * dseq
        z, xbc, dt, q, k, v = _norm_proj(g, norm_mix[l], lw["w_in"], ms)
        y_ssd, c_new, s_new = _ssd(xbc, z, dt, state_conv, state_ssm, conv_w[l], conv_b[l], dt_bias[l],
                                   a_log[l], d_skip[l], ssd_norm[l], sb, dseq, layer=l)
        y_att = _sample_moba(l, pad_rows(q, ROWS8), pad_rows(k, LANES), pad_rows(v, LANES), page_table,
                             cache_kt, cache_vt, bias_tiles, cfar, dseq)[:, :dseq].reshape(ms, ATT_WIDTH)
        g = _mix_ca(g, y_ssd, y_att, cache_mem_k, cache_mem_v, lw["wo1"], lw["wo2"], norm_ca[l],
                    lw["wq"], lw["wco"], dseq, min(32, ms), layer=l)
        g = _mlp(g, norm_mlp[l], lw["wu"], lw["wd"], norm_final, last, ms)
        outs["sk"].append(k.reshape(sb, dseq, ATT_HEADS, ATT_HEAD_DIM))
        outs["sv"].append(v.reshape(sb, dseq, ATT_HEADS, ATT_HEAD_DIM))
        outs["sconv"].append(c_new)
        outs["sssm"].append(s_new)

    st = {n: jnp.stack(vs) for n, vs in outs.items() if vs}
    def token_major(a):
        return jnp.transpose(a.reshape(depth, pb, ATT_HEADS, ATT_HEAD_DIM, seq), (0, 1, 4, 2, 3))

    return (h.reshape(pb, seq, D_MODEL), g.reshape(sb, dseq, D_MODEL), token_major(pk), token_major(pv),
            st["pmk"], st["pmv"],
            st["pconv"], st["pssm"], st["sk"], st["sv"], st["sconv"], st["sssm"])
```

```python
import functools
import math

import numpy as np
import jax
import jax.numpy as jnp
from jax import lax
from jax.experimental import pallas as pl
from jax.experimental.pallas import tpu as pltpu

D_MODEL = 1024
SSD_HEADS = 16
SSD_HEAD_DIM = 64
SSD_WIDTH = SSD_HEADS * SSD_HEAD_DIM
SSD_GROUPS = 2
SSD_STATE = 128
SSD_CHUNK = 128
CONV_WIDTH = 4
CONV_CH = SSD_WIDTH + 2 * SSD_GROUPS * SSD_STATE
ATT_HEADS = 8
ATT_HEAD_DIM = 64
ATT_WIDTH = ATT_HEADS * ATT_HEAD_DIM
MOBA_BLOCK = 256
MOBA_TOPK = 3
REL_BUCKETS = 32
REL_MAX_DIST = 128
MEM_LEN = 256
CA_HEADS = 4
CA_HEAD_DIM = 128
CA_WIDTH = CA_HEADS * CA_HEAD_DIM
FF = 4 * D_MODEL
EPS = 1e-5

LANES = 128
VMEM_LIMIT = 56 * 1024 * 1024
NEG = -1e30
LOG2E = math.log2(math.e)

BF16 = jnp.bfloat16
F32 = jnp.float32
HI = lax.Precision.HIGHEST


def _params(*sem):
    return pltpu.CompilerParams(dimension_semantics=sem, vmem_limit_bytes=VMEM_LIMIT)


def _rms(x, gain):
    return x * lax.rsqrt(jnp.mean(x * x, axis=-1, keepdims=True) + EPS) * gain


def _dot(a, b):
    return jnp.dot(a.astype(BF16), b.astype(BF16), preferred_element_type=F32)


def _dot_t(a, b):
    return lax.dot_general(a.astype(BF16), b.astype(BF16), (((1,), (1,)), ((), ())),
                           preferred_element_type=F32)


def _full(shape):
    return pl.BlockSpec(shape, lambda *_: (0,) * len(shape))


def _norm_proj_body(n_out, transposed, x_ref, g_ref, *refs):
    w_refs, o_refs = refs[:n_out], refs[len(refs) - n_out:]
    xn = _rms(x_ref[...], g_ref[...]).astype(BF16)
    for idx, (w_ref, o_ref) in enumerate(zip(w_refs, o_refs)):
        if idx in transposed:
            o_ref[...] = lax.dot_general(w_ref[...], xn, (((1,), (1,)), ((), ())), preferred_element_type=F32)
        else:
            o_ref[...] = jnp.dot(xn, w_ref[...], preferred_element_type=F32)


def _norm_proj(x, gain, weights, tm, transposed=(), layer=0, depth=1, seq=None, prev=None):
    m, d = x.shape
    n_out = len(weights)
    out_specs, out_shape = [], []
    for idx, w in enumerate(weights):
        if idx in transposed:
            n = w.shape[0]
            per = seq // tm
            out_specs.append(pl.BlockSpec((None, None, n, tm), lambda i: (layer, i // per, 0, i % per)))
            out_shape.append(jax.ShapeDtypeStruct((depth, m // seq, n, seq), F32))
        else:
            n = w.shape[1]
            out_specs.append(pl.BlockSpec((tm, n), lambda i: (i, 0)))
            out_shape.append(jax.ShapeDtypeStruct((m, n), F32))
    prev = list(prev) if prev is not None else []
    n_in = 2 + n_out
    aliases = {n_in + k: idx for k, idx in enumerate(transposed)} if prev else {}
    return pl.pallas_call(
        functools.partial(_norm_proj_body, n_out, tuple(transposed)),
        grid=(m // tm,),
        in_specs=[pl.BlockSpec((tm, d), lambda i: (i, 0)), _full((1, d))]
        + [_full(w.shape) for w in weights] + [pl.BlockSpec(memory_space=pl.ANY) for _ in prev],
        out_specs=out_specs,
        out_shape=out_shape,
        input_output_aliases=aliases,
        compiler_params=_params("parallel"),
        name="norm_proj",
    )(x, gain.reshape(1, d), *weights, *prev)


CONV_PAD = 8


def _ssd_body(real_len, has_init, xbc_ref, z_ref, dt_ref, cs_ref, s0_ref, cw_ref, cb_ref, dtb_ref,
              alog_ref, dsk_ref, nw_ref, exp_ref, y_ref, cso_ref, so_ref, xpad_ref, st_ref, xc_ref):
    c = pl.program_id(1)
    nc = pl.num_programs(1)
    q = SSD_CHUNK
    n_pairs = SSD_HEADS // 2

    @pl.when(c == 0)
    def _():
        if has_init:
            xpad_ref[...] = jnp.zeros((CONV_PAD, CONV_CH), F32)
            xpad_ref[pl.ds(CONV_PAD - (CONV_WIDTH - 1), CONV_WIDTH - 1), :] = cs_ref[...]
            for p in range(n_pairs):
                st_ref[:, p * LANES:(p + 1) * LANES] = s0_ref[pl.ds(p * LANES, LANES), :].T
        else:
            xpad_ref[...] = jnp.zeros((CONV_PAD, CONV_CH), F32)
            st_ref[...] = jnp.zeros_like(st_ref)

    first_row = lax.broadcasted_iota(jnp.int32, (CONV_PAD, LANES), 0) == 0

    def shift_down(a, carry):
        r = pltpu.roll(a, 1, axis=0)
        return jnp.concatenate([jnp.where(first_row, carry, r[:CONV_PAD]), r[CONV_PAD:]], axis=0)

    def conv_slab(j, carry):
        lanes = pl.ds(pl.multiple_of(j * LANES, LANES), LANES)
        x = xbc_ref[:, lanes]
        w = [cw_ref[k:k + 1, lanes] for k in range(CONV_WIDTH)]
        xm3, xm2, xm1 = (xpad_ref[CONV_PAD - 3 + k:CONV_PAD - 2 + k, lanes] for k in range(CONV_WIDTH - 1))
        acc = x * w[0]
        acc = x * w[1] + shift_down(acc, xm1 * w[0])
        acc = x * w[2] + shift_down(acc, xm1 * w[1] + xm2 * w[0])
        acc = cb_ref[:, lanes] + x * w[3] + shift_down(acc, xm1 * w[2] + xm2 * w[1] + xm3 * w[0])
        xc_ref[:, lanes] = acc * (1.0 / (1.0 + jnp.exp(-acc)))
        return carry

    lax.fori_loop(0, CONV_CH // LANES, conv_slab, 0)
    new_tail = xbc_ref[pl.ds(real_len - (CONV_WIDTH - 1), CONV_WIDTH - 1), :]

    @pl.when(c == nc - 1)
    def _():
        cso_ref[...] = new_tail

    xpad_ref[pl.ds(CONV_PAD - (CONV_WIDTH - 1), CONV_WIDTH - 1), :] = new_tail

    xs = xc_ref[:, :SSD_WIDTH]
    gn = SSD_GROUPS * SSD_STATE

    dtx = dt_ref[...] + dtb_ref[...]
    dt = jnp.maximum(dtx, 0.0) + jnp.log1p(jnp.exp(-jnp.abs(dtx)))
    if real_len < q:
        row = lax.broadcasted_iota(jnp.int32, (q, LANES), 0)
        dt = jnp.where(row < real_len, dt, 0.0)
    a = -jnp.exp(alog_ref[...])
    da = dt * a
    ii = lax.broadcasted_iota(jnp.int32, (q, q), 0)
    jj = lax.broadcasted_iota(jnp.int32, (q, q), 1)
    causal = ii >= jj
    tril = jnp.where(causal, 1.0, 0.0).astype(F32)
    acum = jnp.dot(tril, da, preferred_element_type=F32, precision=HI)
    acum_t = acum.T
    dt_t = dt.T
    alast = acum[q - 1:q, :]
    ea = jnp.exp(acum)
    wdec = jnp.exp(alast - acum) * dt
    alast8 = jnp.broadcast_to(alast, (8, LANES))
    dec_e = jnp.exp(jnp.dot(alast8, exp_ref[...], preferred_element_type=F32, precision=HI)[0:1, :])

    lane = lax.broadcasted_iota(jnp.int32, (q, LANES), 1)
    first = lane < SSD_HEAD_DIM
    y_parts = []
    xw_parts = []
    for g in range(SSD_GROUPS):
        bg = xc_ref[:, SSD_WIDTH + g * SSD_STATE:SSD_WIDTH + (g + 1) * SSD_STATE]
        cg = xc_ref[:, SSD_WIDTH + gn + g * SSD_STATE:SSD_WIDTH + gn + (g + 1) * SSD_STATE]
        bg_t = bg.T
        scores = _dot(cg, bg_t)
        per_group = SSD_HEADS // SSD_GROUPS
        for pp in range(per_group // 2):
            p = g * (per_group // 2) + pp
            x_pair = xs[:, p * LANES:(p + 1) * LANES]
            st_pair = st_ref[:, p * LANES:(p + 1) * LANES]
            rhs = jnp.concatenate([x_pair, st_pair], axis=0).astype(BF16)
            outs = []
            for k in range(2):
                h = 2 * p + k
                a_col = acum[:, h:h + 1]
                a_row = acum_t[h:h + 1, :]
                seg = jnp.where(causal, a_col - a_row, 0.0)
                m_h = jnp.where(causal, jnp.exp(seg), 0.0) * scores * dt_t[h:h + 1, :]
                c_h = cg * ea[:, h:h + 1]
                lhs = jnp.concatenate([m_h, c_h], axis=1).astype(BF16)
                outs.append(jnp.dot(lhs, rhs, preferred_element_type=F32))
            y_parts.append(jnp.where(first, outs[0], outs[1]))
            w_pair = jnp.where(first, wdec[:, 2 * p:2 * p + 1], wdec[:, 2 * p + 1:2 * p + 2])
            xw_parts.append(x_pair * w_pair)
        half = SSD_WIDTH // SSD_GROUPS
        xw_g = jnp.concatenate(xw_parts[-(per_group // 2):], axis=1)
        upd = _dot(bg_t, xw_g)
        st_ref[:, g * half:(g + 1) * half] = (
            st_ref[:, g * half:(g + 1) * half] * dec_e[:, g * half:(g + 1) * half] + upd)

    y = jnp.concatenate(y_parts, axis=1) + xs * dsk_ref[...]
    zz = z_ref[...]
    y = y * (zz * (1.0 / (1.0 + jnp.exp(-zz))))
    y_ref[...] = _rms(y, nw_ref[...])

    @pl.when(c == nc - 1)
    def _():
        for p in range(n_pairs):
            so_ref[pl.ds(p * LANES, LANES), :] = st_ref[:, p * LANES:(p + 1) * LANES].T


def _ssd(xbc, z, dt, conv_state, ssm_state, conv_w, conv_b, dt_bias, a_log, d_skip, norm_w, bsz, seq, layer=0):
    has_init = conv_state is not None
    real_len = min(seq, SSD_CHUNK)
    assert real_len >= CONV_WIDTH - 1
    nc = max(seq // SSD_CHUNK, 1)
    if not has_init:
        conv_state = jnp.zeros((1, bsz, CONV_WIDTH - 1, CONV_CH), F32)
        ssm_state = jnp.zeros((1, bsz, SSD_HEADS, SSD_HEAD_DIM, SSD_STATE), F32)
    s0 = ssm_state.reshape(ssm_state.shape[0], bsz, SSD_WIDTH, SSD_STATE)
    pad_h = LANES - SSD_HEADS
    dtb = jnp.pad(dt_bias, (0, pad_h)).reshape(1, LANES)
    alog = jnp.pad(a_log, (0, pad_h)).reshape(1, LANES)
    dsk = jnp.repeat(d_skip, SSD_HEAD_DIM).reshape(1, SSD_WIDTH)
    expand = (np.arange(LANES)[:, None] == (np.arange(SSD_WIDTH) // SSD_HEAD_DIM)[None, :]).astype(np.float32)
    xbc3 = xbc.reshape(bsz, seq, CONV_CH)
    z3 = z.reshape(bsz, seq, SSD_WIDTH)
    dt3 = dt.reshape(bsz, seq, LANES)
    seq_p = nc * SSD_CHUNK
    if seq_p != seq:
        pad = ((0, 0), (0, seq_p - seq), (0, 0))
        xbc3, z3, dt3 = jnp.pad(xbc3, pad), jnp.pad(z3, pad), jnp.pad(dt3, pad)

    def tok(width):
        return pl.BlockSpec((None, SSD_CHUNK, width), lambda b, c: (b, c, 0))

    y, cso, so = pl.pallas_call(
        functools.partial(_ssd_body, real_len, has_init),
        grid=(bsz, nc),
        in_specs=[tok(CONV_CH), tok(SSD_WIDTH), tok(LANES),
                  pl.BlockSpec((None, None, CONV_WIDTH - 1, CONV_CH), lambda b, c: (layer, b, 0, 0)),
                  pl.BlockSpec((None, None, SSD_WIDTH, SSD_STATE), lambda b, c: (layer, b, 0, 0)),
                  _full((CONV_WIDTH, CONV_CH)), _full((1, CONV_CH)), _full((1, LANES)), _full((1, LANES)),
                  _full((1, SSD_WIDTH)), _full((1, SSD_WIDTH)), _full((LANES, SSD_WIDTH))],
        out_specs=[tok(SSD_WIDTH),
                   pl.BlockSpec((None, CONV_WIDTH - 1, CONV_CH), lambda b, c: (b, 0, 0)),
                   pl.BlockSpec((None, SSD_WIDTH, SSD_STATE), lambda b, c: (b, 0, 0))],
        out_shape=[jax.ShapeDtypeStruct((bsz, seq_p, SSD_WIDTH), F32),
                   jax.ShapeDtypeStruct((bsz, CONV_WIDTH - 1, CONV_CH), F32),
                   jax.ShapeDtypeStruct((bsz, SSD_WIDTH, SSD_STATE), F32)],
        scratch_shapes=[pltpu.VMEM((CONV_PAD, CONV_CH), F32),
                        pltpu.VMEM((SSD_STATE, SSD_WIDTH), F32),
                        pltpu.VMEM((SSD_CHUNK, CONV_CH), F32)],
        compiler_params=_params("parallel", "arbitrary"),
        name="ssd_scan",
    )(xbc3, z3, dt3, conv_state, s0, conv_w, conv_b.reshape(1, CONV_CH), dtb, alog, dsk,
      norm_w.reshape(1, SSD_WIDTH), jnp.asarray(expand))
    return (y[:, :seq].reshape(bsz * seq, SSD_WIDTH), cso,
            so.reshape(bsz, SSD_HEADS, SSD_HEAD_DIM, SSD_STATE))


def _t5_bucket_np(rel):
    n = np.maximum(rel, 0)
    max_exact = REL_BUCKETS // 2
    nf = np.maximum(n, 1).astype(np.float32)
    large = max_exact + (np.log(nf / np.float32(max_exact)) / np.float32(math.log(REL_MAX_DIST / max_exact))
                         * np.float32(REL_BUCKETS - max_exact)).astype(np.int32)
    large = np.minimum(large, REL_BUCKETS - 1)
    return np.where(n < max_exact, n, large).astype(np.int32)


def _bias_tiles_body(bk_ref, rb_ref, o_ref):
    h = pl.program_id(0)
    ii = lax.broadcasted_iota(jnp.int32, (MOBA_BLOCK, MOBA_BLOCK), 0)
    jj = lax.broadcasted_iota(jnp.int32, (MOBA_BLOCK, MOBA_BLOCK), 1)
    for t in range(2):
        bk = bk_ref[t]
        acc = jnp.zeros((MOBA_BLOCK, MOBA_BLOCK), F32)
        for u in range(REL_BUCKETS):
            acc = jnp.where(bk == u, rb_ref[u, h], acc)
        if t == 0:
            acc = jnp.where(jj >= ii, acc, NEG)
        o_ref[t] = acc


def _bias_tiles(rel_bias):
    i = np.arange(MOBA_BLOCK)[None, :]
    j = np.arange(MOBA_BLOCK)[:, None]
    buckets = np.stack([_t5_bucket_np(i - j), _t5_bucket_np(MOBA_BLOCK + i - j)])
    return pl.pallas_call(
        _bias_tiles_body,
        grid=(ATT_HEADS,),
        in_specs=[_full((2, MOBA_BLOCK, MOBA_BLOCK)),
                  pl.BlockSpec(memory_space=pltpu.SMEM)],
        out_specs=pl.BlockSpec((None, 2, MOBA_BLOCK, MOBA_BLOCK), lambda h: (h, 0, 0, 0)),
        out_shape=jax.ShapeDtypeStruct((ATT_HEADS, 2, MOBA_BLOCK, MOBA_BLOCK), F32),
        compiler_params=_params("parallel"),
        name="bias_tiles",
    )(jnp.asarray(buckets), rel_bias)


def _block_rank(g, blk):
    rank = jnp.zeros(g.shape, jnp.int32)
    for m in range(g.shape[1]):
        gm = g[:, m:m + 1]
        rank = rank + ((gm > g) | ((gm == g) & (m < blk))).astype(jnp.int32)
    return rank


def _block_rank_t(g, blk):
    rank = jnp.zeros(g.shape, jnp.int32)
    for m in range(g.shape[0]):
        gm = g[m:m + 1, :]
        rank = rank + ((gm > g) | ((gm == g) & (m < blk))).astype(jnp.int32)
    return rank


def _moba_prompt_body(nblk, q_ref, k_ref, v_ref, bias_ref, cfar_ref, o_ref, km_ref, k16_ref, vt_ref, qt_ref,
                      b2_ref):
    pair = pl.program_id(0)
    blk_rows = MOBA_BLOCK
    hd = ATT_HEAD_DIM
    scale = hd ** -0.5 * LOG2E
    for kk in range(2):
        for t in range(2):
            b2_ref[kk, t] = bias_ref[kk, t] * LOG2E
    per_blk = blk_rows // LANES
    for n in range(nblk):
        tot = None
        for c in range(n * per_blk, (n + 1) * per_blk):
            kb = k_ref[:, c * LANES:(c + 1) * LANES].T
            k16_ref[pl.ds(c * LANES, LANES), :] = kb.astype(BF16)
            part = jnp.sum(kb, axis=0, keepdims=True)
            tot = part if tot is None else tot + part
        km_ref[n:n + 1, :] = tot * (1.0 / blk_rows)
    vt_ref[...] = v_ref[...].astype(BF16)
    for c in range(nblk * per_blk):
        qt_ref[:, c * LANES:(c + 1) * LANES] = (q_ref[pl.ds(c * LANES, LANES), :] * scale).T.astype(BF16)
    kmean = km_ref[...].astype(BF16)
    row = lax.broadcasted_iota(jnp.int32, (LANES, blk_rows), 0)
    blk = lax.broadcasted_iota(jnp.int32, (nblk, blk_rows), 0)
    zero16 = jnp.zeros((LANES, blk_rows), BF16)

    for i in range(nblk):
        qt = qt_ref[:, i * blk_rows:(i + 1) * blk_rows]
        n_keys = (i + 1) * blk_rows
        halves = []
        for kk in range(2):
            qth = jnp.where(row < hd, qt, zero16) if kk == 0 else jnp.where(row >= hd, qt, zero16)
            gate = jnp.dot(kmean, qth, preferred_element_type=F32)
            past = blk < i
            gate = jnp.where(past, gate, NEG)
            sel = past & (_block_rank_t(gate, blk) < MOBA_TOPK)
            pen = jnp.where(sel, cfar_ref[2 * pair + kk] * LOG2E, NEG)
            pen_prev = jnp.where(sel[max(i - 1, 0):max(i - 1, 0) + 1, :], 0.0, NEG)

            def logits(n):
                seg = jnp.dot(k16_ref[n * blk_rows:(n + 1) * blk_rows, :], qth, preferred_element_type=F32)
                if n == i:
                    return seg + b2_ref[kk, 0]
                if n == i - 1:
                    return seg + b2_ref[kk, 1] + pen_prev
                return seg + pen[n:n + 1, :]

            m = logits(0)
            for n in range(1, i + 1):
                m = jnp.maximum(m, logits(n))
            m = jnp.max(m, axis=0, keepdims=True)
            tot = None
            acc = None
            for n in range(i + 1):
                p = jnp.exp2(logits(n) - m)
                tot = p if tot is None else tot + p
                pv = jnp.dot(vt_ref[kk * hd:(kk + 1) * hd, n * blk_rows:(n + 1) * blk_rows], p.astype(BF16),
                             preferred_element_type=F32)
                acc = pv if acc is None else acc + pv
            halves.append(acc / jnp.sum(tot, axis=0, keepdims=True))
        o_ref[pl.ds(i * blk_rows, blk_rows), :] = jnp.concatenate(halves, axis=0).T


def _moba_prompt(q, k, v, bias_tiles, cfar, bsz, seq, layer):
    nblk = seq // MOBA_BLOCK
    n_pairs = ATT_HEADS // 2
    tok = pl.BlockSpec((seq, LANES), lambda p, b: (b, p))
    tok_l = pl.BlockSpec((None, None, LANES, seq), lambda p, b: (layer, b, p, 0))
    return pl.pallas_call(
        functools.partial(_moba_prompt_body, nblk),
        grid=(n_pairs, bsz),
        in_specs=[tok, tok_l, tok_l,
                  pl.BlockSpec((2, 2, MOBA_BLOCK, MOBA_BLOCK), lambda p, b: (p, 0, 0, 0)),
                  pl.BlockSpec(memory_space=pltpu.SMEM)],
        out_specs=tok,
        out_shape=jax.ShapeDtypeStruct((bsz * seq, ATT_WIDTH), F32),
        scratch_shapes=[pltpu.VMEM((nblk, LANES), F32), pltpu.VMEM((seq, LANES), BF16),
                        pltpu.VMEM((LANES, seq), BF16), pltpu.VMEM((LANES, seq), BF16),
                        pltpu.VMEM((2, 2, MOBA_BLOCK, MOBA_BLOCK), F32)],
        compiler_params=_params("parallel", "parallel"),
        name="moba_prompt",
    )(q, k, v, bias_tiles, cfar)


PAGES_PER_STEP = 16
PAGE_SLOTS = 4
PAGE_LOOKAHEAD = 3
ROWS8 = 8


def _sample_moba_body(layer, n_new, n_k, page_size, pt_ref, q_ref, kn_ref, vn_ref, tprev_ref, town_ref, cfar_ref,
                      ck_ref, cv_ref, o_ref, buf_ref, sem_ref, s_ref, acc_ref, l_ref, idxv_ref, idxs_ref, isem,
                      vbuf_ref, vsem):
    b = pl.program_id(0)
    n_seq = pl.num_programs(0)
    n_chunks = n_k
    pages_per_blk = MOBA_BLOCK // page_size
    n_tok = s_ref.shape[2]
    n_blk = n_tok // MOBA_BLOCK
    q = q_ref[...]
    qh = [q[:, h * ATT_HEAD_DIM:(h + 1) * ATT_HEAD_DIM] for h in range(ATT_HEADS)]
    scale = ATT_HEAD_DIM ** -0.5

    assert n_chunks % PAGE_SLOTS == 0 and PAGE_LOOKAHEAD < PAGE_SLOTS

    def chunk_copies(seq, g):
        slot = g % PAGE_SLOTS
        return [pltpu.make_async_copy(ck_ref.at[layer, pt_ref[seq, g * PAGES_PER_STEP + r]],
                                      buf_ref.at[slot, r], sem_ref.at[slot])
                for r in range(PAGES_PER_STEP)]

    def start_chunk(seq, g):
        for c in chunk_copies(seq, g):
            c.start()

    def advance(g):
        for c in chunk_copies(b, g):
            c.wait()
        nxt = g + PAGE_LOOKAHEAD
        if nxt < n_chunks:
            start_chunk(b, nxt)
        else:
            @pl.when(b + 1 < n_seq)
            def _():
                start_chunk(b + 1, nxt - n_chunks)

    @pl.when(b == 0)
    def _():
        for g in range(PAGE_LOOKAHEAD):
            start_chunk(b, g)

    for g in range(n_k):
        advance(g)
        for r in range(PAGES_PER_STEP):
            start = (g * PAGES_PER_STEP + r) * page_size
            for h in range(ATT_HEADS):
                s_ref[h, :, start:start + page_size] = _dot(qh[h], buf_ref[g % PAGE_SLOTS, r, h])

    blk = lax.broadcasted_iota(jnp.int32, (ROWS8, n_blk), 1)
    col = lax.broadcasted_iota(jnp.int32, (ROWS8, LANES), 1)
    rowi = lax.broadcasted_iota(jnp.int32, (ROWS8, LANES), 0)
    halves_per_blk = MOBA_BLOCK // LANES

    def head_chunks(h):
        s_all = s_ref[h]
        return [s_all[:, c * LANES:(c + 1) * LANES] for c in range(n_tok // LANES)]

    sels = []
    for h in range(ATT_HEADS):
        chunks = head_chunks(h)
        gate = jnp.zeros((ROWS8, n_blk), F32)
        for n in range(n_blk):
            part = chunks[n * halves_per_blk]
            for c in chunks[n * halves_per_blk + 1:(n + 1) * halves_per_blk]:
                part = part + c
            gate = jnp.where(blk == n, jnp.sum(part, axis=1, keepdims=True) * (1.0 / MOBA_BLOCK), gate)
        rank = _block_rank(gate, blk)
        sels.append(rank < MOBA_TOPK)
        ids = jnp.zeros((ROWS8, LANES), jnp.int32)
        for s in range(MOBA_TOPK):
            ids = jnp.where(col == s, jnp.sum(jnp.where(rank == s, blk, 0), axis=1, keepdims=True), ids)
        idxv_ref[h] = ids
    ids_copy = pltpu.make_async_copy(idxv_ref, idxs_ref, isem)
    ids_copy.start()
    ids_copy.wait()

    def value_copies():
        out = []
        for h in range(ATT_HEADS):
            for t in range(n_new):
                for s in range(MOBA_TOPK):
                    first_page = idxs_ref[h, t, s] * pages_per_blk
                    for r in range(pages_per_blk):
                        out.append(pltpu.make_async_copy(cv_ref.at[layer, pt_ref[b, first_page + r], h],
                                                         vbuf_ref.at[h, t * MOBA_TOPK + s, r], vsem))
        return out

    for c in value_copies():
        c.start(priority=1)

    def softmax_over_selected():
        for h in range(ATT_HEADS):
            chunks = head_chunks(h)
            sel = sels[h]
            pen = jnp.where(sel, cfar_ref[h], NEG)
            pen_last = jnp.where(sel[:, n_blk - 1:n_blk], 0.0, NEG)
            logits = []
            for c, x in enumerate(chunks):
                n = c // halves_per_blk
                if n == n_blk - 1:
                    off = (c % halves_per_blk) * LANES
                    logits.append(x * scale + tprev_ref[h, :, off:off + LANES] + pen_last)
                else:
                    logits.append(x * scale + pen[:, n:n + 1])
            kn_h = kn_ref[:, h * ATT_HEAD_DIM:(h + 1) * ATT_HEAD_DIM]
            vn_h = vn_ref[:, h * ATT_HEAD_DIM:(h + 1) * ATT_HEAD_DIM]
            s_own = jnp.where(col < n_new, _dot_t(qh[h] * scale, kn_h) + town_ref[h], NEG)
            m_el = s_own
            for x in logits:
                m_el = jnp.maximum(m_el, x)
            m = jnp.max(m_el, axis=1, keepdims=True)
            p_own = jnp.exp(s_own - m)
            tot = p_own
            for c, x in enumerate(logits):
                p = jnp.exp(x - m)
                s_ref[h, :, c * LANES:(c + 1) * LANES] = p
                tot = tot + p
            acc_ref[h] = _dot(p_own, vn_h)
            l_ref[h] = jnp.broadcast_to(jnp.sum(tot, axis=1, keepdims=True), (ROWS8, LANES))

    softmax_over_selected()

    for c in value_copies():
        c.wait()

    outs = []
    for h in range(ATT_HEADS):
        acc = acc_ref[h]
        for t in range(n_new):
            for s in range(MOBA_TOPK):
                first_tok = idxs_ref[h, t, s] * MOBA_BLOCK
                for r in range(pages_per_blk):
                    start = pl.multiple_of(first_tok + r * page_size, page_size)
                    p = jnp.where(rowi == t, s_ref[h, :, pl.ds(start, page_size)], 0.0)
                    acc = acc + _dot_t(p, vbuf_ref[h, t * MOBA_TOPK + s, r])
        outs.append(acc / l_ref[h][:, :ATT_HEAD_DIM])
    o_ref[...] = jnp.concatenate(outs, axis=1)


def _sample_moba(layer, q8, kn, vn, page_table, cache_kt, cache_vt, bias_tiles, cfar, n_new):
    bsz = q8.shape[0]
    page_size = cache_kt.shape[4]
    n_pages = page_table.shape[1]
    n_k = n_pages // PAGES_PER_STEP
    tprev = jnp.transpose(bias_tiles[:, 1, :, :ROWS8], (0, 2, 1))
    town = jnp.transpose(bias_tiles[:, 0, :LANES, :ROWS8], (0, 2, 1))
    grid_spec = pltpu.PrefetchScalarGridSpec(
        num_scalar_prefetch=1,
        grid=(bsz,),
        in_specs=[pl.BlockSpec((None, ROWS8, ATT_WIDTH), lambda b, pt: (b, 0, 0)),
                  pl.BlockSpec((None, LANES, ATT_WIDTH), lambda b, pt: (b, 0, 0)),
                  pl.BlockSpec((None, LANES, ATT_WIDTH), lambda b, pt: (b, 0, 0)),
                  pl.BlockSpec((ATT_HEADS, ROWS8, MOBA_BLOCK), lambda b, pt: (0, 0, 0)),
                  pl.BlockSpec((ATT_HEADS, ROWS8, LANES), lambda b, pt: (0, 0, 0)),
                  pl.BlockSpec(memory_space=pltpu.SMEM),
                  pl.BlockSpec(memory_space=pl.ANY),
                  pl.BlockSpec(memory_space=pl.ANY)],
        out_specs=pl.BlockSpec((None, ROWS8, ATT_WIDTH), lambda b, pt: (b, 0, 0)),
        scratch_shapes=[pltpu.VMEM((PAGE_SLOTS, PAGES_PER_STEP, ATT_HEADS, ATT_HEAD_DIM, page_size), F32),
                        pltpu.SemaphoreType.DMA((PAGE_SLOTS,)),
                        pltpu.VMEM((ATT_HEADS, ROWS8, n_pages * page_size), F32),
                        pltpu.VMEM((ATT_HEADS, ROWS8, ATT_HEAD_DIM), F32),
                        pltpu.VMEM((ATT_HEADS, ROWS8, LANES), F32),
                        pltpu.VMEM((ATT_HEADS, ROWS8, LANES), jnp.int32),
                        pltpu.SMEM((ATT_HEADS, ROWS8, LANES), jnp.int32),
                        pltpu.SemaphoreType.DMA(()),
                        pltpu.VMEM((ATT_HEADS, n_new * MOBA_TOPK, MOBA_BLOCK // page_size, ATT_HEAD_DIM, page_size),
                                   F32),
                        pltpu.SemaphoreType.DMA(())],
    )
    return pl.pallas_call(
        functools.partial(_sample_moba_body, layer, n_new, n_k, page_size),
        grid_spec=grid_spec,
        out_shape=jax.ShapeDtypeStruct((bsz, ROWS8, ATT_WIDTH), F32),
        compiler_params=_params("arbitrary"),
        name="sample_moba",
    )(page_table, q8, kn, vn, tprev, town, cfar, cache_kt, cache_vt)


def _mix_ca_body(rows_per_batch, h_ref, ys_ref, ya_ref, mk_ref, mv_ref, wo1_ref, wo2_ref, g_ref,
                 wq_ref, wco_ref, o_ref):
    tm = h_ref.shape[0]
    h1 = (h_ref[...] + jnp.dot(ys_ref[...].astype(BF16), wo1_ref[...], preferred_element_type=F32)
          + jnp.dot(ya_ref[...].astype(BF16), wo2_ref[...], preferred_element_type=F32))
    hn = _rms(h1, g_ref[...]).astype(BF16)
    qq = jnp.dot(hn, wq_ref[...], preferred_element_type=F32) * (CA_HEAD_DIM ** -0.5)
    nkeys = mk_ref.shape[0] * mk_ref.shape[1]

    def head_rows(ref, hd):
        if len(ref.shape) == 3:
            return ref[:, :, hd * CA_HEAD_DIM:(hd + 1) * CA_HEAD_DIM].reshape(nkeys, CA_HEAD_DIM)
        return ref[:, :, hd, :].reshape(nkeys, CA_HEAD_DIM)

    if rows_per_batch < tm:
        r = lax.broadcasted_iota(jnp.int32, (tm, nkeys), 0) // rows_per_batch
        c = lax.broadcasted_iota(jnp.int32, (tm, nkeys), 1) // MEM_LEN
        same = r == c
    outs = []
    for hd in range(CA_HEADS):
        sl = slice(hd * CA_HEAD_DIM, (hd + 1) * CA_HEAD_DIM)
        s = _dot_t(qq[:, sl], head_rows(mk_ref, hd))
        if rows_per_batch < tm:
            s = jnp.where(same, s, NEG)
        s = s - jnp.max(s, axis=1, keepdims=True)
        p = jnp.exp(s)
        p = p / jnp.sum(p, axis=1, keepdims=True)
        outs.append(_dot(p, head_rows(mv_ref, hd)))
    o = jnp.concatenate(outs, axis=1).astype(BF16)
    o_ref[...] = h1 + jnp.dot(o, wco_ref[...], preferred_element_type=F32)


def _mix_ca(h, ys, ya, mk, mv, wo1, wo2, g_ca, wq, wco, rows_per_batch, tm, layer=0):
    m = h.shape[0]
    nbat = max(tm // rows_per_batch, 1)
    per = rows_per_batch // tm if rows_per_batch >= tm else 1
    if mk.ndim == 3:
        mem_spec = pl.BlockSpec((nbat, MEM_LEN, CA_WIDTH), lambda i: (i // per, 0, 0))
    else:
        mem_spec = pl.BlockSpec((None, nbat, MEM_LEN, CA_HEADS, CA_HEAD_DIM), lambda i: (layer, i // per, 0, 0, 0))

    def row(width):
        return pl.BlockSpec((tm, width), lambda i: (i, 0))

    return pl.pallas_call(
        functools.partial(_mix_ca_body, rows_per_batch),
        grid=(m // tm,),
        in_specs=[row(D_MODEL), row(SSD_WIDTH), row(ATT_WIDTH),
                  mem_spec, mem_spec,
                  _full(wo1.shape), _full(wo2.shape), _full((1, D_MODEL)), _full(wq.shape), _full(wco.shape)],
        out_specs=row(D_MODEL),
        out_shape=jax.ShapeDtypeStruct((m, D_MODEL), F32),
        compiler_params=_params("parallel"),
        name="mix_ca",
    )(h, ys, ya, mk, mv, wo1, wo2, g_ca.reshape(1, D_MODEL), wq, wco)


FF_CHUNK = 1024


def _mlp_body(final, h_ref, g_ref, wu_ref, wd_ref, gf_ref, o_ref):
    h = h_ref[...]
    xn = _rms(h, g_ref[...]).astype(BF16)
    acc = h
    for c in range(FF // FF_CHUNK):
        u = jnp.dot(xn, wu_ref[:, c * FF_CHUNK:(c + 1) * FF_CHUNK], preferred_element_type=F32)
        u = jnp.maximum(u, 0.0)
        acc = acc + jnp.dot((u * u).astype(BF16), wd_ref[c * FF_CHUNK:(c + 1) * FF_CHUNK, :],
                            preferred_element_type=F32)
    o_ref[...] = _rms(acc, gf_ref[...]) if final else acc


def _mlp(h, g, wu, wd, g_final, final, tm):
    m = h.shape[0]
    row = pl.BlockSpec((tm, D_MODEL), lambda i: (i, 0))
    return pl.pallas_call(
        functools.partial(_mlp_body, final),
        grid=(m // tm,),
        in_specs=[row, _full((1, D_MODEL)), _full(wu.shape), _full(wd.shape), _full((1, D_MODEL))],
        out_specs=row,
        out_shape=jax.ShapeDtypeStruct((m, D_MODEL), F32),
        compiler_params=_params("parallel"),
        name="mlp",
    )(h, g.reshape(1, D_MODEL), wu, wd, g_final.reshape(1, D_MODEL))


def _layer_weights(l, w_in, w_out, w_ca_q, w_ca_kv, w_ca_o, w_up, w_down):
    w = w_in[l]
    o = 0
    parts = []
    for width in (SSD_WIDTH, CONV_CH, SSD_HEADS, ATT_WIDTH, ATT_WIDTH, ATT_WIDTH):
        parts.append(w[:, o:o + width])
        o += width
    parts[2] = jnp.pad(parts[2], ((0, 0), (0, LANES - SSD_HEADS)))
    wo = w_out[l]
    w_in = [p.astype(BF16) for p in parts]
    return dict(
        w_in=w_in,
        w_in_prompt=w_in[:4] + [w_in[4].T, w_in[5].T],
        wo1=wo[:SSD_WIDTH].astype(BF16), wo2=wo[SSD_WIDTH:].astype(BF16),
        wq=w_ca_q[l].astype(BF16), wco=w_ca_o[l].astype(BF16),
        wkv=[w_ca_kv[l][:, :CA_WIDTH].astype(BF16), w_ca_kv[l][:, CA_WIDTH:].astype(BF16)],
        wu=w_up[l].astype(BF16), wd=w_down[l].astype(BF16))


def kernel(x_prompt, x_sample, cache_k, cache_v, cache_mem_k, cache_mem_v, state_conv, state_ssm,
           page_table, mem_prompt, norm_mix, w_in, conv_w, conv_b, dt_bias, a_log, d_skip, ssd_norm,
           rel_bias, w_out, norm_ca, norm_mem, w_ca_q, w_ca_kv, w_ca_o, norm_mlp, w_up, w_down, norm_final):
    depth = w_in.shape[0]
    pb, seq, _ = x_prompt.shape
    sb, dseq, _ = x_sample.shape
    page_size = cache_k.shape[2]
    tm = 512

    bias_tiles = _bias_tiles(rel_bias)
    cfar = rel_bias[REL_BUCKETS - 1]
    cache_kt = jnp.transpose(cache_k, (0, 1, 3, 4, 2))
    cache_vt = jnp.transpose(cache_v, (0, 1, 3, 4, 2))
    mem2d = mem_prompt.reshape(pb * MEM_LEN, D_MODEL)

    def pad_rows(a, rows):
        return jnp.pad(a.reshape(sb, dseq, ATT_WIDTH), ((0, 0), (0, rows - dseq), (0, 0)))

    h = x_prompt.reshape(pb * seq, D_MODEL)
    g = x_sample.reshape(sb * dseq, D_MODEL)
    outs = {n: [] for n in ("pmk", "pmv", "pconv", "pssm", "sk", "sv", "sconv", "sssm")}
    pk = jnp.zeros((depth, pb, ATT_WIDTH, seq), F32)
    pv = jnp.zeros((depth, pb, ATT_WIDTH, seq), F32)
    for l in range(depth):
        lw = _layer_weights(l, w_in, w_out, w_ca_q, w_ca_kv, w_ca_o, w_up, w_down)
        last = l == depth - 1

        z, xbc, dt, q, pk, pv = _norm_proj(h, norm_mix[l], lw["w_in_prompt"], tm, transposed=(4, 5), layer=l,
                                           depth=depth, seq=seq, prev=[pk, pv])
        y_ssd, c_new, s_new = _ssd(xbc, z, dt, None, None, conv_w[l], conv_b[l], dt_bias[l], a_log[l],
                                   d_skip[l], ssd_norm[l], pb, seq)
        y_att = _moba_prompt(q, pk, pv, bias_tiles, cfar, pb, seq, l)
        mk, mv = _norm_proj(mem2d, norm_mem[l], lw["wkv"], min(tm, pb * MEM_LEN))
        h = _mix_ca(h, y_ssd, y_att, mk.reshape(pb, MEM_LEN, CA_WIDTH), mv.reshape(pb, MEM_LEN, CA_WIDTH),
                    lw["wo1"], lw["wo2"], norm_ca[l], lw["wq"], lw["wco"], seq, tm)
        h = _mlp(h, norm_mlp[l], lw["wu"], lw["wd"], norm_final, last, tm)
        outs["pmk"].append(mk.reshape(pb, MEM_LEN, CA_HEADS, CA_HEAD_DIM))
        outs["pmv"].append(mv.reshape(pb, MEM_LEN, CA_HEADS, CA_HEAD_DIM))
        outs["pconv"].append(c_new)
        outs["pssm"].append(s_new)

        ms = sb * dseq
        z, xbc, dt, q, k, v = _norm_proj(g, norm_mix[l], lw["w_in"], ms)
        y_ssd, c_new, s_new = _ssd(xbc, z, dt, state_conv, state_ssm, conv_w[l], conv_b[l], dt_bias[l],
                                   a_log[l], d_skip[l], ssd_norm[l], sb, dseq, layer=l)
        y_att = _sample_moba(l, pad_rows(q, ROWS8), pad_rows(k, LANES), pad_rows(v, LANES), page_table,
                             cache_kt, cache_vt, bias_tiles, cfar, dseq)[:, :dseq].reshape(ms, ATT_WIDTH)
        g = _mix_ca(g, y_ssd, y_att, cache_mem_k, cache_mem_v, lw["wo1"], lw["wo2"], norm_ca[l],
                    lw["wq"], lw["wco"], dseq, min(32, ms), layer=l)
        g = _mlp(g, norm_mlp[l], lw["wu"], lw["wd"], norm_final, last, ms)
        outs["sk"].append(k.reshape(sb, dseq, ATT_HEADS, ATT_HEAD_DIM))
        outs["sv"].append(v.reshape(sb, dseq, ATT_HEADS, ATT_HEAD_DIM))
        outs["sconv"].append(c_new)
        outs["sssm"].append(s_new)

    st = {n: jnp.stack(vs) for n, vs in outs.items() if vs}
    def token_major(a):
        return jnp.transpose(a.reshape(depth, pb, ATT_HEADS, ATT_HEAD_DIM, seq), (0, 1, 4, 2, 3))

    return (h.reshape(pb, seq, D_MODEL), g.reshape(sb, dseq, D_MODEL), token_major(pk), token_major(pv),
            st["pmk"], st["pmv"],
            st["pconv"], st["pssm"], st["sk"], st["sv"], st["sconv"], st["sssm"])
```

```python
import functools
import math

import numpy as np
import jax
import jax.numpy as jnp
from jax import lax
from jax.experimental import pallas as pl
from jax.experimental.pallas import tpu as pltpu

D_MODEL = 1024
SSD_HEADS = 16
SSD_HEAD_DIM = 64
SSD_WIDTH = SSD_HEADS * SSD_HEAD_DIM
SSD_GROUPS = 2
SSD_STATE = 128
SSD_CHUNK = 128
CONV_WIDTH = 4
CONV_CH = SSD_WIDTH + 2 * SSD_GROUPS * SSD_STATE
ATT_HEADS = 8
ATT_HEAD_DIM = 64
ATT_WIDTH = ATT_HEADS * ATT_HEAD_DIM
MOBA_BLOCK = 256
MOBA_TOPK = 3
REL_BUCKETS = 32
REL_MAX_DIST = 128
MEM_LEN = 256
CA_HEADS = 4
CA_HEAD_DIM = 128
CA_WIDTH = CA_HEADS * CA_HEAD_DIM
FF = 4 * D_MODEL
EPS = 1e-5

LANES = 128
VMEM_LIMIT = 56 * 1024 * 1024
NEG = -1e30
LOG2E = math.log2(math.e)

BF16 = jnp.bfloat16
F32 = jnp.float32
HI = lax.Precision.HIGHEST


def _params(*sem):
    return pltpu.CompilerParams(dimension_semantics=sem, vmem_limit_bytes=VMEM_LIMIT)


def _rms(x, gain):
    return x * lax.rsqrt(jnp.mean(x * x, axis=-1, keepdims=True) + EPS) * gain


def _dot(a, b):
    return jnp.dot(a.astype(BF16), b.astype(BF16), preferred_element_type=F32)


def _dot_t(a, b):
    return lax.dot_general(a.astype(BF16), b.astype(BF16), (((1,), (1,)), ((), ())),
                           preferred_element_type=F32)


def _full(shape):
    return pl.BlockSpec(shape, lambda *_: (0,) * len(shape))


def _norm_proj_body(n_out, transposed, x_ref, g_ref, *refs):
    w_refs, o_refs = refs[:n_out], refs[len(refs) - n_out:]
    xn = _rms(x_ref[...], g_ref[...]).astype(BF16)
    for idx, (w_ref, o_ref) in enumerate(zip(w_refs, o_refs)):
        if idx in transposed:
            o_ref[...] = lax.dot_general(w_ref[...], xn, (((1,), (1,)), ((), ())), preferred_element_type=F32)
        else:
            o_ref[...] = jnp.dot(xn, w_ref[...], preferred_element_type=F32)


def _norm_proj(x, gain, weights, tm, transposed=(), layer=0, depth=1, seq=None, prev=None):
    m, d = x.shape
    n_out = len(weights)
    out_specs, out_shape = [], []
    for idx, w in enumerate(weights):
        if idx in transposed:
            n = w.shape[0]
            per = seq // tm
            out_specs.append(pl.BlockSpec((None, None, n, tm), lambda i: (layer, i // per, 0, i % per)))
            out_shape.append(jax.ShapeDtypeStruct((depth, m // seq, n, seq), F32))
        else:
            n = w.shape[1]
            out_specs.append(pl.BlockSpec((tm, n), lambda i: (i, 0)))
            out_shape.append(jax.ShapeDtypeStruct((m, n), F32))
    prev = list(prev) if prev is not None else []
    n_in = 2 + n_out
    aliases = {n_in + k: idx for k, idx in enumerate(transposed)} if prev else {}
    return pl.pallas_call(
        functools.partial(_norm_proj_body, n_out, tuple(transposed)),
        grid=(m // tm,),
        in_specs=[pl.BlockSpec((tm, d), lambda i: (i, 0)), _full((1, d))]
        + [_full(w.shape) for w in weights] + [pl.BlockSpec(memory_space=pl.ANY) for _ in prev],
        out_specs=out_specs,
        out_shape=out_shape,
        input_output_aliases=aliases,
        compiler_params=_params("parallel"),
        name="norm_proj",
    )(x, gain.reshape(1, d), *weights, *prev)


CONV_PAD = 8


def _ssd_body(real_len, has_init, xbc_ref, z_ref, dt_ref, cs_ref, s0_ref, cw_ref, cb_ref, dtb_ref,
              alog_ref, dsk_ref, nw_ref, exp_ref, y_ref, cso_ref, so_ref, xpad_ref, st_ref):
    c = pl.program_id(1)
    nc = pl.num_programs(1)
    q = SSD_CHUNK
    n_pairs = SSD_HEADS // 2

    @pl.when(c == 0)
    def _():
        if has_init:
            xpad_ref[pl.ds(0, CONV_PAD), :] = jnp.zeros((CONV_PAD, CONV_CH), F32)
            xpad_ref[pl.ds(CONV_PAD - (CONV_WIDTH - 1), CONV_WIDTH - 1), :] = cs_ref[...]
            for p in range(n_pairs):
                st_ref[:, p * LANES:(p + 1) * LANES] = s0_ref[pl.ds(p * LANES, LANES), :].T
        else:
            xpad_ref[pl.ds(0, CONV_PAD), :] = jnp.zeros((CONV_PAD, CONV_CH), F32)
            st_ref[...] = jnp.zeros_like(st_ref)

    xpad_ref[pl.ds(CONV_PAD, q), :] = xbc_ref[...]

    acc = cb_ref[...] + xpad_ref[pl.ds(CONV_PAD - 3, q), :] * cw_ref[0:1, :]
    acc = acc + xpad_ref[pl.ds(CONV_PAD - 2, q), :] * cw_ref[1:2, :]
    acc = acc + xpad_ref[pl.ds(CONV_PAD - 1, q), :] * cw_ref[2:3, :]
    acc = acc + xpad_ref[pl.ds(CONV_PAD, q), :] * cw_ref[3:4, :]
    xc = acc * (1.0 / (1.0 + jnp.exp(-acc)))
    new_tail = xpad_ref[pl.ds(CONV_PAD + real_len - (CONV_WIDTH - 1), CONV_WIDTH - 1), :]

    @pl.when(c == nc - 1)
    def _():
        cso_ref[...] = new_tail

    xpad_ref[pl.ds(CONV_PAD - (CONV_WIDTH - 1), CONV_WIDTH - 1), :] = new_tail

    xs = xc[:, :SSD_WIDTH]
    gn = SSD_GROUPS * SSD_STATE

    dtx = dt_ref[...] + dtb_ref[...]
    dt = jnp.maximum(dtx, 0.0) + jnp.log1p(jnp.exp(-jnp.abs(dtx)))
    if real_len < q:
        row = lax.broadcasted_iota(jnp.int32, (q, LANES), 0)
        dt = jnp.where(row < real_len, dt, 0.0)
    a = -jnp.exp(alog_ref[...])
    da = dt * a
    ii = lax.broadcasted_iota(jnp.int32, (q, q), 0)
    jj = lax.broadcasted_iota(jnp.int32, (q, q), 1)
    causal = ii >= jj
    tril = jnp.where(causal, 1.0, 0.0).astype(F32)
    acum = jnp.dot(tril, da, preferred_element_type=F32, precision=HI)
    acum_t = acum.T
    dt_t = dt.T
    alast = acum[q - 1:q, :]
    ea = jnp.exp(acum)
    wdec = jnp.exp(alast - acum) * dt
    alast8 = jnp.broadcast_to(alast, (8, LANES))
    dec_e = jnp.exp(jnp.dot(alast8, exp_ref[...], preferred_element_type=F32, precision=HI)[0:1, :])

    lane = lax.broadcasted_iota(jnp.int32, (q, LANES), 1)
    first = lane < SSD_HEAD_DIM
    y_parts = []
    xw_parts = []
    for g in range(SSD_GROUPS):
        bg = xc[:, SSD_WIDTH + g * SSD_STATE:SSD_WIDTH + (g + 1) * SSD_STATE]
        cg = xc[:, SSD_WIDTH + gn + g * SSD_STATE:SSD_WIDTH + gn + (g + 1) * SSD_STATE]
        bg_t = bg.T
        scores = _dot(cg, bg_t)
        per_group = SSD_HEADS // SSD_GROUPS
        for pp in range(per_group // 2):
            p = g * (per_group // 2) + pp
            x_pair = xs[:, p * LANES:(p + 1) * LANES]
            st_pair = st_ref[:, p * LANES:(p + 1) * LANES]
            rhs = jnp.concatenate([x_pair, st_pair], axis=0).astype(BF16)
            outs = []
            for k in range(2):
                h = 2 * p + k
                a_col = acum[:, h:h + 1]
                a_row = acum_t[h:h + 1, :]
                seg = jnp.where(causal, a_col - a_row, 0.0)
                m_h = jnp.where(causal, jnp.exp(seg), 0.0) * scores * dt_t[h:h + 1, :]
                c_h = cg * ea[:, h:h + 1]
                lhs = jnp.concatenate([m_h, c_h], axis=1).astype(BF16)
                outs.append(jnp.dot(lhs, rhs, preferred_element_type=F32))
            y_parts.append(jnp.where(first, outs[0], outs[1]))
            w_pair = jnp.where(first, wdec[:, 2 * p:2 * p + 1], wdec[:, 2 * p + 1:2 * p + 2])
            xw_parts.append(x_pair * w_pair)
        half = SSD_WIDTH // SSD_GROUPS
        xw_g = jnp.concatenate(xw_parts[-(per_group // 2):], axis=1)
        upd = _dot(bg_t, xw_g)
        st_ref[:, g * half:(g + 1) * half] = (
            st_ref[:, g * half:(g + 1) * half] * dec_e[:, g * half:(g + 1) * half] + upd)

    y = jnp.concatenate(y_parts, axis=1) + xs * dsk_ref[...]
    zz = z_ref[...]
    y = y * (zz * (1.0 / (1.0 + jnp.exp(-zz))))
    y_ref[...] = _rms(y, nw_ref[...])

    @pl.when(c == nc - 1)
    def _():
        for p in range(n_pairs):
            so_ref[pl.ds(p * LANES, LANES), :] = st_ref[:, p * LANES:(p + 1) * LANES].T


def _ssd(xbc, z, dt, conv_state, ssm_state, conv_w, conv_b, dt_bias, a_log, d_skip, norm_w, bsz, seq, layer=0):
    has_init = conv_state is not None
    real_len = min(seq, SSD_CHUNK)
    nc = max(seq // SSD_CHUNK, 1)
    if not has_init:
        conv_state = jnp.zeros((1, bsz, CONV_WIDTH - 1, CONV_CH), F32)
        ssm_state = jnp.zeros((1, bsz, SSD_HEADS, SSD_HEAD_DIM, SSD_STATE), F32)
    s0 = ssm_state.reshape(ssm_state.shape[0], bsz, SSD_WIDTH, SSD_STATE)
    pad_h = LANES - SSD_HEADS
    dtb = jnp.pad(dt_bias, (0, pad_h)).reshape(1, LANES)
    alog = jnp.pad(a_log, (0, pad_h)).reshape(1, LANES)
    dsk = jnp.repeat(d_skip, SSD_HEAD_DIM).reshape(1, SSD_WIDTH)
    expand = (np.arange(LANES)[:, None] == (np.arange(SSD_WIDTH) // SSD_HEAD_DIM)[None, :]).astype(np.float32)
    xbc3 = xbc.reshape(bsz, seq, CONV_CH)
    z3 = z.reshape(bsz, seq, SSD_WIDTH)
    dt3 = dt.reshape(bsz, seq, LANES)
    seq_p = nc * SSD_CHUNK
    if seq_p != seq:
        pad = ((0, 0), (0, seq_p - seq), (0, 0))
        xbc3, z3, dt3 = jnp.pad(xbc3, pad), jnp.pad(z3, pad), jnp.pad(dt3, pad)

    def tok(width):
        return pl.BlockSpec((None, SSD_CHUNK, width), lambda b, c: (b, c, 0))

    y, cso, so = pl.pallas_call(
        functools.partial(_ssd_body, real_len, has_init),
        grid=(bsz, nc),
        in_specs=[tok(CONV_CH), tok(SSD_WIDTH), tok(LANES),
                  pl.BlockSpec((None, None, CONV_WIDTH - 1, CONV_CH), lambda b, c: (layer, b, 0, 0)),
                  pl.BlockSpec((None, None, SSD_WIDTH, SSD_STATE), lambda b, c: (layer, b, 0, 0)),
                  _full((CONV_WIDTH, CONV_CH)), _full((1, CONV_CH)), _full((1, LANES)), _full((1, LANES)),
                  _full((1, SSD_WIDTH)), _full((1, SSD_WIDTH)), _full((LANES, SSD_WIDTH))],
        out_specs=[tok(SSD_WIDTH),
                   pl.BlockSpec((None, CONV_WIDTH - 1, CONV_CH), lambda b, c: (b, 0, 0)),
                   pl.BlockSpec((None, SSD_WIDTH, SSD_STATE), lambda b, c: (b, 0, 0))],
        out_shape=[jax.ShapeDtypeStruct((bsz, seq_p, SSD_WIDTH), F32),
                   jax.ShapeDtypeStruct((bsz, CONV_WIDTH - 1, CONV_CH), F32),
                   jax.ShapeDtypeStruct((bsz, SSD_WIDTH, SSD_STATE), F32)],
        scratch_shapes=[pltpu.VMEM((CONV_PAD + SSD_CHUNK, CONV_CH), F32),
                        pltpu.VMEM((SSD_STATE, SSD_WIDTH), F32)],
        compiler_params=_params("parallel", "arbitrary"),
        name="ssd_scan",
    )(xbc3, z3, dt3, conv_state, s0, conv_w, conv_b.reshape(1, CONV_CH), dtb, alog, dsk,
      norm_w.reshape(1, SSD_WIDTH), jnp.asarray(expand))
    return (y[:, :seq].reshape(bsz * seq, SSD_WIDTH), cso,
            so.reshape(bsz, SSD_HEADS, SSD_HEAD_DIM, SSD_STATE))


def _t5_bucket_np(rel):
    n = np.maximum(rel, 0)
    max_exact = REL_BUCKETS // 2
    nf = np.maximum(n, 1).astype(np.float32)
    large = max_exact + (np.log(nf / np.float32(max_exact)) / np.float32(math.log(REL_MAX_DIST / max_exact))
                         * np.float32(REL_BUCKETS - max_exact)).astype(np.int32)
    large = np.minimum(large, REL_BUCKETS - 1)
    return np.where(n < max_exact, n, large).astype(np.int32)


def _bias_tiles_body(bk_ref, rb_ref, o_ref):
    h = pl.program_id(0)
    ii = lax.broadcasted_iota(jnp.int32, (MOBA_BLOCK, MOBA_BLOCK), 0)
    jj = lax.broadcasted_iota(jnp.int32, (MOBA_BLOCK, MOBA_BLOCK), 1)
    for t in range(2):
        bk = bk_ref[t]
        acc = jnp.zeros((MOBA_BLOCK, MOBA_BLOCK), F32)
        for u in range(REL_BUCKETS):
            acc = jnp.where(bk == u, rb_ref[u, h], acc)
        if t == 0:
            acc = jnp.where(jj >= ii, acc, NEG)
        o_ref[t] = acc


def _bias_tiles(rel_bias):
    i = np.arange(MOBA_BLOCK)[None, :]
    j = np.arange(MOBA_BLOCK)[:, None]
    buckets = np.stack([_t5_bucket_np(i - j), _t5_bucket_np(MOBA_BLOCK + i - j)])
    return pl.pallas_call(
        _bias_tiles_body,
        grid=(ATT_HEADS,),
        in_specs=[_full((2, MOBA_BLOCK, MOBA_BLOCK)),
                  pl.BlockSpec(memory_space=pltpu.SMEM)],
        out_specs=pl.BlockSpec((None, 2, MOBA_BLOCK, MOBA_BLOCK), lambda h: (h, 0, 0, 0)),
        out_shape=jax.ShapeDtypeStruct((ATT_HEADS, 2, MOBA_BLOCK, MOBA_BLOCK), F32),
        compiler_params=_params("parallel"),
        name="bias_tiles",
    )(jnp.asarray(buckets), rel_bias)


def _block_rank(g, blk):
    rank = jnp.zeros(g.shape, jnp.int32)
    for m in range(g.shape[1]):
        gm = g[:, m:m + 1]
        rank = rank + ((gm > g) | ((gm == g) & (m < blk))).astype(jnp.int32)
    return rank


def _block_rank_t(g, blk):
    rank = jnp.zeros(g.shape, jnp.int32)
    for m in range(g.shape[0]):
        gm = g[m:m + 1, :]
        rank = rank + ((gm > g) | ((gm == g) & (m < blk))).astype(jnp.int32)
    return rank


def _moba_prompt_body(nblk, q_ref, k_ref, v_ref, bias_ref, cfar_ref, o_ref, km_ref, k16_ref, vt_ref, qt_ref,
                      b2_ref):
    pair = pl.program_id(0)
    blk_rows = MOBA_BLOCK
    hd = ATT_HEAD_DIM
    scale = hd ** -0.5 * LOG2E
    for kk in range(2):
        for t in range(2):
            b2_ref[kk, t] = bias_ref[kk, t] * LOG2E
    per_blk = blk_rows // LANES
    for n in range(nblk):
        tot = None
        for c in range(n * per_blk, (n + 1) * per_blk):
            kb = k_ref[:, c * LANES:(c + 1) * LANES].T
            k16_ref[pl.ds(c * LANES, LANES), :] = kb.astype(BF16)
            part = jnp.sum(kb, axis=0, keepdims=True)
            tot = part if tot is None else tot + part
        km_ref[n:n + 1, :] = tot * (1.0 / blk_rows)
    vt_ref[...] = v_ref[...].astype(BF16)
    for c in range(nblk * per_blk):
        qt_ref[:, c * LANES:(c + 1) * LANES] = (q_ref[pl.ds(c * LANES, LANES), :] * scale).T.astype(BF16)
    kmean = km_ref[...].astype(BF16)
    row = lax.broadcasted_iota(jnp.int32, (LANES, blk_rows), 0)
    blk = lax.broadcasted_iota(jnp.int32, (nblk, blk_rows), 0)
    zero16 = jnp.zeros((LANES, blk_rows), BF16)

    for i in range(nblk):
        qt = qt_ref[:, i * blk_rows:(i + 1) * blk_rows]
        n_keys = (i + 1) * blk_rows
        halves = []
        for kk in range(2):
            qth = jnp.where(row < hd, qt, zero16) if kk == 0 else jnp.where(row >= hd, qt, zero16)
            gate = jnp.dot(kmean, qth, preferred_element_type=F32)
            past = blk < i
            gate = jnp.where(past, gate, NEG)
            sel = past & (_block_rank_t(gate, blk) < MOBA_TOPK)
            pen = jnp.where(sel, cfar_ref[2 * pair + kk] * LOG2E, NEG)
            pen_prev = jnp.where(sel[max(i - 1, 0):max(i - 1, 0) + 1, :], 0.0, NEG)

            def logits(n):
                seg = jnp.dot(k16_ref[n * blk_rows:(n + 1) * blk_rows, :], qth, preferred_element_type=F32)
                if n == i:
                    return seg + b2_ref[kk, 0]
                if n == i - 1:
                    return seg + b2_ref[kk, 1] + pen_prev
                return seg + pen[n:n + 1, :]

            m = logits(0)
            for n in range(1, i + 1):
                m = jnp.maximum(m, logits(n))
            m = jnp.max(m, axis=0, keepdims=True)
            tot = None
            acc = None
            for n in range(i + 1):
                p = jnp.exp2(logits(n) - m)
                tot = p if tot is None else tot + p
                pv = jnp.dot(vt_ref[kk * hd:(kk + 1) * hd, n * blk_rows:(n + 1) * blk_rows], p.astype(BF16),
                             preferred_element_type=F32)
                acc = pv if acc is None else acc + pv
            halves.append(acc / jnp.sum(tot, axis=0, keepdims=True))
        o_ref[pl.ds(i * blk_rows, blk_rows), :] = jnp.concatenate(halves, axis=0).T


def _moba_prompt(q, k, v, bias_tiles, cfar, bsz, seq, layer):
    nblk = seq // MOBA_BLOCK
    n_pairs = ATT_HEADS // 2
    tok = pl.BlockSpec((seq, LANES), lambda p, b: (b, p))
    tok_l = pl.BlockSpec((None, None, LANES, seq), lambda p, b: (layer, b, p, 0))
    return pl.pallas_call(
        functools.partial(_moba_prompt_body, nblk),
        grid=(n_pairs, bsz),
        in_specs=[tok, tok_l, tok_l,
                  pl.BlockSpec((2, 2, MOBA_BLOCK, MOBA_BLOCK), lambda p, b: (p, 0, 0, 0)),
                  pl.BlockSpec(memory_space=pltpu.SMEM)],
        out_specs=tok,
        out_shape=jax.ShapeDtypeStruct((bsz * seq, ATT_WIDTH), F32),
        scratch_shapes=[pltpu.VMEM((nblk, LANES), F32), pltpu.VMEM((seq, LANES), BF16),
                        pltpu.VMEM((LANES, seq), BF16), pltpu.VMEM((LANES, seq), BF16),
                        pltpu.VMEM((2, 2, MOBA_BLOCK, MOBA_BLOCK), F32)],
        compiler_params=_params("parallel", "parallel"),
        name="moba_prompt",
    )(q, k, v, bias_tiles, cfar)


PAGES_PER_STEP = 8
PAGE_SLOTS = 8
PAGE_LOOKAHEAD = 7
ROWS8 = 8


def _sample_moba_body(layer, n_new, n_k, page_size, pt_ref, q_ref, kn_ref, vn_ref, tprev_ref, town_ref, cfar_ref,
                      ck_ref, cv_ref, o_ref, buf_ref, sem_ref, s_ref, acc_ref, l_ref, idxv_ref, idxs_ref, isem,
                      vbuf_ref, vsem):
    b = pl.program_id(0)
    n_seq = pl.num_programs(0)
    n_chunks = n_k
    pages_per_blk = MOBA_BLOCK // page_size
    n_tok = s_ref.shape[2]
    n_blk = n_tok // MOBA_BLOCK
    q = q_ref[...]
    qh = [q[:, h * ATT_HEAD_DIM:(h + 1) * ATT_HEAD_DIM] for h in range(ATT_HEADS)]
    scale = ATT_HEAD_DIM ** -0.5

    assert n_chunks % PAGE_SLOTS == 0 and PAGE_LOOKAHEAD < PAGE_SLOTS

    def chunk_copies(seq, g):
        slot = g % PAGE_SLOTS
        return [pltpu.make_async_copy(ck_ref.at[layer, pt_ref[seq, g * PAGES_PER_STEP + r]],
                                      buf_ref.at[slot, r], sem_ref.at[slot])
                for r in range(PAGES_PER_STEP)]

    def start_chunk(seq, g):
        for c in chunk_copies(seq, g):
            c.start()

    def advance(g):
        for c in chunk_copies(b, g):
            c.wait()
        nxt = g + PAGE_LOOKAHEAD
        if nxt < n_chunks:
            start_chunk(b, nxt)
        else:
            @pl.when(b + 1 < n_seq)
            def _():
                start_chunk(b + 1, nxt - n_chunks)

    @pl.when(b == 0)
    def _():
        for g in range(PAGE_LOOKAHEAD):
            start_chunk(b, g)

    for g in range(n_k):
        advance(g)
        for r in range(PAGES_PER_STEP):
            start = (g * PAGES_PER_STEP + r) * page_size
            for h in range(ATT_HEADS):
                s_ref[h, :, start:start + page_size] = _dot(qh[h], buf_ref[g % PAGE_SLOTS, r, h])

    blk = lax.broadcasted_iota(jnp.int32, (ROWS8, n_blk), 1)
    col = lax.broadcasted_iota(jnp.int32, (ROWS8, LANES), 1)
    rowi = lax.broadcasted_iota(jnp.int32, (ROWS8, LANES), 0)
    halves_per_blk = MOBA_BLOCK // LANES

    def head_chunks(h):
        s_all = s_ref[h]
        return [s_all[:, c * LANES:(c + 1) * LANES] for c in range(n_tok // LANES)]

    sels = []
    for h in range(ATT_HEADS):
        chunks = head_chunks(h)
        gate = jnp.zeros((ROWS8, n_blk), F32)
        for n in range(n_blk):
            part = chunks[n * halves_per_blk]
            for c in chunks[n * halves_per_blk + 1:(n + 1) * halves_per_blk]:
                part = part + c
            gate = jnp.where(blk == n, jnp.sum(part, axis=1, keepdims=True) * (1.0 / MOBA_BLOCK), gate)
        rank = _block_rank(gate, blk)
        sels.append(rank < MOBA_TOPK)
        ids = jnp.zeros((ROWS8, LANES), jnp.int32)
        for s in range(MOBA_TOPK):
            ids = jnp.where(col == s, jnp.sum(jnp.where(rank == s, blk, 0), axis=1, keepdims=True), ids)
        idxv_ref[h] = ids
    ids_copy = pltpu.make_async_copy(idxv_ref, idxs_ref, isem)
    ids_copy.start()
    ids_copy.wait()

    def value_copies():
        out = []
        for h in range(ATT_HEADS):
            for t in range(n_new):
                for s in range(MOBA_TOPK):
                    first_page = idxs_ref[h, t, s] * pages_per_blk
                    for r in range(pages_per_blk):
                        out.append(pltpu.make_async_copy(cv_ref.at[layer, pt_ref[b, first_page + r], h],
                                                         vbuf_ref.at[h, t * MOBA_TOPK + s, r], vsem))
        return out

    for c in value_copies():
        c.start()

    def softmax_over_selected():
        for h in range(ATT_HEADS):
            chunks = head_chunks(h)
            sel = sels[h]
            pen = jnp.where(sel, cfar_ref[h], NEG)
            pen_last = jnp.where(sel[:, n_blk - 1:n_blk], 0.0, NEG)
            logits = []
            for c, x in enumerate(chunks):
                n = c // halves_per_blk
                if n == n_blk - 1:
                    off = (c % halves_per_blk) * LANES
                    logits.append(x * scale + tprev_ref[h, :, off:off + LANES] + pen_last)
                else:
                    logits.append(x * scale + pen[:, n:n + 1])
            kn_h = kn_ref[:, h * ATT_HEAD_DIM:(h + 1) * ATT_HEAD_DIM]
            vn_h = vn_ref[:, h * ATT_HEAD_DIM:(h + 1) * ATT_HEAD_DIM]
            s_own = jnp.where(col < n_new, _dot_t(qh[h] * scale, kn_h) + town_ref[h], NEG)
            m_el = s_own
            for x in logits:
                m_el = jnp.maximum(m_el, x)
            m = jnp.max(m_el, axis=1, keepdims=True)
            p_own = jnp.exp(s_own - m)
            tot = p_own
            for c, x in enumerate(logits):
                p = jnp.exp(x - m)
                s_ref[h, :, c * LANES:(c + 1) * LANES] = p
                tot = tot + p
            acc_ref[h] = _dot(p_own, vn_h)
            l_ref[h] = jnp.broadcast_to(jnp.sum(tot, axis=1, keepdims=True), (ROWS8, LANES))

    softmax_over_selected()

    for c in value_copies():
        c.wait()

    outs = []
    for h in range(ATT_HEADS):
        acc = acc_ref[h]
        for t in range(n_new):
            for s in range(MOBA_TOPK):
                first_tok = idxs_ref[h, t, s] * MOBA_BLOCK
                for r in range(pages_per_blk):
                    start = pl.multiple_of(first_tok + r * page_size, page_size)
                    p = jnp.where(rowi == t, s_ref[h, :, pl.ds(start, page_size)], 0.0)
                    acc = acc + _dot_t(p, vbuf_ref[h, t * MOBA_TOPK + s, r])
        outs.append(acc / l_ref[h][:, :ATT_HEAD_DIM])
    o_ref[...] = jnp.concatenate(outs, axis=1)


def _sample_moba(layer, q8, kn, vn, page_table, cache_kt, cache_vt, bias_tiles, cfar, n_new):
    bsz = q8.shape[0]
    page_size = cache_kt.shape[4]
    n_pages = page_table.shape[1]
    n_k = n_pages // PAGES_PER_STEP
    tprev = jnp.transpose(bias_tiles[:, 1, :, :ROWS8], (0, 2, 1))
    town = jnp.transpose(bias_tiles[:, 0, :LANES, :ROWS8], (0, 2, 1))
    grid_spec = pltpu.PrefetchScalarGridSpec(
        num_scalar_prefetch=1,
        grid=(bsz,),
        in_specs=[pl.BlockSpec((None, ROWS8, ATT_WIDTH), lambda b, pt: (b, 0, 0)),
                  pl.BlockSpec((None, LANES, ATT_WIDTH), lambda b, pt: (b, 0, 0)),
                  pl.BlockSpec((None, LANES, ATT_WIDTH), lambda b, pt: (b, 0, 0)),
                  pl.BlockSpec((ATT_HEADS, ROWS8, MOBA_BLOCK), lambda b, pt: (0, 0, 0)),
                  pl.BlockSpec((ATT_HEADS, ROWS8, LANES), lambda b, pt: (0, 0, 0)),
                  pl.BlockSpec(memory_space=pltpu.SMEM),
                  pl.BlockSpec(memory_space=pl.ANY),
                  pl.BlockSpec(memory_space=pl.ANY)],
        out_specs=pl.BlockSpec((None, ROWS8, ATT_WIDTH), lambda b, pt: (b, 0, 0)),
        scratch_shapes=[pltpu.VMEM((PAGE_SLOTS, PAGES_PER_STEP, ATT_HEADS, ATT_HEAD_DIM, page_size), F32),
                        pltpu.SemaphoreType.DMA((PAGE_SLOTS,)),
                        pltpu.VMEM((ATT_HEADS, ROWS8, n_pages * page_size), F32),
                        pltpu.VMEM((ATT_HEADS, ROWS8, ATT_HEAD_DIM), F32),
                        pltpu.VMEM((ATT_HEADS, ROWS8, LANES), F32),
                        pltpu.VMEM((ATT_HEADS, ROWS8, LANES), jnp.int32),
                        pltpu.SMEM((ATT_HEADS, ROWS8, LANES), jnp.int32),
                        pltpu.SemaphoreType.DMA(()),
                        pltpu.VMEM((ATT_HEADS, n_new * MOBA_TOPK, MOBA_BLOCK // page_size, ATT_HEAD_DIM, page_size),
                                   F32),
                        pltpu.SemaphoreType.DMA(())],
    )
    return pl.pallas_call(
        functools.partial(_sample_moba_body, layer, n_new, n_k, page_size),
        grid_spec=grid_spec,
        out_shape=jax.ShapeDtypeStruct((bsz, ROWS8, ATT_WIDTH), F32),
        compiler_params=_params("arbitrary"),
        name="sample_moba",
    )(page_table, q8, kn, vn, tprev, town, cfar, cache_kt, cache_vt)


def _mix_ca_body(rows_per_batch, h_ref, ys_ref, ya_ref, mk_ref, mv_ref, wo1_ref, wo2_ref, g_ref,
                 wq_ref, wco_ref, o_ref):
    tm = h_ref.shape[0]
    h1 = (h_ref[...] + jnp.dot(ys_ref[...].astype(BF16), wo1_ref[...], preferred_element_type=F32)
          + jnp.dot(ya_ref[...].astype(BF16), wo2_ref[...], preferred_element_type=F32))
    hn = _rms(h1, g_ref[...]).astype(BF16)
    qq = jnp.dot(hn, wq_ref[...], preferred_element_type=F32) * (CA_HEAD_DIM ** -0.5)
    nkeys = mk_ref.shape[0] * mk_ref.shape[1]

    def head_rows(ref, hd):
        if len(ref.shape) == 3:
            return ref[:, :, hd * CA_HEAD_DIM:(hd + 1) * CA_HEAD_DIM].reshape(nkeys, CA_HEAD_DIM)
        return ref[:, :, hd, :].reshape(nkeys, CA_HEAD_DIM)

    if rows_per_batch < tm:
        r = lax.broadcasted_iota(jnp.int32, (tm, nkeys), 0) // rows_per_batch
        c = lax.broadcasted_iota(jnp.int32, (tm, nkeys), 1) // MEM_LEN
        same = r == c
    outs = []
    for hd in range(CA_HEADS):
        sl = slice(hd * CA_HEAD_DIM, (hd + 1) * CA_HEAD_DIM)
        s = _dot_t(qq[:, sl], head_rows(mk_ref, hd))
        if rows_per_batch < tm:
            s = jnp.where(same, s, NEG)
        s = s - jnp.max(s, axis=1, keepdims=True)
        p = jnp.exp(s)
        p = p / jnp.sum(p, axis=1, keepdims=True)
        outs.append(_dot(p, head_rows(mv_ref, hd)))
    o = jnp.concatenate(outs, axis=1).astype(BF16)
    o_ref[...] = h1 + jnp.dot(o, wco_ref[...], preferred_element_type=F32)


def _mix_ca(h, ys, ya, mk, mv, wo1, wo2, g_ca, wq, wco, rows_per_batch, tm, layer=0):
    m = h.shape[0]
    nbat = max(tm // rows_per_batch, 1)
    per = rows_per_batch // tm if rows_per_batch >= tm else 1
    if mk.ndim == 3:
        mem_spec = pl.BlockSpec((nbat, MEM_LEN, CA_WIDTH), lambda i: (i // per, 0, 0))
    else:
        mem_spec = pl.BlockSpec((None, nbat, MEM_LEN, CA_HEADS, CA_HEAD_DIM), lambda i: (layer, i // per, 0, 0, 0))

    def row(width):
        return pl.BlockSpec((tm, width), lambda i: (i, 0))

    return pl.pallas_call(
        functools.partial(_mix_ca_body, rows_per_batch),
        grid=(m // tm,),
        in_specs=[row(D_MODEL), row(SSD_WIDTH), row(ATT_WIDTH),
                  mem_spec, mem_spec,
                  _full(wo1.shape), _full(wo2.shape), _full((1, D_MODEL)), _full(wq.shape), _full(wco.shape)],
        out_specs=row(D_MODEL),
        out_shape=jax.ShapeDtypeStruct((m, D_MODEL), F32),
        compiler_params=_params("parallel"),
        name="mix_ca",
    )(h, ys, ya, mk, mv, wo1, wo2, g_ca.reshape(1, D_MODEL), wq, wco)


FF_CHUNK = 1024


def _mlp_body(final, h_ref, g_ref, wu_ref, wd_ref, gf_ref, o_ref):
    h = h_ref[...]
    xn = _rms(h, g_ref[...]).astype(BF16)
    acc = h
    for c in range(FF // FF_CHUNK):
        u = jnp.dot(xn, wu_ref[:, c * FF_CHUNK:(c + 1) * FF_CHUNK], preferred_element_type=F32)
        u = jnp.maximum(u, 0.0)
        acc = acc + jnp.dot((u * u).astype(BF16), wd_ref[c * FF_CHUNK:(c + 1) * FF_CHUNK, :],
                            preferred_element_type=F32)
    o_ref[...] = _rms(acc, gf_ref[...]) if final else acc


def _mlp(h, g, wu, wd, g_final, final, tm):
    m = h.shape[0]
    row = pl.BlockSpec((tm, D_MODEL), lambda i: (i, 0))
    return pl.pallas_call(
        functools.partial(_mlp_body, final),
        grid=(m // tm,),
        in_specs=[row, _full((1, D_MODEL)), _full(wu.shape), _full(wd.shape), _full((1, D_MODEL))],
        out_specs=row,
        out_shape=jax.ShapeDtypeStruct((m, D_MODEL), F32),
        compiler_params=_params("parallel"),
        name="mlp",
    )(h, g.reshape(1, D_MODEL), wu, wd, g_final.reshape(1, D_MODEL))


def _layer_weights(l, w_in, w_out, w_ca_q, w_ca_kv, w_ca_o, w_up, w_down):
    w = w_in[l]
    o = 0
    parts = []
    for width in (SSD_WIDTH, CONV_CH, SSD_HEADS, ATT_WIDTH, ATT_WIDTH, ATT_WIDTH):
        parts.append(w[:, o:o + width])
        o += width
    parts[2] = jnp.pad(parts[2], ((0, 0), (0, LANES - SSD_HEADS)))
    wo = w_out[l]
    w_in = [p.astype(BF16) for p in parts]
    return dict(
        w_in=w_in,
        w_in_prompt=w_in[:4] + [w_in[4].T, w_in[5].T],
        wo1=wo[:SSD_WIDTH].astype(BF16), wo2=wo[SSD_WIDTH:].astype(BF16),
        wq=w_ca_q[l].astype(BF16), wco=w_ca_o[l].astype(BF16),
        wkv=[w_ca_kv[l][:, :CA_WIDTH].astype(BF16), w_ca_kv[l][:, CA_WIDTH:].astype(BF16)],
        wu=w_up[l].astype(BF16), wd=w_down[l].astype(BF16))


def kernel(x_prompt, x_sample, cache_k, cache_v, cache_mem_k, cache_mem_v, state_conv, state_ssm,
           page_table, mem_prompt, norm_mix, w_in, conv_w, conv_b, dt_bias, a_log, d_skip, ssd_norm,
           rel_bias, w_out, norm_ca, norm_mem, w_ca_q, w_ca_kv, w_ca_o, norm_mlp, w_up, w_down, norm_final):
    depth = w_in.shape[0]
    pb, seq, _ = x_prompt.shape
    sb, dseq, _ = x_sample.shape
    page_size = cache_k.shape[2]
    tm = 512

    bias_tiles = _bias_tiles(rel_bias)
    cfar = rel_bias[REL_BUCKETS - 1]
    cache_kt = jnp.transpose(cache_k, (0, 1, 3, 4, 2))
    cache_vt = jnp.transpose(cache_v, (0, 1, 3, 4, 2))
    mem2d = mem_prompt.reshape(pb * MEM_LEN, D_MODEL)

    def pad_rows(a, rows):
        return jnp.pad(a.reshape(sb, dseq, ATT_WIDTH), ((0, 0), (0, rows - dseq), (0, 0)))

    h = x_prompt.reshape(pb * seq, D_MODEL)
    g = x_sample.reshape(sb * dseq, D_MODEL)
    outs = {n: [] for n in ("pmk", "pmv", "pconv", "pssm", "sk", "sv", "sconv", "sssm")}
    pk = jnp.zeros((depth, pb, ATT_WIDTH, seq), F32)
    pv = jnp.zeros((depth, pb, ATT_WIDTH, seq), F32)
    for l in range(depth):
        lw = _layer_weights(l, w_in, w_out, w_ca_q, w_ca_kv, w_ca_o, w_up, w_down)
        last = l == depth - 1

        z, xbc, dt, q, pk, pv = _norm_proj(h, norm_mix[l], lw["w_in_prompt"], tm, transposed=(4, 5), layer=l,
                                           depth=depth, seq=seq, prev=[pk, pv])
        y_ssd, c_new, s_new = _ssd(xbc, z, dt, None, None, conv_w[l], conv_b[l], dt_bias[l], a_log[l],
                                   d_skip[l], ssd_norm[l], pb, seq)
        y_att = _moba_prompt(q, pk, pv, bias_tiles, cfar, pb, seq, l)
        mk, mv = _norm_proj(mem2d, norm_mem[l], lw["wkv"], min(tm, pb * MEM_LEN))
        h = _mix_ca(h, y_ssd, y_att, mk.reshape(pb, MEM_LEN, CA_WIDTH), mv.reshape(pb, MEM_LEN, CA_WIDTH),
                    lw["wo1"], lw["wo2"], norm_ca[l], lw["wq"], lw["wco"], seq, tm)
        h = _mlp(h, norm_mlp[l], lw["wu"], lw["wd"], norm_final, last, tm)
        outs["pmk"].append(mk.reshape(pb, MEM_LEN, CA_HEADS, CA_HEAD_DIM))
        outs["pmv"].append(mv.reshape(pb, MEM_LEN, CA_HEADS, CA_HEAD_DIM))
        outs["pconv"].append(c_new)
        outs["pssm"].append(s_new)

        ms = sb * dseq
        z, xbc, dt, q, k, v = _norm_proj(g, norm_mix[l], lw["w_in"], ms)
        y_ssd, c_new, s_new = _ssd(xbc, z, dt, state_conv, state_ssm, conv_w[l], conv_b[l], dt_bias[l],
                                   a_log[l], d_skip[l], ssd_norm[l], sb, dseq, layer=l)
        y_att = _sample_moba(l, pad_rows(q, ROWS8), pad_rows(k, LANES), pad_rows(v, LANES), page_table,
                             cache_kt, cache_vt, bias_tiles, cfar, dseq)[:, :dseq].reshape(ms, ATT_WIDTH)
        g = _mix_ca(g, y_ssd, y_att, cache_mem_k, cache_mem_v, lw["wo1"], lw["wo2"], norm_ca[l],
                    lw["wq"], lw["wco"], dseq, min(32, ms), layer=l)
        g = _mlp(g, norm_mlp[l], lw["wu"], lw["wd"], norm_final, last, ms)
        outs["sk"].append(k.reshape(sb, dseq, ATT_HEADS, ATT_HEAD_DIM))
        outs["sv"].append(v.reshape(sb, dseq, ATT_HEADS, ATT_HEAD_DIM))
        outs["sconv"].append(c_new)
        outs["sssm"].append(s_new)

    st = {n: jnp.stack(vs) for n, vs in outs.items() if vs}
    def token_major(a):
        return jnp.transpose(a.reshape(depth, pb, ATT_HEADS, ATT_HEAD_DIM, seq), (0, 1, 4, 2, 3))

    return (h.reshape(pb, seq, D_MODEL), g.reshape(sb, dseq, D_MODEL), token_major(pk), token_major(pv),
            st["pmk"], st["pmv"],
            st["pconv"], st["pssm"], st["sk"], st["sv"], st["sconv"], st["sssm"])
```

```python
import functools
import math

import numpy as np
import jax
import jax.numpy as jnp
from jax import lax
from jax.experimental import pallas as pl
from jax.experimental.pallas import tpu as pltpu

D_MODEL = 1024
SSD_HEADS = 16
SSD_HEAD_DIM = 64
SSD_WIDTH = SSD_HEADS * SSD_HEAD_DIM
SSD_GROUPS = 2
SSD_STATE = 128
SSD_CHUNK = 128
CONV_WIDTH = 4
CONV_CH = SSD_WIDTH + 2 * SSD_GROUPS * SSD_STATE
ATT_HEADS = 8
ATT_HEAD_DIM = 64
ATT_WIDTH = ATT_HEADS * ATT_HEAD_DIM
MOBA_BLOCK = 256
MOBA_TOPK = 3
REL_BUCKETS = 32
REL_MAX_DIST = 128
MEM_LEN = 256
CA_HEADS = 4
CA_HEAD_DIM = 128
CA_WIDTH = CA_HEADS * CA_HEAD_DIM
FF = 4 * D_MODEL
EPS = 1e-5

LANES = 128
VMEM_LIMIT = 56 * 1024 * 1024
NEG = -1e30
LOG2E = math.log2(math.e)

BF16 = jnp.bfloat16
F32 = jnp.float32
HI = lax.Precision.HIGHEST


def _params(*sem):
    return pltpu.CompilerParams(dimension_semantics=sem, vmem_limit_bytes=VMEM_LIMIT)


def _rms(x, gain):
    return x * lax.rsqrt(jnp.mean(x * x, axis=-1, keepdims=True) + EPS) * gain


def _dot(a, b):
    return jnp.dot(a.astype(BF16), b.astype(BF16), preferred_element_type=F32)


def _dot_t(a, b):
    return lax.dot_general(a.astype(BF16), b.astype(BF16), (((1,), (1,)), ((), ())),
                           preferred_element_type=F32)


def _full(shape):
    return pl.BlockSpec(shape, lambda *_: (0,) * len(shape))


def _norm_proj_body(n_out, transposed, x_ref, g_ref, *refs):
    w_refs, o_refs = refs[:n_out], refs[len(refs) - n_out:]
    xn = _rms(x_ref[...], g_ref[...]).astype(BF16)
    for idx, (w_ref, o_ref) in enumerate(zip(w_refs, o_refs)):
        if idx in transposed:
            o_ref[...] = lax.dot_general(w_ref[...], xn, (((1,), (1,)), ((), ())), preferred_element_type=F32)
        else:
            o_ref[...] = jnp.dot(xn, w_ref[...], preferred_element_type=F32)


def _norm_proj(x, gain, weights, tm, transposed=(), layer=0, depth=1, seq=None, prev=None):
    m, d = x.shape
    n_out = len(weights)
    out_specs, out_shape = [], []
    for idx, w in enumerate(weights):
        if idx in transposed:
            n = w.shape[0]
            per = seq // tm
            out_specs.append(pl.BlockSpec((None, None, n, tm), lambda i: (layer, i // per, 0, i % per)))
            out_shape.append(jax.ShapeDtypeStruct((depth, m // seq, n, seq), F32))
        else:
            n = w.shape[1]
            out_specs.append(pl.BlockSpec((tm, n), lambda i: (i, 0)))
            out_shape.append(jax.ShapeDtypeStruct((m, n), F32))
    prev = list(prev) if prev is not None else []
    n_in = 2 + n_out
    aliases = {n_in + k: idx for k, idx in enumerate(transposed)} if prev else {}
    return pl.pallas_call(
        functools.partial(_norm_proj_body, n_out, tuple(transposed)),
        grid=(m // tm,),
        in_specs=[pl.BlockSpec((tm, d), lambda i: (i, 0)), _full((1, d))]
        + [_full(w.shape) for w in weights] + [pl.BlockSpec(memory_space=pl.ANY) for _ in prev],
        out_specs=out_specs,
        out_shape=out_shape,
        input_output_aliases=aliases,
        compiler_params=_params("parallel"),
        name="norm_proj",
    )(x, gain.reshape(1, d), *weights, *prev)


CONV_PAD = 8


def _ssd_body(real_len, has_init, xbc_ref, z_ref, dt_ref, cs_ref, s0_ref, cw_ref, cb_ref, dtb_ref,
              alog_ref, dsk_ref, nw_ref, exp_ref, y_ref, cso_ref, so_ref, xpad_ref, st_ref):
    c = pl.program_id(1)
    nc = pl.num_programs(1)
    q = SSD_CHUNK
    n_pairs = SSD_HEADS // 2

    @pl.when(c == 0)
    def _():
        if has_init:
            xpad_ref[pl.ds(0, CONV_PAD), :] = jnp.zeros((CONV_PAD, CONV_CH), F32)
            xpad_ref[pl.ds(CONV_PAD - (CONV_WIDTH - 1), CONV_WIDTH - 1), :] = cs_ref[...]
            for p in range(n_pairs):
                st_ref[:, p * LANES:(p + 1) * LANES] = s0_ref[pl.ds(p * LANES, LANES), :].T
        else:
            xpad_ref[pl.ds(0, CONV_PAD), :] = jnp.zeros((CONV_PAD, CONV_CH), F32)
            st_ref[...] = jnp.zeros_like(st_ref)

    xpad_ref[pl.ds(CONV_PAD, q), :] = xbc_ref[...]

    acc = cb_ref[...] + xpad_ref[pl.ds(CONV_PAD - 3, q), :] * cw_ref[0:1, :]
    acc = acc + xpad_ref[pl.ds(CONV_PAD - 2, q), :] * cw_ref[1:2, :]
    acc = acc + xpad_ref[pl.ds(CONV_PAD - 1, q), :] * cw_ref[2:3, :]
    acc = acc + xpad_ref[pl.ds(CONV_PAD, q), :] * cw_ref[3:4, :]
    xc = acc * (1.0 / (1.0 + jnp.exp(-acc)))
    new_tail = xpad_ref[pl.ds(CONV_PAD + real_len - (CONV_WIDTH - 1), CONV_WIDTH - 1), :]

    @pl.when(c == nc - 1)
    def _():
        cso_ref[...] = new_tail

    xpad_ref[pl.ds(CONV_PAD - (CONV_WIDTH - 1), CONV_WIDTH - 1), :] = new_tail

    xs = xc[:, :SSD_WIDTH]
    gn = SSD_GROUPS * SSD_STATE

    dtx = dt_ref[...] + dtb_ref[...]
    dt = jnp.maximum(dtx, 0.0) + jnp.log1p(jnp.exp(-jnp.abs(dtx)))
    if real_len < q:
        row = lax.broadcasted_iota(jnp.int32, (q, LANES), 0)
        dt = jnp.where(row < real_len, dt, 0.0)
    a = -jnp.exp(alog_ref[...])
    da = dt * a
    ii = lax.broadcasted_iota(jnp.int32, (q, q), 0)
    jj = lax.broadcasted_iota(jnp.int32, (q, q), 1)
    causal = ii >= jj
    tril = jnp.where(causal, 1.0, 0.0).astype(F32)
    acum = jnp.dot(tril, da, preferred_element_type=F32, precision=HI)
    acum_t = acum.T
    dt_t = dt.T
    alast = acum[q - 1:q, :]
    ea = jnp.exp(acum)
    wdec = jnp.exp(alast - acum) * dt
    alast8 = jnp.broadcast_to(alast, (8, LANES))
    dec_e = jnp.exp(jnp.dot(alast8, exp_ref[...], preferred_element_type=F32, precision=HI)[0:1, :])

    lane = lax.broadcasted_iota(jnp.int32, (q, LANES), 1)
    first = lane < SSD_HEAD_DIM
    y_parts = []
    xw_parts = []
    for g in range(SSD_GROUPS):
        bg = xc[:, SSD_WIDTH + g * SSD_STATE:SSD_WIDTH + (g + 1) * SSD_STATE]
        cg = xc[:, SSD_WIDTH + gn + g * SSD_STATE:SSD_WIDTH + gn + (g + 1) * SSD_STATE]
        bg_t = bg.T
        scores = _dot(cg, bg_t)
        per_group = SSD_HEADS // SSD_GROUPS
        for pp in range(per_group // 2):
            p = g * (per_group // 2) + pp
            x_pair = xs[:, p * LANES:(p + 1) * LANES]
            st_pair = st_ref[:, p * LANES:(p + 1) * LANES]
            rhs = jnp.concatenate([x_pair, st_pair], axis=0).astype(BF16)
            outs = []
            for k in range(2):
                h = 2 * p + k
                a_col = acum[:, h:h + 1]
                a_row = acum_t[h:h + 1, :]
                seg = jnp.where(causal, a_col - a_row, 0.0)
                m_h = jnp.where(causal, jnp.exp(seg), 0.0) * scores * dt_t[h:h + 1, :]
                c_h = cg * ea[:, h:h + 1]
                lhs = jnp.concatenate([m_h, c_h], axis=1).astype(BF16)
                outs.append(jnp.dot(lhs, rhs, preferred_element_type=F32))
            y_parts.append(jnp.where(first, outs[0], outs[1]))
            w_pair = jnp.where(first, wdec[:, 2 * p:2 * p + 1], wdec[:, 2 * p + 1:2 * p + 2])
            xw_parts.append(x_pair * w_pair)
        half = SSD_WIDTH // SSD_GROUPS
        xw_g = jnp.concatenate(xw_parts[-(per_group // 2):], axis=1)
        upd = _dot(bg_t, xw_g)
        st_ref[:, g * half:(g + 1) * half] = (
            st_ref[:, g * half:(g + 1) * half] * dec_e[:, g * half:(g + 1) * half] + upd)

    y = jnp.concatenate(y_parts, axis=1) + xs * dsk_ref[...]
    zz = z_ref[...]
    y = y * (zz * (1.0 / (1.0 + jnp.exp(-zz))))
    y_ref[...] = _rms(y, nw_ref[...])

    @pl.when(c == nc - 1)
    def _():
        for p in range(n_pairs):
            so_ref[pl.ds(p * LANES, LANES), :] = st_ref[:, p * LANES:(p + 1) * LANES].T


def _ssd(xbc, z, dt, conv_state, ssm_state, conv_w, conv_b, dt_bias, a_log, d_skip, norm_w, bsz, seq, layer=0):
    has_init = conv_state is not None
    real_len = min(seq, SSD_CHUNK)
    nc = max(seq // SSD_CHUNK, 1)
    if not has_init:
        conv_state = jnp.zeros((1, bsz, CONV_WIDTH - 1, CONV_CH), F32)
        ssm_state = jnp.zeros((1, bsz, SSD_HEADS, SSD_HEAD_DIM, SSD_STATE), F32)
    s0 = ssm_state.reshape(ssm_state.shape[0], bsz, SSD_WIDTH, SSD_STATE)
    pad_h = LANES - SSD_HEADS
    dtb = jnp.pad(dt_bias, (0, pad_h)).reshape(1, LANES)
    alog = jnp.pad(a_log, (0, pad_h)).reshape(1, LANES)
    dsk = jnp.repeat(d_skip, SSD_HEAD_DIM).reshape(1, SSD_WIDTH)
    expand = (np.arange(LANES)[:, None] == (np.arange(SSD_WIDTH) // SSD_HEAD_DIM)[None, :]).astype(np.float32)
    xbc3 = xbc.reshape(bsz, seq, CONV_CH)
    z3 = z.reshape(bsz, seq, SSD_WIDTH)
    dt3 = dt.reshape(bsz, seq, LANES)
    seq_p = nc * SSD_CHUNK
    if seq_p != seq:
        pad = ((0, 0), (0, seq_p - seq), (0, 0))
        xbc3, z3, dt3 = jnp.pad(xbc3, pad), jnp.pad(z3, pad), jnp.pad(dt3, pad)

    def tok(width):
        return pl.BlockSpec((None, SSD_CHUNK, width), lambda b, c: (b, c, 0))

    y, cso, so = pl.pallas_call(
        functools.partial(_ssd_body, real_len, has_init),
        grid=(bsz, nc),
        in_specs=[tok(CONV_CH), tok(SSD_WIDTH), tok(LANES),
                  pl.BlockSpec((None, None, CONV_WIDTH - 1, CONV_CH), lambda b, c: (layer, b, 0, 0)),
                  pl.BlockSpec((None, None, SSD_WIDTH, SSD_STATE), lambda b, c: (layer, b, 0, 0)),
                  _full((CONV_WIDTH, CONV_CH)), _full((1, CONV_CH)), _full((1, LANES)), _full((1, LANES)),
                  _full((1, SSD_WIDTH)), _full((1, SSD_WIDTH)), _full((LANES, SSD_WIDTH))],
        out_specs=[tok(SSD_WIDTH),
                   pl.BlockSpec((None, CONV_WIDTH - 1, CONV_CH), lambda b, c: (b, 0, 0)),
                   pl.BlockSpec((None, SSD_WIDTH, SSD_STATE), lambda b, c: (b, 0, 0))],
        out_shape=[jax.ShapeDtypeStruct((bsz, seq_p, SSD_WIDTH), F32),
                   jax.ShapeDtypeStruct((bsz, CONV_WIDTH - 1, CONV_CH), F32),
                   jax.ShapeDtypeStruct((bsz, SSD_WIDTH, SSD_STATE), F32)],
        scratch_shapes=[pltpu.VMEM((CONV_PAD + SSD_CHUNK, CONV_CH), F32),
                        pltpu.VMEM((SSD_STATE, SSD_WIDTH), F32)],
        compiler_params=_params("parallel", "arbitrary"),
        name="ssd_scan",
    )(xbc3, z3, dt3, conv_state, s0, conv_w, conv_b.reshape(1, CONV_CH), dtb, alog, dsk,
      norm_w.reshape(1, SSD_WIDTH), jnp.asarray(expand))
    return (y[:, :seq].reshape(bsz * seq, SSD_WIDTH), cso,
            so.reshape(bsz, SSD_HEADS, SSD_HEAD_DIM, SSD_STATE))


def _t5_bucket_np(rel):
    n = np.maximum(rel, 0)
    max_exact = REL_BUCKETS // 2
    nf = np.maximum(n, 1).astype(np.float32)
    large = max_exact + (np.log(nf / np.float32(max_exact)) / np.float32(math.log(REL_MAX_DIST / max_exact))
                         * np.float32(REL_BUCKETS - max_exact)).astype(np.int32)
    large = np.minimum(large, REL_BUCKETS - 1)
    return np.where(n < max_exact, n, large).astype(np.int32)


def _bias_tiles_body(bk_ref, rb_ref, o_ref):
    h = pl.program_id(0)
    ii = lax.broadcasted_iota(jnp.int32, (MOBA_BLOCK, MOBA_BLOCK), 0)
    jj = lax.broadcasted_iota(jnp.int32, (MOBA_BLOCK, MOBA_BLOCK), 1)
    for t in range(2):
        bk = bk_ref[t]
        acc = jnp.zeros((MOBA_BLOCK, MOBA_BLOCK), F32)
        for u in range(REL_BUCKETS):
            acc = jnp.where(bk == u, rb_ref[u, h], acc)
        if t == 0:
            acc = jnp.where(jj >= ii, acc, NEG)
        o_ref[t] = acc


def _bias_tiles(rel_bias):
    i = np.arange(MOBA_BLOCK)[None, :]
    j = np.arange(MOBA_BLOCK)[:, None]
    buckets = np.stack([_t5_bucket_np(i - j), _t5_bucket_np(MOBA_BLOCK + i - j)])
    return pl.pallas_call(
        _bias_tiles_body,
        grid=(ATT_HEADS,),
        in_specs=[_full((2, MOBA_BLOCK, MOBA_BLOCK)),
                  pl.BlockSpec(memory_space=pltpu.SMEM)],
        out_specs=pl.BlockSpec((None, 2, MOBA_BLOCK, MOBA_BLOCK), lambda h: (h, 0, 0, 0)),
        out_shape=jax.ShapeDtypeStruct((ATT_HEADS, 2, MOBA_BLOCK, MOBA_BLOCK), F32),
        compiler_params=_params("parallel"),
        name="bias_tiles",
    )(jnp.asarray(buckets), rel_bias)


def _block_rank(g, blk):
    rank = jnp.zeros(g.shape, jnp.int32)
    for m in range(g.shape[1]):
        gm = g[:, m:m + 1]
        rank = rank + ((gm > g) | ((gm == g) & (m < blk))).astype(jnp.int32)
    return rank


def _block_rank_t(g, blk):
    rank = jnp.zeros(g.shape, jnp.int32)
    for m in range(g.shape[0]):
        gm = g[m:m + 1, :]
        rank = rank + ((gm > g) | ((gm == g) & (m < blk))).astype(jnp.int32)
    return rank


def _moba_prompt_body(nblk, q_ref, k_ref, v_ref, bias_ref, cfar_ref, o_ref, km_ref, k16_ref, vt_ref, qt_ref,
                      b2_ref):
    pair = pl.program_id(0)
    blk_rows = MOBA_BLOCK
    hd = ATT_HEAD_DIM
    scale = hd ** -0.5 * LOG2E
    for kk in range(2):
        for t in range(2):
            b2_ref[kk, t] = bias_ref[kk, t] * LOG2E
    per_blk = blk_rows // LANES
    for n in range(nblk):
        tot = None
        for c in range(n * per_blk, (n + 1) * per_blk):
            kb = k_ref[:, c * LANES:(c + 1) * LANES].T
            k16_ref[pl.ds(c * LANES, LANES), :] = kb.astype(BF16)
            part = jnp.sum(kb, axis=0, keepdims=True)
            tot = part if tot is None else tot + part
        km_ref[n:n + 1, :] = tot * (1.0 / blk_rows)
    vt_ref[...] = v_ref[...].astype(BF16)
    for c in range(nblk * per_blk):
        qt_ref[:, c * LANES:(c + 1) * LANES] = (q_ref[pl.ds(c * LANES, LANES), :] * scale).T.astype(BF16)
    kmean = km_ref[...].astype(BF16)
    row = lax.broadcasted_iota(jnp.int32, (LANES, blk_rows), 0)
    blk = lax.broadcasted_iota(jnp.int32, (nblk, blk_rows), 0)
    zero16 = jnp.zeros((LANES, blk_rows), BF16)

    for i in range(nblk):
        qt = qt_ref[:, i * blk_rows:(i + 1) * blk_rows]
        n_keys = (i + 1) * blk_rows
        halves = []
        for kk in range(2):
            qth = jnp.where(row < hd, qt, zero16) if kk == 0 else jnp.where(row >= hd, qt, zero16)
            gate = jnp.dot(kmean, qth, preferred_element_type=F32)
            past = blk < i
            gate = jnp.where(past, gate, NEG)
            sel = past & (_block_rank_t(gate, blk) < MOBA_TOPK)
            pen = jnp.where(sel, cfar_ref[2 * pair + kk] * LOG2E, NEG)
            pen_prev = jnp.where(sel[max(i - 1, 0):max(i - 1, 0) + 1, :], 0.0, NEG)

            def logits(n):
                seg = jnp.dot(k16_ref[n * blk_rows:(n + 1) * blk_rows, :], qth, preferred_element_type=F32)
                if n == i:
                    return seg + b2_ref[kk, 0]
                if n == i - 1:
                    return seg + b2_ref[kk, 1] + pen_prev
                return seg + pen[n:n + 1, :]

            m = logits(0)
            for n in range(1, i + 1):
                m = jnp.maximum(m, logits(n))
            m = jnp.max(m, axis=0, keepdims=True)
            tot = None
            acc = None
            for n in range(i + 1):
                p = jnp.exp2(logits(n) - m)
                tot = p if tot is None else tot + p
                pv = jnp.dot(vt_ref[kk * hd:(kk + 1) * hd, n * blk_rows:(n + 1) * blk_rows], p.astype(BF16),
                             preferred_element_type=F32)
                acc = pv if acc is None else acc + pv
            halves.append(acc / jnp.sum(tot, axis=0, keepdims=True))
        o_ref[pl.ds(i * blk_rows, blk_rows), :] = jnp.concatenate(halves, axis=0).T


def _moba_prompt(q, k, v, bias_tiles, cfar, bsz, seq, layer):
    nblk = seq // MOBA_BLOCK
    n_pairs = ATT_HEADS // 2
    tok = pl.BlockSpec((seq, LANES), lambda p, b: (b, p))
    tok_l = pl.BlockSpec((None, None, LANES, seq), lambda p, b: (layer, b, p, 0))
    return pl.pallas_call(
        functools.partial(_moba_prompt_body, nblk),
        grid=(n_pairs, bsz),
        in_specs=[tok, tok_l, tok_l,
                  pl.BlockSpec((2, 2, MOBA_BLOCK, MOBA_BLOCK), lambda p, b: (p, 0, 0, 0)),
                  pl.BlockSpec(memory_space=pltpu.SMEM)],
        out_specs=tok,
        out_shape=jax.ShapeDtypeStruct((bsz * seq, ATT_WIDTH), F32),
        scratch_shapes=[pltpu.VMEM((nblk, LANES), F32), pltpu.VMEM((seq, LANES), BF16),
                        pltpu.VMEM((LANES, seq), BF16), pltpu.VMEM((LANES, seq), BF16),
                        pltpu.VMEM((2, 2, MOBA_BLOCK, MOBA_BLOCK), F32)],
        compiler_params=_params("parallel", "parallel"),
        name="moba_prompt",
    )(q, k, v, bias_tiles, cfar)


PAGES_PER_STEP = 16
PAGE_SLOTS = 4
PAGE_LOOKAHEAD = 3
ROWS8 = 8


def _sample_moba_body(layer, n_new, n_k, page_size, pt_ref, q_ref, kn_ref, vn_ref, tprev_ref, town_ref, cfar_ref,
                      ck_ref, cv_ref, o_ref, buf_ref, sem_ref, s_ref, acc_ref, l_ref):
    b = pl.program_id(0)
    n_seq = pl.num_programs(0)
    n_chunks = 2 * n_k
    n_tok = s_ref.shape[2]
    n_blk = n_tok // MOBA_BLOCK
    q = q_ref[...]
    qh = [q[:, h * ATT_HEAD_DIM:(h + 1) * ATT_HEAD_DIM] for h in range(ATT_HEADS)]
    scale = ATT_HEAD_DIM ** -0.5

    assert n_chunks % PAGE_SLOTS == 0 and PAGE_LOOKAHEAD < PAGE_SLOTS

    def chunk_copies(seq, g):
        src = ck_ref if g < n_k else cv_ref
        slot = g % PAGE_SLOTS
        return [pltpu.make_async_copy(src.at[layer, pt_ref[seq, (g % n_k) * PAGES_PER_STEP + r]],
                                      buf_ref.at[slot, r], sem_ref.at[slot])
                for r in range(PAGES_PER_STEP)]

    def start_chunk(seq, g):
        for c in chunk_copies(seq, g):
            c.start()

    def advance(g):
        for c in chunk_copies(b, g):
            c.wait()
        nxt = g + PAGE_LOOKAHEAD
        if nxt < n_chunks:
            start_chunk(b, nxt)
        else:
            @pl.when(b + 1 < n_seq)
            def _():
                start_chunk(b + 1, nxt - n_chunks)

    @pl.when(b == 0)
    def _():
        for g in range(PAGE_LOOKAHEAD):
            start_chunk(b, g)

    for g in range(n_k):
        advance(g)
        for r in range(PAGES_PER_STEP):
            start = (g * PAGES_PER_STEP + r) * page_size
            for h in range(ATT_HEADS):
                s_ref[h, :, start:start + page_size] = _dot(qh[h], buf_ref[g % PAGE_SLOTS, r, h])

    def softmax_over_selected():
        blk = lax.broadcasted_iota(jnp.int32, (ROWS8, n_blk), 1)
        col = lax.broadcasted_iota(jnp.int32, (ROWS8, LANES), 1)
        halves_per_blk = MOBA_BLOCK // LANES
        for h in range(ATT_HEADS):
            s_all = s_ref[h]
            chunks = [s_all[:, c * LANES:(c + 1) * LANES] for c in range(n_tok // LANES)]
            gate = jnp.zeros((ROWS8, n_blk), F32)
            for n in range(n_blk):
                part = chunks[n * halves_per_blk]
                for c in chunks[n * halves_per_blk + 1:(n + 1) * halves_per_blk]:
                    part = part + c
                gate = jnp.where(blk == n, jnp.sum(part, axis=1, keepdims=True) * (1.0 / MOBA_BLOCK), gate)
            sel = _block_rank(gate, blk) < MOBA_TOPK
            pen = jnp.where(sel, cfar_ref[h], NEG)
            pen_last = jnp.where(sel[:, n_blk - 1:n_blk], 0.0, NEG)
            logits = []
            for c, x in enumerate(chunks):
                n = c // halves_per_blk
                if n == n_blk - 1:
                    off = (c % halves_per_blk) * LANES
                    logits.append(x * scale + tprev_ref[h, :, off:off + LANES] + pen_last)
                else:
                    logits.append(x * scale + pen[:, n:n + 1])
            kn_h = kn_ref[:, h * ATT_HEAD_DIM:(h + 1) * ATT_HEAD_DIM]
            vn_h = vn_ref[:, h * ATT_HEAD_DIM:(h + 1) * ATT_HEAD_DIM]
            s_own = jnp.where(col < n_new, _dot_t(qh[h] * scale, kn_h) + town_ref[h], NEG)
            m_el = s_own
            for x in logits:
                m_el = jnp.maximum(m_el, x)
            m = jnp.max(m_el, axis=1, keepdims=True)
            p_own = jnp.exp(s_own - m)
            tot = p_own
            for c, x in enumerate(logits):
                p = jnp.exp(x - m)
                s_ref[h, :, c * LANES:(c + 1) * LANES] = p
                tot = tot + p
            acc_ref[h] = _dot(p_own, vn_h)
            l_ref[h] = jnp.broadcast_to(jnp.sum(tot, axis=1, keepdims=True), (ROWS8, LANES))

    softmax_over_selected()

    for g in range(n_k, n_chunks):
        advance(g)
        for h in range(ATT_HEADS):
            acc = acc_ref[h]
            for r in range(PAGES_PER_STEP):
                start = ((g - n_k) * PAGES_PER_STEP + r) * page_size
                acc = acc + _dot_t(s_ref[h, :, start:start + page_size], buf_ref[g % PAGE_SLOTS, r, h])
            acc_ref[h] = acc

    o_ref[...] = jnp.concatenate(
        [acc_ref[h] / l_ref[h][:, :ATT_HEAD_DIM] for h in range(ATT_HEADS)], axis=1)


def _sample_moba(layer, q8, kn, vn, page_table, cache_kt, cache_vt, bias_tiles, cfar, n_new):
    bsz = q8.shape[0]
    page_size = cache_kt.shape[4]
    n_pages = page_table.shape[1]
    n_k = n_pages // PAGES_PER_STEP
    tprev = jnp.transpose(bias_tiles[:, 1, :, :ROWS8], (0, 2, 1))
    town = jnp.transpose(bias_tiles[:, 0, :LANES, :ROWS8], (0, 2, 1))
    grid_spec = pltpu.PrefetchScalarGridSpec(
        num_scalar_prefetch=1,
        grid=(bsz,),
        in_specs=[pl.BlockSpec((None, ROWS8, ATT_WIDTH), lambda b, pt: (b, 0, 0)),
                  pl.BlockSpec((None, LANES, ATT_WIDTH), lambda b, pt: (b, 0, 0)),
                  pl.BlockSpec((None, LANES, ATT_WIDTH), lambda b, pt: (b, 0, 0)),
                  pl.BlockSpec((ATT_HEADS, ROWS8, MOBA_BLOCK), lambda b, pt: (0, 0, 0)),
                  pl.BlockSpec((ATT_HEADS, ROWS8, LANES), lambda b, pt: (0, 0, 0)),
                  pl.BlockSpec(memory_space=pltpu.SMEM),
                  pl.BlockSpec(memory_space=pl.ANY),
                  pl.BlockSpec(memory_space=pl.ANY)],
        out_specs=pl.BlockSpec((None, ROWS8, ATT_WIDTH), lambda b, pt: (b, 0, 0)),
        scratch_shapes=[pltpu.VMEM((PAGE_SLOTS, PAGES_PER_STEP, ATT_HEADS, ATT_HEAD_DIM, page_size), F32),
                        pltpu.SemaphoreType.DMA((PAGE_SLOTS,)),
                        pltpu.VMEM((ATT_HEADS, ROWS8, n_pages * page_size), F32),
                        pltpu.VMEM((ATT_HEADS, ROWS8, ATT_HEAD_DIM), F32),
                        pltpu.VMEM((ATT_HEADS, ROWS8, LANES), F32)],
    )
    return pl.pallas_call(
        functools.partial(_sample_moba_body, layer, n_new, n_k, page_size),
        grid_spec=grid_spec,
        out_shape=jax.ShapeDtypeStruct((bsz, ROWS8, ATT_WIDTH), F32),
        compiler_params=_params("arbitrary"),
        name="sample_moba",
    )(page_table, q8, kn, vn, tprev, town, cfar, cache_kt, cache_vt)


def _mix_ca_body(rows_per_batch, h_ref, ys_ref, ya_ref, mk_ref, mv_ref, wo1_ref, wo2_ref, g_ref,
                 wq_ref, wco_ref, o_ref):
    tm = h_ref.shape[0]
    h1 = (h_ref[...] + jnp.dot(ys_ref[...].astype(BF16), wo1_ref[...], preferred_element_type=F32)
          + jnp.dot(ya_ref[...].astype(BF16), wo2_ref[...], preferred_element_type=F32))
    hn = _rms(h1, g_ref[...]).astype(BF16)
    qq = jnp.dot(hn, wq_ref[...], preferred_element_type=F32) * (CA_HEAD_DIM ** -0.5)
    nkeys = mk_ref.shape[0] * mk_ref.shape[1]

    def head_rows(ref, hd):
        if len(ref.shape) == 3:
            return ref[:, :, hd * CA_HEAD_DIM:(hd + 1) * CA_HEAD_DIM].reshape(nkeys, CA_HEAD_DIM)
        return ref[:, :, hd, :].reshape(nkeys, CA_HEAD_DIM)

    if rows_per_batch < tm:
        r = lax.broadcasted_iota(jnp.int32, (tm, nkeys), 0) // rows_per_batch
        c = lax.broadcasted_iota(jnp.int32, (tm, nkeys), 1) // MEM_LEN
        same = r == c
    outs = []
    for hd in range(CA_HEADS):
        sl = slice(hd * CA_HEAD_DIM, (hd + 1) * CA_HEAD_DIM)
        s = _dot_t(qq[:, sl], head_rows(mk_ref, hd))
        if rows_per_batch < tm:
            s = jnp.where(same, s, NEG)
        s = s - jnp.max(s, axis=1, keepdims=True)
        p = jnp.exp(s)
        p = p / jnp.sum(p, axis=1, keepdims=True)
        outs.append(_dot(p, head_rows(mv_ref, hd)))
    o = jnp.concatenate(outs, axis=1).astype(BF16)
    o_ref[...] = h1 + jnp.dot(o, wco_ref[...], preferred_element_type=F32)


def _mix_ca(h, ys, ya, mk, mv, wo1, wo2, g_ca, wq, wco, rows_per_batch, tm, layer=0):
    m = h.shape[0]
    nbat = max(tm // rows_per_batch, 1)
    per = rows_per_batch // tm if rows_per_batch >= tm else 1
    if mk.ndim == 3:
        mem_spec = pl.BlockSpec((nbat, MEM_LEN, CA_WIDTH), lambda i: (i // per, 0, 0))
    else:
        mem_spec = pl.BlockSpec((None, nbat, MEM_LEN, CA_HEADS, CA_HEAD_DIM), lambda i: (layer, i // per, 0, 0, 0))

    def row(width):
        return pl.BlockSpec((tm, width), lambda i: (i, 0))

    return pl.pallas_call(
        functools.partial(_mix_ca_body, rows_per_batch),
        grid=(m // tm,),
        in_specs=[row(D_MODEL), row(SSD_WIDTH), row(ATT_WIDTH),
                  mem_spec, mem_spec,
                  _full(wo1.shape), _full(wo2.shape), _full((1, D_MODEL)), _full(wq.shape), _full(wco.shape)],
        out_specs=row(D_MODEL),
        out_shape=jax.ShapeDtypeStruct((m, D_MODEL), F32),
        compiler_params=_params("parallel"),
        name="mix_ca",
    )(h, ys, ya, mk, mv, wo1, wo2, g_ca.reshape(1, D_MODEL), wq, wco)


FF_CHUNK = 1024


def _mlp_body(final, h_ref, g_ref, wu_ref, wd_ref, gf_ref, o_ref):
    h = h_ref[...]
    xn = _rms(h, g_ref[...]).astype(BF16)
    acc = h
    for c in range(FF // FF_CHUNK):
        u = jnp.dot(xn, wu_ref[:, c * FF_CHUNK:(c + 1) * FF_CHUNK], preferred_element_type=F32)
        u = jnp.maximum(u, 0.0)
        acc = acc + jnp.dot((u * u).astype(BF16), wd_ref[c * FF_CHUNK:(c + 1) * FF_CHUNK, :],
                            preferred_element_type=F32)
    o_ref[...] = _rms(acc, gf_ref[...]) if final else acc


def _mlp(h, g, wu, wd, g_final, final, tm):
    m = h.shape[0]
    row = pl.BlockSpec((tm, D_MODEL), lambda i: (i, 0))
    return pl.pallas_call(
        functools.partial(_mlp_body, final),
        grid=(m // tm,),
        in_specs=[row, _full((1, D_MODEL)), _full(wu.shape), _full(wd.shape), _full((1, D_MODEL))],
        out_specs=row,
        out_shape=jax.ShapeDtypeStruct((m, D_MODEL), F32),
        compiler_params=_params("parallel"),
        name="mlp",
    )(h, g.reshape(1, D_MODEL), wu, wd, g_final.reshape(1, D_MODEL))


def _layer_weights(l, w_in, w_out, w_ca_q, w_ca_kv, w_ca_o, w_up, w_down):
    w = w_in[l]
    o = 0
    parts = []
    for width in (SSD_WIDTH, CONV_CH, SSD_HEADS, ATT_WIDTH, ATT_WIDTH, ATT_WIDTH):
        parts.append(w[:, o:o + width])
        o += width
    parts[2] = jnp.pad(parts[2], ((0, 0), (0, LANES - SSD_HEADS)))
    wo = w_out[l]
    w_in = [p.astype(BF16) for p in parts]
    return dict(
        w_in=w_in,
        w_in_prompt=w_in[:4] + [w_in[4].T, w_in[5].T],
        wo1=wo[:SSD_WIDTH].astype(BF16), wo2=wo[SSD_WIDTH:].astype(BF16),
        wq=w_ca_q[l].astype(BF16), wco=w_ca_o[l].astype(BF16),
        wkv=[w_ca_kv[l][:, :CA_WIDTH].astype(BF16), w_ca_kv[l][:, CA_WIDTH:].astype(BF16)],
        wu=w_up[l].astype(BF16), wd=w_down[l].astype(BF16))


def kernel(x_prompt, x_sample, cache_k, cache_v, cache_mem_k, cache_mem_v, state_conv, state_ssm,
           page_table, mem_prompt, norm_mix, w_in, conv_w, conv_b, dt_bias, a_log, d_skip, ssd_norm,
           rel_bias, w_out, norm_ca, norm_mem, w_ca_q, w_ca_kv, w_ca_o, norm_mlp, w_up, w_down, norm_final):
    depth = w_in.shape[0]
    pb, seq, _ = x_prompt.shape
    sb, dseq, _ = x_sample.shape
    page_size = cache_k.shape[2]
    tm = 512

    bias_tiles = _bias_tiles(rel_bias)
    cfar = rel_bias[REL_BUCKETS - 1]
    cache_kt = jnp.transpose(cache_k, (0, 1, 3, 4, 2))
    cache_vt = jnp.transpose(cache_v, (0, 1, 3, 4, 2))
    mem2d = mem_prompt.reshape(pb * MEM_LEN, D_MODEL)

    def pad_rows(a, rows):
        return jnp.pad(a.reshape(sb, dseq, ATT_WIDTH), ((0, 0), (0, rows - dseq), (0, 0)))

    h = x_prompt.reshape(pb * seq, D_MODEL)
    g = x_sample.reshape(sb * dseq, D_MODEL)
    outs = {n: [] for n in ("pmk", "pmv", "pconv", "pssm", "sk", "sv", "sconv", "sssm")}
    pk = jnp.zeros((depth, pb, ATT_WIDTH, seq), F32)
    pv = jnp.zeros((depth, pb, ATT_WIDTH, seq), F32)
    for l in range(depth):
        lw = _layer_weights(l, w_in, w_out, w_ca_q, w_ca_kv, w_ca_o, w_up, w_down)
        last = l == depth - 1

        z, xbc, dt, q, pk, pv = _norm_proj(h, norm_mix[l], lw["w_in_prompt"], tm, transposed=(4, 5), layer=l,
                                           depth=depth, seq=seq, prev=[pk, pv])
        y_ssd, c_new, s_new = _ssd(xbc, z, dt, None, None, conv_w[l], conv_b[l], dt_bias[l], a_log[l],
                                   d_skip[l], ssd_norm[l], pb, seq)
        y_att = _moba_prompt(q, pk, pv, bias_tiles, cfar, pb, seq, l)
        mk, mv = _norm_proj(mem2d, norm_mem[l], lw["wkv"], min(tm, pb * MEM_LEN))
        h = _mix_ca(h, y_ssd, y_att, mk.reshape(pb, MEM_LEN, CA_WIDTH), mv.reshape(pb, MEM_LEN, CA_WIDTH),
                    lw["wo1"], lw["wo2"], norm_ca[l], lw["wq"], lw["wco"], seq, tm)
        h = _mlp(h, norm_mlp[l], lw["wu"], lw["wd"], norm_final, last, tm)
        outs["pmk"].append(mk.reshape(pb, MEM_LEN, CA_HEADS, CA_HEAD_DIM))
        outs["pmv"].append(mv.reshape(pb, MEM_LEN, CA_HEADS, CA_HEAD_DIM))
        outs["pconv"].append(c_new)
        outs["pssm"].append(s_new)

        ms = sb * dseq
        z, xbc, dt, q, k, v = _norm_proj(g, norm_mix[l], lw["w_in"], ms)
        y_ssd, c_new, s_new = _ssd(xbc, z, dt, state_conv, state_ssm, conv_w[l], conv_b[l], dt_bias[l],
                                   a_log[l], d_skip[l], ssd_norm[l], sb, dseq, layer=l)
        y_att = _sample_moba(l, pad_rows(q, ROWS8), pad_rows(k, LANES), pad_rows(v, LANES), page_table,
                             cache_kt, cache_vt, bias_tiles, cfar, dseq)[:, :dseq].reshape(ms, ATT_WIDTH)
        g = _mix_ca(g, y_ssd, y_att, cache_mem_k, cache_mem_v, lw["wo1"], lw["wo2"], norm_ca[l],
                    lw["wq"], lw["wco"], dseq, min(32, ms), layer=l)
        g = _mlp(g, norm_mlp[l], lw["wu"], lw["wd"], norm_final, last, ms)
        outs["sk"].append(k.reshape(sb, dseq, ATT_HEADS, ATT_HEAD_DIM))
        outs["sv"].append(v.reshape(sb, dseq, ATT_HEADS, ATT_HEAD_DIM))
        outs["sconv"].append(c_new)
        outs["sssm"].append(s_new)

    st = {n: jnp.stack(vs) for n, vs in outs.items() if vs}
    def token_major(a):
        return jnp.transpose(a.reshape(depth, pb, ATT_HEADS, ATT_HEAD_DIM, seq), (0, 1, 4, 2, 3))

    return (h.reshape(pb, seq, D_MODEL), g.reshape(sb, dseq, D_MODEL), token_major(pk), token_major(pv),
            st["pmk"], st["pmv"],
            st["pconv"], st["pssm"], st["sk"], st["sv"], st["sconv"], st["sssm"])
```

```python
import functools
import math

import numpy as np
import jax
import jax.numpy as jnp
from jax import lax
from jax.experimental import pallas as pl
from jax.experimental.pallas import tpu as pltpu

D_MODEL = 1024
SSD_HEADS = 16
SSD_HEAD_DIM = 64
SSD_WIDTH = SSD_HEADS * SSD_HEAD_DIM
SSD_GROUPS = 2
SSD_STATE = 128
SSD_CHUNK = 128
CONV_WIDTH = 4
CONV_CH = SSD_WIDTH + 2 * SSD_GROUPS * SSD_STATE
ATT_HEADS = 8
ATT_HEAD_DIM = 64
ATT_WIDTH = ATT_HEADS * ATT_HEAD_DIM
MOBA_BLOCK = 256
MOBA_TOPK = 3
REL_BUCKETS = 32
REL_MAX_DIST = 128
MEM_LEN = 256
CA_HEADS = 4
CA_HEAD_DIM = 128
CA_WIDTH = CA_HEADS * CA_HEAD_DIM
FF = 4 * D_MODEL
EPS = 1e-5

LANES = 128
VMEM_LIMIT = 56 * 1024 * 1024
NEG = -1e30
LOG2E = math.log2(math.e)

BF16 = jnp.bfloat16
F32 = jnp.float32
HI = lax.Precision.HIGHEST


def _params(*sem):
    return pltpu.CompilerParams(dimension_semantics=sem, vmem_limit_bytes=VMEM_LIMIT)


def _rms(x, gain):
    return x * lax.rsqrt(jnp.mean(x * x, axis=-1, keepdims=True) + EPS) * gain


def _dot(a, b):
    return jnp.dot(a.astype(BF16), b.astype(BF16), preferred_element_type=F32)


def _dot_t(a, b):
    return lax.dot_general(a.astype(BF16), b.astype(BF16), (((1,), (1,)), ((), ())),
                           preferred_element_type=F32)


def _full(shape):
    return pl.BlockSpec(shape, lambda *_: (0,) * len(shape))


def _norm_proj_body(n_out, transposed, x_ref, g_ref, *refs):
    w_refs, o_refs = refs[:n_out], refs[len(refs) - n_out:]
    xn = _rms(x_ref[...], g_ref[...]).astype(BF16)
    for idx, (w_ref, o_ref) in enumerate(zip(w_refs, o_refs)):
        if idx in transposed:
            o_ref[...] = lax.dot_general(w_ref[...], xn, (((1,), (1,)), ((), ())), preferred_element_type=F32)
        else:
            o_ref[...] = jnp.dot(xn, w_ref[...], preferred_element_type=F32)


def _norm_proj(x, gain, weights, tm, transposed=(), layer=0, depth=1, seq=None, prev=None):
    m, d = x.shape
    n_out = len(weights)
    out_specs, out_shape = [], []
    for idx, w in enumerate(weights):
        if idx in transposed:
            n = w.shape[0]
            per = seq // tm
            out_specs.append(pl.BlockSpec((None, None, n, tm), lambda i: (layer, i // per, 0, i % per)))
            out_shape.append(jax.ShapeDtypeStruct((depth, m // seq, n, seq), F32))
        else:
            n = w.shape[1]
            out_specs.append(pl.BlockSpec((tm, n), lambda i: (i, 0)))
            out_shape.append(jax.ShapeDtypeStruct((m, n), F32))
    prev = list(prev) if prev is not None else []
    n_in = 2 + n_out
    aliases = {n_in + k: idx for k, idx in enumerate(transposed)} if prev else {}
    return pl.pallas_call(
        functools.partial(_norm_proj_body, n_out, tuple(transposed)),
        grid=(m // tm,),
        in_specs=[pl.BlockSpec((tm, d), lambda i: (i, 0)), _full((1, d))]
        + [_full(w.shape) for w in weights] + [pl.BlockSpec(memory_space=pl.ANY) for _ in prev],
        out_specs=out_specs,
        out_shape=out_shape,
        input_output_aliases=aliases,
        compiler_params=_params("parallel"),
        name="norm_proj",
    )(x, gain.reshape(1, d), *weights, *prev)


CONV_PAD = 8


def _ssd_body(real_len, has_init, xbc_ref, z_ref, dt_ref, cs_ref, s0_ref, cw_ref, cb_ref, dtb_ref,
              alog_ref, dsk_ref, nw_ref, exp_ref, y_ref, cso_ref, so_ref, xpad_ref, st_ref):
    c = pl.program_id(1)
    nc = pl.num_programs(1)
    q = SSD_CHUNK
    n_pairs = SSD_HEADS // 2

    @pl.when(c == 0)
    def _():
        if has_init:
            xpad_ref[pl.ds(0, CONV_PAD), :] = jnp.zeros((CONV_PAD, CONV_CH), F32)
            xpad_ref[pl.ds(CONV_PAD - (CONV_WIDTH - 1), CONV_WIDTH - 1), :] = cs_ref[...]
            for p in range(n_pairs):
                st_ref[:, p * LANES:(p + 1) * LANES] = s0_ref[pl.ds(p * LANES, LANES), :].T
        else:
            xpad_ref[pl.ds(0, CONV_PAD), :] = jnp.zeros((CONV_PAD, CONV_CH), F32)
            st_ref[...] = jnp.zeros_like(st_ref)

    xpad_ref[pl.ds(CONV_PAD, q), :] = xbc_ref[...]

    acc = cb_ref[...] + xpad_ref[pl.ds(CONV_PAD - 3, q), :] * cw_ref[0:1, :]
    acc = acc + xpad_ref[pl.ds(CONV_PAD - 2, q), :] * cw_ref[1:2, :]
    acc = acc + xpad_ref[pl.ds(CONV_PAD - 1, q), :] * cw_ref[2:3, :]
    acc = acc + xpad_ref[pl.ds(CONV_PAD, q), :] * cw_ref[3:4, :]
    xc = acc * (1.0 / (1.0 + jnp.exp(-acc)))
    new_tail = xpad_ref[pl.ds(CONV_PAD + real_len - (CONV_WIDTH - 1), CONV_WIDTH - 1), :]

    @pl.when(c == nc - 1)
    def _():
        cso_ref[...] = new_tail

    xpad_ref[pl.ds(CONV_PAD - (CONV_WIDTH - 1), CONV_WIDTH - 1), :] = new_tail

    xs = xc[:, :SSD_WIDTH]
    gn = SSD_GROUPS * SSD_STATE

    dtx = dt_ref[...] + dtb_ref[...]
    dt = jnp.maximum(dtx, 0.0) + jnp.log1p(jnp.exp(-jnp.abs(dtx)))
    if real_len < q:
        row = lax.broadcasted_iota(jnp.int32, (q, LANES), 0)
        dt = jnp.where(row < real_len, dt, 0.0)
    a = -jnp.exp(alog_ref[...])
    da = dt * a
    ii = lax.broadcasted_iota(jnp.int32, (q, q), 0)
    jj = lax.broadcasted_iota(jnp.int32, (q, q), 1)
    causal = ii >= jj
    tril = jnp.where(causal, 1.0, 0.0).astype(F32)
    acum = jnp.dot(tril, da, preferred_element_type=F32, precision=HI)
    acum_t = acum.T
    dt_t = dt.T
    alast = acum[q - 1:q, :]
    ea = jnp.exp(acum)
    wdec = jnp.exp(alast - acum) * dt
    alast8 = jnp.broadcast_to(alast, (8, LANES))
    dec_e = jnp.exp(jnp.dot(alast8, exp_ref[...], preferred_element_type=F32, precision=HI)[0:1, :])

    lane = lax.broadcasted_iota(jnp.int32, (q, LANES), 1)
    first = lane < SSD_HEAD_DIM
    y_parts = []
    xw_parts = []
    for g in range(SSD_GROUPS):
        bg = xc[:, SSD_WIDTH + g * SSD_STATE:SSD_WIDTH + (g + 1) * SSD_STATE]
        cg = xc[:, SSD_WIDTH + gn + g * SSD_STATE:SSD_WIDTH + gn + (g + 1) * SSD_STATE]
        bg_t = bg.T
        scores = _dot(cg, bg_t)
        per_group = SSD_HEADS // SSD_GROUPS
        for pp in range(per_group // 2):
            p = g * (per_group // 2) + pp
            x_pair = xs[:, p * LANES:(p + 1) * LANES]
            st_pair = st_ref[:, p * LANES:(p + 1) * LANES]
            rhs = jnp.concatenate([x_pair, st_pair], axis=0).astype(BF16)
            outs = []
            for k in range(2):
                h = 2 * p + k
                a_col = acum[:, h:h + 1]
                a_row = acum_t[h:h + 1, :]
                seg = jnp.where(causal, a_col - a_row, 0.0)
                m_h = jnp.where(causal, jnp.exp(seg), 0.0) * scores * dt_t[h:h + 1, :]
                c_h = cg * ea[:, h:h + 1]
                lhs = jnp.concatenate([m_h, c_h], axis=1).astype(BF16)
                outs.append(jnp.dot(lhs, rhs, preferred_element_type=F32))
            y_parts.append(jnp.where(first, outs[0], outs[1]))
            w_pair = jnp.where(first, wdec[:, 2 * p:2 * p + 1], wdec[:, 2 * p + 1:2 * p + 2])
            xw_parts.append(x_pair * w_pair)
        half = SSD_WIDTH // SSD_GROUPS
        xw_g = jnp.concatenate(xw_parts[-(per_group // 2):], axis=1)
        upd = _dot(bg_t, xw_g)
        st_ref[:, g * half:(g + 1) * half] = (
            st_ref[:, g * half:(g + 1) * half] * dec_e[:, g * half:(g + 1) * half] + upd)

    y = jnp.concatenate(y_parts, axis=1) + xs * dsk_ref[...]
    zz = z_ref[...]
    y = y * (zz * (1.0 / (1.0 + jnp.exp(-zz))))
    y_ref[...] = _rms(y, nw_ref[...])

    @pl.when(c == nc - 1)
    def _():
        for p in range(n_pairs):
            so_ref[pl.ds(p * LANES, LANES), :] = st_ref[:, p * LANES:(p + 1) * LANES].T


def _ssd(xbc, z, dt, conv_state, ssm_state, conv_w, conv_b, dt_bias, a_log, d_skip, norm_w, bsz, seq, layer=0):
    has_init = conv_state is not None
    real_len = min(seq, SSD_CHUNK)
    nc = max(seq // SSD_CHUNK, 1)
    if not has_init:
        conv_state = jnp.zeros((1, bsz, CONV_WIDTH - 1, CONV_CH), F32)
        ssm_state = jnp.zeros((1, bsz, SSD_HEADS, SSD_HEAD_DIM, SSD_STATE), F32)
    s0 = ssm_state.reshape(ssm_state.shape[0], bsz, SSD_WIDTH, SSD_STATE)
    pad_h = LANES - SSD_HEADS
    dtb = jnp.pad(dt_bias, (0, pad_h)).reshape(1, LANES)
    alog = jnp.pad(a_log, (0, pad_h)).reshape(1, LANES)
    dsk = jnp.repeat(d_skip, SSD_HEAD_DIM).reshape(1, SSD_WIDTH)
    expand = (np.arange(LANES)[:, None] == (np.arange(SSD_WIDTH) // SSD_HEAD_DIM)[None, :]).astype(np.float32)
    xbc3 = xbc.reshape(bsz, seq, CONV_CH)
    z3 = z.reshape(bsz, seq, SSD_WIDTH)
    dt3 = dt.reshape(bsz, seq, LANES)
    seq_p = nc * SSD_CHUNK
    if seq_p != seq:
        pad = ((0, 0), (0, seq_p - seq), (0, 0))
        xbc3, z3, dt3 = jnp.pad(xbc3, pad), jnp.pad(z3, pad), jnp.pad(dt3, pad)

    def tok(width):
        return pl.BlockSpec((None, SSD_CHUNK, width), lambda b, c: (b, c, 0))

    y, cso, so = pl.pallas_call(
        functools.partial(_ssd_body, real_len, has_init),
        grid=(bsz, nc),
        in_specs=[tok(CONV_CH), tok(SSD_WIDTH), tok(LANES),
                  pl.BlockSpec((None, None, CONV_WIDTH - 1, CONV_CH), lambda b, c: (layer, b, 0, 0)),
                  pl.BlockSpec((None, None, SSD_WIDTH, SSD_STATE), lambda b, c: (layer, b, 0, 0)),
                  _full((CONV_WIDTH, CONV_CH)), _full((1, CONV_CH)), _full((1, LANES)), _full((1, LANES)),
                  _full((1, SSD_WIDTH)), _full((1, SSD_WIDTH)), _full((LANES, SSD_WIDTH))],
        out_specs=[tok(SSD_WIDTH),
                   pl.BlockSpec((None, CONV_WIDTH - 1, CONV_CH), lambda b, c: (b, 0, 0)),
                   pl.BlockSpec((None, SSD_WIDTH, SSD_STATE), lambda b, c: (b, 0, 0))],
        out_shape=[jax.ShapeDtypeStruct((bsz, seq_p, SSD_WIDTH), F32),
                   jax.ShapeDtypeStruct((bsz, CONV_WIDTH - 1, CONV_CH), F32),
                   jax.ShapeDtypeStruct((bsz, SSD_WIDTH, SSD_STATE), F32)],
        scratch_shapes=[pltpu.VMEM((CONV_PAD + SSD_CHUNK, CONV_CH), F32),
                        pltpu.VMEM((SSD_STATE, SSD_WIDTH), F32)],
        compiler_params=_params("parallel", "arbitrary"),
        name="ssd_scan",
    )(xbc3, z3, dt3, conv_state, s0, conv_w, conv_b.reshape(1, CONV_CH), dtb, alog, dsk,
      norm_w.reshape(1, SSD_WIDTH), jnp.asarray(expand))
    return (y[:, :seq].reshape(bsz * seq, SSD_WIDTH), cso,
            so.reshape(bsz, SSD_HEADS, SSD_HEAD_DIM, SSD_STATE))


def _t5_bucket_np(rel):
    n = np.maximum(rel, 0)
    max_exact = REL_BUCKETS // 2
    nf = np.maximum(n, 1).astype(np.float32)
    large = max_exact + (np.log(nf / np.float32(max_exact)) / np.float32(math.log(REL_MAX_DIST / max_exact))
                         * np.float32(REL_BUCKETS - max_exact)).astype(np.int32)
    large = np.minimum(large, REL_BUCKETS - 1)
    return np.where(n < max_exact, n, large).astype(np.int32)


def _bias_tiles_body(bk_ref, rb_ref, o_ref):
    h = pl.program_id(0)
    ii = lax.broadcasted_iota(jnp.int32, (MOBA_BLOCK, MOBA_BLOCK), 0)
    jj = lax.broadcasted_iota(jnp.int32, (MOBA_BLOCK, MOBA_BLOCK), 1)
    for t in range(2):
        bk = bk_ref[t]
        acc = jnp.zeros((MOBA_BLOCK, MOBA_BLOCK), F32)
        for u in range(REL_BUCKETS):
            acc = jnp.where(bk == u, rb_ref[u, h], acc)
        if t == 0:
            acc = jnp.where(jj >= ii, acc, NEG)
        o_ref[t] = acc


def _bias_tiles(rel_bias):
    i = np.arange(MOBA_BLOCK)[None, :]
    j = np.arange(MOBA_BLOCK)[:, None]
    buckets = np.stack([_t5_bucket_np(i - j), _t5_bucket_np(MOBA_BLOCK + i - j)])
    return pl.pallas_call(
        _bias_tiles_body,
        grid=(ATT_HEADS,),
        in_specs=[_full((2, MOBA_BLOCK, MOBA_BLOCK)),
                  pl.BlockSpec(memory_space=pltpu.SMEM)],
        out_specs=pl.BlockSpec((None, 2, MOBA_BLOCK, MOBA_BLOCK), lambda h: (h, 0, 0, 0)),
        out_shape=jax.ShapeDtypeStruct((ATT_HEADS, 2, MOBA_BLOCK, MOBA_BLOCK), F32),
        compiler_params=_params("parallel"),
        name="bias_tiles",
    )(jnp.asarray(buckets), rel_bias)


def _block_rank(g, blk):
    rank = jnp.zeros(g.shape, jnp.int32)
    for m in range(g.shape[1]):
        gm = g[:, m:m + 1]
        rank = rank + ((gm > g) | ((gm == g) & (m < blk))).astype(jnp.int32)
    return rank


def _block_rank_t(g, blk):
    rank = jnp.zeros(g.shape, jnp.int32)
    for m in range(g.shape[0]):
        gm = g[m:m + 1, :]
        rank = rank + ((gm > g) | ((gm == g) & (m < blk))).astype(jnp.int32)
    return rank


def _moba_prompt_body(nblk, q_ref, k_ref, v_ref, bias_ref, cfar_ref, o_ref, km_ref, k16_ref, vt_ref, qt_ref,
                      b2_ref):
    pair = pl.program_id(0)
    blk_rows = MOBA_BLOCK
    hd = ATT_HEAD_DIM
    scale = hd ** -0.5 * LOG2E
    for kk in range(2):
        for t in range(2):
            b2_ref[kk, t] = bias_ref[kk, t] * LOG2E
    per_blk = blk_rows // LANES
    for n in range(nblk):
        tot = None
        for c in range(n * per_blk, (n + 1) * per_blk):
            kb = k_ref[:, c * LANES:(c + 1) * LANES].T
            k16_ref[pl.ds(c * LANES, LANES), :] = kb.astype(BF16)
            part = jnp.sum(kb, axis=0, keepdims=True)
            tot = part if tot is None else tot + part
        km_ref[n:n + 1, :] = tot * (1.0 / blk_rows)
    vt_ref[...] = v_ref[...].astype(BF16)
    for c in range(nblk * per_blk):
        qt_ref[:, c * LANES:(c + 1) * LANES] = (q_ref[pl.ds(c * LANES, LANES), :] * scale).T.astype(BF16)
    kmean = km_ref[...].astype(BF16)
    row = lax.broadcasted_iota(jnp.int32, (LANES, blk_rows), 0)
    blk = lax.broadcasted_iota(jnp.int32, (nblk, blk_rows), 0)
    zero16 = jnp.zeros((LANES, blk_rows), BF16)

    for i in range(nblk):
        qt = qt_ref[:, i * blk_rows:(i + 1) * blk_rows]
        n_keys = (i + 1) * blk_rows
        halves = []
        for kk in range(2):
            qth = jnp.where(row < hd, qt, zero16) if kk == 0 else jnp.where(row >= hd, qt, zero16)
            gate = jnp.dot(kmean, qth, preferred_element_type=F32)
            past = blk < i
            gate = jnp.where(past, gate, NEG)
            sel = past & (_block_rank_t(gate, blk) < MOBA_TOPK)
            pen = jnp.where(sel, cfar_ref[2 * pair + kk] * LOG2E, NEG)
            pen_prev = jnp.where(sel[max(i - 1, 0):max(i - 1, 0) + 1, :], 0.0, NEG)

            def logits(n):
                seg = jnp.dot(k16_ref[n * blk_rows:(n + 1) * blk_rows, :], qth, preferred_element_type=F32)
                if n == i:
                    return seg + b2_ref[kk, 0]
                if n == i - 1:
                    return seg + b2_ref[kk, 1] + pen_prev
                return seg + pen[n:n + 1, :]

            m = logits(0)
            for n in range(1, i + 1):
                m = jnp.maximum(m, logits(n))
            m = jnp.max(m, axis=0, keepdims=True)
            tot = None
            acc = None
            for n in range(i + 1):
                p = jnp.exp2(logits(n) - m)
                tot = p if tot is None else tot + p
                pv = jnp.dot(vt_ref[kk * hd:(kk + 1) * hd, n * blk_rows:(n + 1) * blk_rows], p.astype(BF16),
                             preferred_element_type=F32)
                acc = pv if acc is None else acc + pv
            halves.append(acc / jnp.sum(tot, axis=0, keepdims=True))
        o_ref[pl.ds(i * blk_rows, blk_rows), :] = jnp.concatenate(halves, axis=0).T


def _moba_prompt(q, k, v, bias_tiles, cfar, bsz, seq, layer):
    nblk = seq // MOBA_BLOCK
    n_pairs = ATT_HEADS // 2
    tok = pl.BlockSpec((seq, LANES), lambda p, b: (b, p))
    tok_l = pl.BlockSpec((None, None, LANES, seq), lambda p, b: (layer, b, p, 0))
    return pl.pallas_call(
        functools.partial(_moba_prompt_body, nblk),
        grid=(n_pairs, bsz),
        in_specs=[tok, tok_l, tok_l,
                  pl.BlockSpec((2, 2, MOBA_BLOCK, MOBA_BLOCK), lambda p, b: (p, 0, 0, 0)),
                  pl.BlockSpec(memory_space=pltpu.SMEM)],
        out_specs=tok,
        out_shape=jax.ShapeDtypeStruct((bsz * seq, ATT_WIDTH), F32),
        scratch_shapes=[pltpu.VMEM((nblk, LANES), F32), pltpu.VMEM((seq, LANES), BF16),
                        pltpu.VMEM((LANES, seq), BF16), pltpu.VMEM((LANES, seq), BF16),
                        pltpu.VMEM((2, 2, MOBA_BLOCK, MOBA_BLOCK), F32)],
        compiler_params=_params("parallel", "parallel"),
        name="moba_prompt",
    )(q, k, v, bias_tiles, cfar)


PAGES_PER_STEP = 16
PAGE_SLOTS = 4
PAGE_LOOKAHEAD = 3
ROWS8 = 8


def _sample_moba_body(layer, n_new, n_k, page_size, pt_ref, q_ref, kn_ref, vn_ref, tprev_ref, town_ref, cfar_ref,
                      ck_ref, cv_ref, o_ref, buf_ref, sem_ref, s_ref, acc_ref, l_ref):
    b = pl.program_id(0)
    n_seq = pl.num_programs(0)
    n_chunks = 2 * n_k
    n_tok = s_ref.shape[2]
    n_blk = n_tok // MOBA_BLOCK
    q = q_ref[...]
    qh = [q[:, h * ATT_HEAD_DIM:(h + 1) * ATT_HEAD_DIM] for h in range(ATT_HEADS)]
    scale = ATT_HEAD_DIM ** -0.5

    assert n_chunks % PAGE_SLOTS == 0 and PAGE_LOOKAHEAD < PAGE_SLOTS

    def chunk_copies(seq, g):
        src = ck_ref if g < n_k else cv_ref
        slot = g % PAGE_SLOTS
        return [pltpu.make_async_copy(src.at[layer, pt_ref[seq, (g % n_k) * PAGES_PER_STEP + r]],
                                      buf_ref.at[slot, r], sem_ref.at[slot])
                for r in range(PAGES_PER_STEP)]

    def start_chunk(seq, g):
        for c in chunk_copies(seq, g):
            c.start()

    def advance(g):
        for c in chunk_copies(b, g):
            c.wait()
        nxt = g + PAGE_LOOKAHEAD
        if nxt < n_chunks:
            start_chunk(b, nxt)
        else:
            @pl.when(b + 1 < n_seq)
            def _():
                start_chunk(b + 1, nxt - n_chunks)

    @pl.when(b == 0)
    def _():
        for g in range(PAGE_LOOKAHEAD):
            start_chunk(b, g)

    for g in range(n_k):
        advance(g)
        for r in range(PAGES_PER_STEP):
            start = (g * PAGES_PER_STEP + r) * page_size
            for h in range(ATT_HEADS):
                s_ref[h, :, start:start + page_size] = _dot(qh[h], buf_ref[g % PAGE_SLOTS, r, h])

    def softmax_over_selected():
        blk = lax.broadcasted_iota(jnp.int32, (ROWS8, n_blk), 1)
        col = lax.broadcasted_iota(jnp.int32, (ROWS8, LANES), 1)
        halves_per_blk = MOBA_BLOCK // LANES
        for h in range(ATT_HEADS):
            s_all = s_ref[h]
            chunks = [s_all[:, c * LANES:(c + 1) * LANES] for c in range(n_tok // LANES)]
            gate = jnp.zeros((ROWS8, n_blk), F32)
            for n in range(n_blk):
                part = chunks[n * halves_per_blk]
                for c in chunks[n * halves_per_blk + 1:(n + 1) * halves_per_blk]:
                    part = part + c
                gate = jnp.where(blk == n, jnp.sum(part, axis=1, keepdims=True) * (1.0 / MOBA_BLOCK), gate)
            sel = _block_rank(gate, blk) < MOBA_TOPK
            pen = jnp.where(sel, cfar_ref[h], NEG)
            pen_last = jnp.where(sel[:, n_blk - 1:n_blk], 0.0, NEG)
            logits = []
            for c, x in enumerate(chunks):
                n = c // halves_per_blk
                if n == n_blk - 1:
                    off = (c % halves_per_blk) * LANES
                    logits.append(x * scale + tprev_ref[h, :, off:off + LANES] + pen_last)
                else:
                    logits.append(x * scale + pen[:, n:n + 1])
            kn_h = kn_ref[:, h * ATT_HEAD_DIM:(h + 1) * ATT_HEAD_DIM]
            vn_h = vn_ref[:, h * ATT_HEAD_DIM:(h + 1) * ATT_HEAD_DIM]
            s_own = jnp.where(col < n_new, _dot_t(qh[h] * scale, kn_h) + town_ref[h], NEG)
            m_el = s_own
            for x in logits:
                m_el = jnp.maximum(m_el, x)
            m = jnp.max(m_el, axis=1, keepdims=True)
            p_own = jnp.exp(s_own - m)
            tot = p_own
            for c, x in enumerate(logits):
                p = jnp.exp(x - m)
                s_ref[h, :, c * LANES:(c + 1) * LANES] = p
                tot = tot + p
            acc_ref[h] = _dot(p_own, vn_h)
            l_ref[h] = jnp.broadcast_to(jnp.sum(tot, axis=1, keepdims=True), (ROWS8, LANES))

    softmax_over_selected()

    for g in range(n_k, n_chunks):
        advance(g)
        for h in range(ATT_HEADS):
            acc = acc_ref[h]
            for r in range(PAGES_PER_STEP):
                start = ((g - n_k) * PAGES_PER_STEP + r) * page_size
                acc = acc + _dot_t(s_ref[h, :, start:start + page_size], buf_ref[g % PAGE_SLOTS, r, h])
            acc_ref[h] = acc

    o_ref[...] = jnp.concatenate(
        [acc_ref[h] / l_ref[h][:, :ATT_HEAD_DIM] for h in range(ATT_HEADS)], axis=1)


def _sample_moba(layer, q8, kn, vn, page_table, cache_kt, cache_vt, bias_tiles, cfar, n_new):
    bsz = q8.shape[0]
    page_size = cache_kt.shape[4]
    n_pages = page_table.shape[1]
    n_k = n_pages // PAGES_PER_STEP
    tprev = jnp.transpose(bias_tiles[:, 1, :, :ROWS8], (0, 2, 1))
    town = jnp.transpose(bias_tiles[:, 0, :LANES, :ROWS8], (0, 2, 1))
    grid_spec = pltpu.PrefetchScalarGridSpec(
        num_scalar_prefetch=1,
        grid=(bsz,),
        in_specs=[pl.BlockSpec((None, ROWS8, ATT_WIDTH), lambda b, pt: (b, 0, 0)),
                  pl.BlockSpec((None, LANES, ATT_WIDTH), lambda b, pt: (b, 0, 0)),
                  pl.BlockSpec((None, LANES, ATT_WIDTH), lambda b, pt: (b, 0, 0)),
                  pl.BlockSpec((ATT_HEADS, ROWS8, MOBA_BLOCK), lambda b, pt: (0, 0, 0)),
                  pl.BlockSpec((ATT_HEADS, ROWS8, LANES), lambda b, pt: (0, 0, 0)),
                  pl.BlockSpec(memory_space=pltpu.SMEM),
                  pl.BlockSpec(memory_space=pl.ANY),
                  pl.BlockSpec(memory_space=pl.ANY)],
        out_specs=pl.BlockSpec((None, ROWS8, ATT_WIDTH), lambda b, pt: (b, 0, 0)),
        scratch_shapes=[pltpu.VMEM((PAGE_SLOTS, PAGES_PER_STEP, ATT_HEADS, ATT_HEAD_DIM, page_size), F32),
                        pltpu.SemaphoreType.DMA((PAGE_SLOTS,)),
                        pltpu.VMEM((ATT_HEADS, ROWS8, n_pages * page_size), F32),
                        pltpu.VMEM((ATT_HEADS, ROWS8, ATT_HEAD_DIM), F32),
                        pltpu.VMEM((ATT_HEADS, ROWS8, LANES), F32)],
    )
    return pl.pallas_call(
        functools.partial(_sample_moba_body, layer, n_new, n_k, page_size),
        grid_spec=grid_spec,
        out_shape=jax.ShapeDtypeStruct((bsz, ROWS8, ATT_WIDTH), F32),
        compiler_params=_params("arbitrary"),
        name="sample_moba",
    )(page_table, q8, kn, vn, tprev, town, cfar, cache_kt, cache_vt)


def _mix_ca_body(rows_per_batch, h_ref, ys_ref, ya_ref, mk_ref, mv_ref, wo1_ref, wo2_ref, g_ref,
                 wq_ref, wco_ref, o_ref):
    tm = h_ref.shape[0]
    h1 = (h_ref[...] + jnp.dot(ys_ref[...].astype(BF16), wo1_ref[...], preferred_element_type=F32)
          + jnp.dot(ya_ref[...].astype(BF16), wo2_ref[...], preferred_element_type=F32))
    hn = _rms(h1, g_ref[...]).astype(BF16)
    qq = jnp.dot(hn, wq_ref[...], preferred_element_type=F32) * (CA_HEAD_DIM ** -0.5)
    nkeys = mk_ref.shape[0] * mk_ref.shape[1]

    def head_rows(ref, hd):
        if len(ref.shape) == 3:
            return ref[:, :, hd * CA_HEAD_DIM:(hd + 1) * CA_HEAD_DIM].reshape(nkeys, CA_HEAD_DIM)
        return ref[:, :, hd, :].reshape(nkeys, CA_HEAD_DIM)

    if rows_per_batch < tm:
        r = lax.broadcasted_iota(jnp.int32, (tm, nkeys), 0) // rows_per_batch
        c = lax.broadcasted_iota(jnp.int32, (tm, nkeys), 1) // MEM_LEN
        same = r == c
    outs = []
    for hd in range(CA_HEADS):
        sl = slice(hd * CA_HEAD_DIM, (hd + 1) * CA_HEAD_DIM)
        s = _dot_t(qq[:, sl], head_rows(mk_ref, hd))
        if rows_per_batch < tm:
            s = jnp.where(same, s, NEG)
        s = s - jnp.max(s, axis=1, keepdims=True)
        p = jnp.exp(s)
        p = p / jnp.sum(p, axis=1, keepdims=True)
        outs.append(_dot(p, head_rows(mv_ref, hd)))
    o = jnp.concatenate(outs, axis=1).astype(BF16)
    o_ref[...] = h1 + jnp.dot(o, wco_ref[...], preferred_element_type=F32)


def _mix_ca(h, ys, ya, mk, mv, wo1, wo2, g_ca, wq, wco, rows_per_batch, tm, layer=0):
    m = h.shape[0]
    nbat = max(tm // rows_per_batch, 1)
    per = rows_per_batch // tm if rows_per_batch >= tm else 1
    if mk.ndim == 3:
        mem_spec = pl.BlockSpec((nbat, MEM_LEN, CA_WIDTH), lambda i: (i // per, 0, 0))
    else:
        mem_spec = pl.BlockSpec((None, nbat, MEM_LEN, CA_HEADS, CA_HEAD_DIM), lambda i: (layer, i // per, 0, 0, 0))

    def row(width):
        return pl.BlockSpec((tm, width), lambda i: (i, 0))

    return pl.pallas_call(
        functools.partial(_mix_ca_body, rows_per_batch),
        grid=(m // tm,),
        in_specs=[row(D_MODEL), row(SSD_WIDTH), row(ATT_WIDTH),
                  mem_spec, mem_spec,
                  _full(wo1.shape), _full(wo2.shape), _full((1, D_MODEL)), _full(wq.shape), _full(wco.shape)],
        out_specs=row(D_MODEL),
        out_shape=jax.ShapeDtypeStruct((m, D_MODEL), F32),
        compiler_params=_params("parallel"),
        name="mix_ca",
    )(h, ys, ya, mk, mv, wo1, wo2, g_ca.reshape(1, D_MODEL), wq, wco)


FF_CHUNK = 1024


def _mlp_body(final, h_ref, g_ref, wu_ref, wd_ref, gf_ref, o_ref):
    h = h_ref[...]
    xn = _rms(h, g_ref[...]).astype(BF16)
    acc = h
    for c in range(FF // FF_CHUNK):
        u = jnp.dot(xn, wu_ref[:, c * FF_CHUNK:(c + 1) * FF_CHUNK], preferred_element_type=F32)
        u = jnp.maximum(u, 0.0)
        acc = acc + jnp.dot((u * u).astype(BF16), wd_ref[c * FF_CHUNK:(c + 1) * FF_CHUNK, :],
                            preferred_element_type=F32)
    o_ref[...] = _rms(acc, gf_ref[...]) if final else acc


def _mlp(h, g, wu, wd, g_final, final, tm):
    m = h.shape[0]
    row = pl.BlockSpec((tm, D_MODEL), lambda i: (i, 0))
    return pl.pallas_call(
        functools.partial(_mlp_body, final),
        grid=(m // tm,),
        in_specs=[row, _full((1, D_MODEL)), _full(wu.shape), _full(wd.shape), _full((1, D_MODEL))],
        out_specs=row,
        out_shape=jax.ShapeDtypeStruct((m, D_MODEL), F32),
        compiler_params=_params("parallel"),
        name="mlp",
    )(h, g.reshape(1, D_MODEL), wu, wd, g_final.reshape(1, D_MODEL))


def _mix_ca_mlp_body(rows_per_batch, final, h_ref, ys_ref, ya_ref, mk_ref, mv_ref, wo1_ref, wo2_ref, g_ref,
                     wq_ref, wco_ref, gm_ref, wu_ref, wd_ref, gf_ref, o_ref, mid_ref):
    _mix_ca_body(rows_per_batch, h_ref, ys_ref, ya_ref, mk_ref, mv_ref, wo1_ref, wo2_ref, g_ref,
                 wq_ref, wco_ref, mid_ref)
    _mlp_body(final, mid_ref, gm_ref, wu_ref, wd_ref, gf_ref, o_ref)


def _mix_ca_mlp(h, ys, ya, mk, mv, wo1, wo2, g_ca, wq, wco, g_mlp, wu, wd, g_final, final, rows_per_batch, tm):
    m = h.shape[0]
    per = rows_per_batch // tm

    def row(width):
        return pl.BlockSpec((tm, width), lambda i: (i, 0))

    def resident(shape):
        return pl.BlockSpec(shape, lambda *_: (0,) * len(shape), pipeline_mode=pl.Buffered(1))

    mem_spec = pl.BlockSpec((1, MEM_LEN, CA_WIDTH), lambda i: (i // per, 0, 0))
    return pl.pallas_call(
        functools.partial(_mix_ca_mlp_body, rows_per_batch, final),
        grid=(m // tm,),
        in_specs=[row(D_MODEL), row(SSD_WIDTH), row(ATT_WIDTH), mem_spec, mem_spec,
                  resident(wo1.shape), resident(wo2.shape), resident((1, D_MODEL)), resident(wq.shape),
                  resident(wco.shape), resident((1, D_MODEL)), resident(wu.shape), resident(wd.shape),
                  resident((1, D_MODEL))],
        out_specs=row(D_MODEL),
        out_shape=jax.ShapeDtypeStruct((m, D_MODEL), F32),
        scratch_shapes=[pltpu.VMEM((tm, D_MODEL), F32)],
        compiler_params=_params("parallel"),
        name="mix_ca_mlp",
    )(h, ys, ya, mk, mv, wo1, wo2, g_ca.reshape(1, D_MODEL), wq, wco, g_mlp.reshape(1, D_MODEL), wu, wd,
      g_final.reshape(1, D_MODEL))


def _layer_weights(l, w_in, w_out, w_ca_q, w_ca_kv, w_ca_o, w_up, w_down):
    w = w_in[l]
    o = 0
    parts = []
    for width in (SSD_WIDTH, CONV_CH, SSD_HEADS, ATT_WIDTH, ATT_WIDTH, ATT_WIDTH):
        parts.append(w[:, o:o + width])
        o += width
    parts[2] = jnp.pad(parts[2], ((0, 0), (0, LANES - SSD_HEADS)))
    wo = w_out[l]
    w_in = [p.astype(BF16) for p in parts]
    return dict(
        w_in=w_in,
        w_in_prompt=w_in[:4] + [w_in[4].T, w_in[5].T],
        wo1=wo[:SSD_WIDTH].astype(BF16), wo2=wo[SSD_WIDTH:].astype(BF16),
        wq=w_ca_q[l].astype(BF16), wco=w_ca_o[l].astype(BF16),
        wkv=[w_ca_kv[l][:, :CA_WIDTH].astype(BF16), w_ca_kv[l][:, CA_WIDTH:].astype(BF16)],
        wu=w_up[l].astype(BF16), wd=w_down[l].astype(BF16))


def kernel(x_prompt, x_sample, cache_k, cache_v, cache_mem_k, cache_mem_v, state_conv, state_ssm,
           page_table, mem_prompt, norm_mix, w_in, conv_w, conv_b, dt_bias, a_log, d_skip, ssd_norm,
           rel_bias, w_out, norm_ca, norm_mem, w_ca_q, w_ca_kv, w_ca_o, norm_mlp, w_up, w_down, norm_final):
    depth = w_in.shape[0]
    pb, seq, _ = x_prompt.shape
    sb, dseq, _ = x_sample.shape
    page_size = cache_k.shape[2]
    tm = 512

    bias_tiles = _bias_tiles(rel_bias)
    cfar = rel_bias[REL_BUCKETS - 1]
    cache_kt = jnp.transpose(cache_k, (0, 1, 3, 4, 2))
    cache_vt = jnp.transpose(cache_v, (0, 1, 3, 4, 2))
    mem2d = mem_prompt.reshape(pb * MEM_LEN, D_MODEL)

    def pad_rows(a, rows):
        return jnp.pad(a.reshape(sb, dseq, ATT_WIDTH), ((0, 0), (0, rows - dseq), (0, 0)))

    h = x_prompt.reshape(pb * seq, D_MODEL)
    g = x_sample.reshape(sb * dseq, D_MODEL)
    outs = {n: [] for n in ("pmk", "pmv", "pconv", "pssm", "sk", "sv", "sconv", "sssm")}
    pk = jnp.zeros((depth, pb, ATT_WIDTH, seq), F32)
    pv = jnp.zeros((depth, pb, ATT_WIDTH, seq), F32)
    for l in range(depth):
        lw = _layer_weights(l, w_in, w_out, w_ca_q, w_ca_kv, w_ca_o, w_up, w_down)
        last = l == depth - 1

        z, xbc, dt, q, pk, pv = _norm_proj(h, norm_mix[l], lw["w_in_prompt"], tm, transposed=(4, 5), layer=l,
                                           depth=depth, seq=seq, prev=[pk, pv])
        y_ssd, c_new, s_new = _ssd(xbc, z, dt, None, None, conv_w[l], conv_b[l], dt_bias[l], a_log[l],
                                   d_skip[l], ssd_norm[l], pb, seq)
        y_att = _moba_prompt(q, pk, pv, bias_tiles, cfar, pb, seq, l)
        mk, mv = _norm_proj(mem2d, norm_mem[l], lw["wkv"], min(tm, pb * MEM_LEN))
        h = _mix_ca_mlp(h, y_ssd, y_att, mk.reshape(pb, MEM_LEN, CA_WIDTH), mv.reshape(pb, MEM_LEN, CA_WIDTH),
                        lw["wo1"], lw["wo2"], norm_ca[l], lw["wq"], lw["wco"], norm_mlp[l], lw["wu"], lw["wd"],
                        norm_final, last, seq, tm)
        outs["pmk"].append(mk.reshape(pb, MEM_LEN, CA_HEADS, CA_HEAD_DIM))
        outs["pmv"].append(mv.reshape(pb, MEM_LEN, CA_HEADS, CA_HEAD_DIM))
        outs["pconv"].append(c_new)
        outs["pssm"].append(s_new)

        ms = sb * dseq
        z, xbc, dt, q, k, v = _norm_proj(g, norm_mix[l], lw["w_in"], ms)
        y_ssd, c_new, s_new = _ssd(xbc, z, dt, state_conv, state_ssm, conv_w[l], conv_b[l], dt_bias[l],
                                   a_log[l], d_skip[l], ssd_norm[l], sb, dseq, layer=l)
        y_att = _sample_moba(l, pad_rows(q, ROWS8), pad_rows(k, LANES), pad_rows(v, LANES), page_table,
                             cache_kt, cache_vt, bias_tiles, cfar, dseq)[:, :dseq].reshape(ms, ATT_WIDTH)
        g = _mix_ca(g, y_ssd, y_att, cache_mem_k, cache_mem_v, lw["wo1"], lw["wo2"], norm_ca[l],
                    lw["wq"], lw["wco"], dseq, min(32, ms), layer=l)
        g = _mlp(g, norm_mlp[l], lw["wu"], lw["wd"], norm_final, last, ms)
        outs["sk"].append(k.reshape(sb, dseq, ATT_HEADS, ATT_HEAD_DIM))
        outs["sv"].append(v.reshape(sb, dseq, ATT_HEADS, ATT_HEAD_DIM))
        outs["sconv"].append(c_new)
        outs["sssm"].append(s_new)

    st = {n: jnp.stack(vs) for n, vs in outs.items() if vs}
    def token_major(a):
        return jnp.transpose(a.reshape(depth, pb, ATT_HEADS, ATT_HEAD_DIM, seq), (0, 1, 4, 2, 3))

    return (h.reshape(pb, seq, D_MODEL), g.reshape(sb, dseq, D_MODEL), token_major(pk), token_major(pv),
            st["pmk"], st["pmv"],
            st["pconv"], st["pssm"], st["sk"], st["sv"], st["sconv"], st["sssm"])
```
